```python
import jax, jax.numpy as jnp
from jax import lax
import numpy as np


D_MODEL = 2048
BATCH = 4
SEQ = 4096
DEPTH = 2

GRID_W = 64
CTX_LEN = 256
N_MIXERS = 2
FOURIER_GROUPS = 4
MLA_HEADS = 16
Q_LORA_RANK = 448
KV_LORA_RANK = 512
QK_NOPE_DIM = 128
QK_ROPE_DIM = 64
V_HEAD_DIM = 128
QK_HEAD_DIM = QK_NOPE_DIM + QK_ROPE_DIM
ROPE_PAIRS_PER_AXIS = QK_ROPE_DIM // 4
ROPE_THETA = 10000.0
Q_BLOCK = 128
N_EXPERTS = 32
TOP_K = 4
D_EXPERT = D_MODEL
SWIGLU_ALPHA = 1.702
SWIGLU_LIMIT = 7.0
MOE_BLOCK = 256
NORM_EPS = 1e-6

kernel_name = "hybrid_fourier_mla_moe_diffusion_block"


def _rmsnorm(x, g):
    xf = x.astype(jnp.float32)
    y = xf * lax.rsqrt(jnp.mean(xf * xf, axis=-1, keepdims=True) + NORM_EPS)
    return y.astype(x.dtype) * g


def _modulate(h, shift, scale):
    return h * (1 + scale) + shift


def _axial_rope_tables(n):
    rows = n // GRID_W
    row = jnp.repeat(jnp.arange(rows, dtype=jnp.float32), GRID_W)
    col = jnp.broadcast_to(jnp.arange(GRID_W, dtype=jnp.float32), (rows, GRID_W)).reshape(-1)
    inv = ROPE_THETA ** (-jnp.arange(ROPE_PAIRS_PER_AXIS, dtype=jnp.float32) / ROPE_PAIRS_PER_AXIS)
    ang = jnp.stack([row[:, None] * inv, col[:, None] * inv], axis=1)
    return jnp.cos(ang), jnp.sin(ang)


def _apply_rope(x, cos, sin):
    xs = x.astype(jnp.float32).reshape(x.shape[:-1] + (2, 2, ROPE_PAIRS_PER_AXIS))
    x1, x2 = xs[..., 0, :], xs[..., 1, :]
    cs = cos[None, :, None]
    sn = sin[None, :, None]
    out = jnp.stack([x1 * cs - x2 * sn, x2 * cs + x1 * sn], axis=-2)
    return out.reshape(x.shape).astype(x.dtype)


def _fourier_mix(h, w_out, b_out):
    b, n, d = h.shape
    hg = h.astype(jnp.float32).reshape(b, n, FOURIER_GROUPS, d // FOURIER_GROUPS)
    y = jnp.fft.fft2(hg, axes=(1, 3), norm="ortho").real
    return y.reshape(b, n, d).astype(h.dtype) @ w_out + b_out


def _mla_qkv(h, w_in, g_q_lora, w_q_up, g_kv_lora, w_kv_up, g_q_head, g_k_head, rope):
    b, n, _ = h.shape
    a = h @ w_in
    cq, ckv, k_pe = jnp.split(a, [Q_LORA_RANK, Q_LORA_RANK + KV_LORA_RANK], axis=-1)
    q = (_rmsnorm(cq, g_q_lora) @ w_q_up).reshape(b, n, MLA_HEADS, QK_HEAD_DIM)
    kv = (_rmsnorm(ckv, g_kv_lora) @ w_kv_up).reshape(b, n, MLA_HEADS, QK_NOPE_DIM + V_HEAD_DIM)
    k_nope, v = jnp.split(kv, [QK_NOPE_DIM], axis=-1)
    k_pe = jnp.broadcast_to(k_pe[:, :, None, :], (b, n, MLA_HEADS, QK_ROPE_DIM))
    k = jnp.concatenate([k_nope, k_pe], axis=-1)
    q = _rmsnorm(q, g_q_head)
    k = _rmsnorm(k, g_k_head)
    if rope is not None:
        cos, sin = rope
        q = jnp.concatenate([q[..., :QK_NOPE_DIM], _apply_rope(q[..., QK_NOPE_DIM:], cos, sin)], axis=-1)
        k = jnp.concatenate([k[..., :QK_NOPE_DIM], _apply_rope(k[..., QK_NOPE_DIM:], cos, sin)], axis=-1)
    return q, k, v


def _attend_dense(q, k, v):
    s = jnp.einsum("bqhd,bkhd->bhqk", q, k).astype(jnp.float32) * (QK_HEAD_DIM ** -0.5)
    p = jax.nn.softmax(s, axis=-1).astype(v.dtype)
    return jnp.einsum("bhqk,bkhd->bqhd", p, v)


def _attend_latent(q, k_lat, v_lat, k_ctx, v_ctx):
    b, n, h, dq = q.shape
    k = jnp.concatenate([k_ctx, k_lat], axis=1)
    v = jnp.concatenate([v_ctx, v_lat], axis=1)
    qb = q.reshape(b, n // Q_BLOCK, Q_BLOCK, h, dq).swapaxes(0, 1)
    o = lax.map(lambda qi: _attend_dense(qi, k, v), qb)
    return o.swapaxes(0, 1).reshape(b, n, h * V_HEAD_DIM)


def _mla_mixer(h_lat, h_ctx, cos, sin, w_in, g_q_lora, w_q_up, g_kv_lora, w_kv_up,
               g_q_head, g_k_head, w_out, with_ctx):
    q_l, k_l, v_l = _mla_qkv(h_lat, w_in, g_q_lora, w_q_up, g_kv_lora, w_kv_up,
                             g_q_head, g_k_head, (cos, sin))
    q_c, k_c, v_c = _mla_qkv(h_ctx, w_in, g_q_lora, w_q_up, g_kv_lora, w_kv_up,
                             g_q_head, g_k_head, None)
    y_lat = _attend_latent(q_l, k_l, v_l, k_c, v_c) @ w_out
    if not with_ctx:
        return y_lat, None
    b, l = h_ctx.shape[:2]
    y_ctx = _attend_dense(q_c, k_c, v_c).reshape(b, l, MLA_HEADS * V_HEAD_DIM) @ w_out
    return y_lat, y_ctx


def _clamped_swiglu(u):
    u = u.reshape(u.shape[:-1] + (D_EXPERT, 2))
    glu = jnp.minimum(u[..., 0], SWIGLU_LIMIT)
    lin = jnp.clip(u[..., 1], -SWIGLU_LIMIT, SWIGLU_LIMIT)
    return glu * jax.nn.sigmoid(SWIGLU_ALPHA * glu) * (lin + 1)


def _moe(h, w_router, b_router, w_up, b_up, w_down, b_down):
    t, d = h.shape
    logits = (h @ w_router + b_router).astype(jnp.float32)
    top_logit, top_idx = lax.top_k(logits, TOP_K)
    gates = jax.nn.softmax(top_logit, axis=-1)
    n_rows = t * TOP_K
    flat_expert = top_idx.reshape(-1)
    order = jnp.argsort(flat_expert)
    e_sorted = flat_expert[order]
    tok_sorted = order // TOP_K
    gate_sorted = gates.reshape(-1)[order]
    counts = jnp.bincount(flat_expert, length=N_EXPERTS)
    padded = (counts + MOE_BLOCK - 1) // MOE_BLOCK * MOE_BLOCK
    start = jnp.cumsum(counts) - counts
    pad_end = jnp.cumsum(padded)
    pad_start = pad_end - padded
    dest = pad_start[e_sorted] + (jnp.arange(n_rows) - start[e_sorted])
    n_blocks = -(-n_rows // MOE_BLOCK) + N_EXPERTS
    slot_tok = jnp.zeros((n_blocks * MOE_BLOCK,), jnp.int32).at[dest].set(tok_sorted.astype(jnp.int32))
    block_expert = jnp.minimum(
        jnp.searchsorted(pad_end, jnp.arange(n_blocks) * MOE_BLOCK, side="right"), N_EXPERTS - 1)

    def expert_block(args):
        tok, e = args
        u = h[tok] @ w_up[e] + b_up[e]
        return _clamped_swiglu(u) @ w_down[e] + b_down[e]

    y = lax.map(expert_block, (slot_tok.reshape(n_blocks, MOE_BLOCK), block_expert))
    y = y.reshape(-1, d)[dest] * gate_sorted[:, None].astype(h.dtype)
    return jnp.zeros_like(h).at[tok_sorted].add(y)


def setup_inputs(seed: int = 0) -> dict:
    key = jax.random.key(seed)
    ks = jax.random.split(key, 24)
    d = D_MODEL
    f32 = jnp.float32
    n_f = len(range(0, DEPTH, N_MIXERS))
    n_m = len(range(1, DEPTH, N_MIXERS))

    def nrm(k, shape, scale):
        return jax.random.normal(k, shape, f32) * scale

    def gain(k, shape):
        return 1.0 + 0.05 * jax.random.normal(k, shape, f32)

    return {
        "x": nrm(ks[0], (BATCH, SEQ, d), 1.0),
        "c": nrm(ks[1], (BATCH, d), 1.0),
        "ctx": nrm(ks[2], (BATCH, CTX_LEN, d), 1.0),
        "c_ctx": nrm(ks[3], (d,), 1.0),
        "w_mod": nrm(ks[4], (DEPTH, d, 6 * d), 0.5 * d ** -0.5),
        "b_mod": nrm(ks[5], (DEPTH, 6 * d), 0.02),
        "g_mix": gain(ks[6], (DEPTH, d)),
        "g_ffn": gain(ks[7], (DEPTH, d)),
        "fourier_w_out": nrm(ks[8], (n_f, d, d), d ** -0.5),
        "fourier_b_out": nrm(ks[9], (n_f, d), 0.02),
        "mla_w_in": nrm(ks[10], (n_m, d, Q_LORA_RANK + KV_LORA_RANK + QK_ROPE_DIM), d ** -0.5),
        "mla_g_q_lora": gain(ks[11], (n_m, Q_LORA_RANK)),
        "mla_w_q_up": nrm(ks[12], (n_m, Q_LORA_RANK, MLA_HEADS * QK_HEAD_DIM), Q_LORA_RANK ** -0.5),
        "mla_g_kv_lora": gain(ks[13], (n_m, KV_LORA_RANK)),
        "mla_w_kv_up": nrm(ks[14], (n_m, KV_LORA_RANK, MLA_HEADS * (QK_NOPE_DIM + V_HEAD_DIM)), KV_LORA_RANK ** -0.5),
        "mla_g_q_head": gain(ks[15], (n_m, QK_HEAD_DIM)),
        "mla_g_k_head": gain(ks[16], (n_m, QK_HEAD_DIM)),
        "mla_w_out": nrm(ks[17], (n_m, MLA_HEADS * V_HEAD_DIM, d), (MLA_HEADS * V_HEAD_DIM) ** -0.5),
        "router_w": nrm(ks[18], (DEPTH, d, N_EXPERTS), d ** -0.5),
        "router_b": nrm(ks[19], (DEPTH, N_EXPERTS), 0.01),
        "expert_w_up": nrm(ks[20], (DEPTH, N_EXPERTS, d, 2 * D_EXPERT), d ** -0.5),
        "expert_b_up": nrm(ks[21], (DEPTH, N_EXPERTS, 2 * D_EXPERT), 0.01),
        "expert_w_down": nrm(ks[22], (DEPTH, N_EXPERTS, D_EXPERT, d), D_EXPERT ** -0.5),
        "expert_b_down": nrm(ks[23], (DEPTH, N_EXPERTS, d), 0.01),
    }


def reference(x, c, ctx, c_ctx, w_mod, b_mod, g_mix, g_ffn, fourier_w_out, fourier_b_out,
              mla_w_in, mla_g_q_lora, mla_w_q_up, mla_g_kv_lora, mla_w_kv_up,
              mla_g_q_head, mla_g_k_head, mla_w_out, router_w, router_b,
              expert_w_up, expert_b_up, expert_w_down, expert_b_down):
    b, n, d = x.shape
    l = ctx.shape[1]
    cos, sin = _axial_rope_tables(n)
    x_lat, x_ctx = x, ctx
    for i in range(DEPTH):
        last = i == DEPTH - 1
        j = i // N_MIXERS
        mod_lat = (jax.nn.silu(c) @ w_mod[i] + b_mod[i])[:, None, :]
        mod_ctx = jax.nn.silu(c_ctx) @ w_mod[i] + b_mod[i]
        sh1, sc1, ga1, sh2, sc2, ga2 = jnp.split(mod_lat, 6, axis=-1)
        csh1, csc1, cga1, csh2, csc2, cga2 = jnp.split(mod_ctx, 6, axis=-1)

        h_lat = _modulate(_rmsnorm(x_lat, g_mix[i]), sh1, sc1)
        h_ctx = _modulate(_rmsnorm(x_ctx, g_mix[i]), csh1, csc1)
        if i % N_MIXERS == 0:
            y_lat = _fourier_mix(h_lat, fourier_w_out[j], fourier_b_out[j])
            y_ctx = None if last else _fourier_mix(h_ctx, fourier_w_out[j], fourier_b_out[j])
        else:
            y_lat, y_ctx = _mla_mixer(h_lat, h_ctx, cos, sin, mla_w_in[j], mla_g_q_lora[j],
                                      mla_w_q_up[j], mla_g_kv_lora[j], mla_w_kv_up[j],
                                      mla_g_q_head[j], mla_g_k_head[j], mla_w_out[j],
                                      not last)
        x_lat = x_lat + ga1 * y_lat

        f_lat = _modulate(_rmsnorm(x_lat, g_ffn[i]), sh2, sc2).reshape(b * n, d)
        if last:
            out = _moe(f_lat, router_w[i], router_b[i], expert_w_up[i], expert_b_up[i],
                       expert_w_down[i], expert_b_down[i])
            x_lat = x_lat + ga2 * out.reshape(b, n, d)
        else:
            x_ctx = x_ctx + cga1 * y_ctx
            f_ctx = _modulate(_rmsnorm(x_ctx, g_ffn[i]), csh2, csc2).reshape(b * l, d)
            out = _moe(jnp.concatenate([f_ctx, f_lat], axis=0), router_w[i], router_b[i],
                       expert_w_up[i], expert_b_up[i], expert_w_down[i], expert_b_down[i])
            x_ctx = x_ctx + cga2 * out[:b * l].reshape(b, l, d)
            x_lat = x_lat + ga2 * out[b * l:].reshape(b, n, d)
    return x_lat
```

```python
import functools
import math

import jax
import jax.numpy as jnp
from jax import lax
from jax.experimental import pallas as pl
from jax.experimental.pallas import tpu as pltpu

GRID_W = 64
FOURIER_GROUPS = 4
MLA_HEADS = 16
Q_LORA_RANK = 448
KV_LORA_RANK = 512
QK_NOPE_DIM = 128
QK_ROPE_DIM = 64
V_HEAD_DIM = 128
ROPE_THETA = 10000.0
TOP_K = 4
SWIGLU_ALPHA = 1.702
SWIGLU_LIMIT = 7.0
NORM_EPS = 1e-6

LANES = 128
MOE_ROWS = 256
HEAD_SLOT = 256
VMEM_LIMIT = 56 << 20

F32 = jnp.float32
BF16 = jnp.bfloat16
HIGHEST = lax.Precision.HIGHEST


def _cparams(*sem, vmem=VMEM_LIMIT):
    return pltpu.CompilerParams(dimension_semantics=sem, vmem_limit_bytes=vmem)


def _norm_mod(x, g, sh, sc):
    ms = jnp.mean(x * x, axis=-1, keepdims=True)
    return (x * lax.rsqrt(ms + NORM_EPS)) * g * (1.0 + sc) + sh


def _mod_spec(d, piece, row_fn, layer):
    return pl.BlockSpec((None, None, 1, d), lambda *ids: (layer, row_fn(*ids), 0, piece))


def _mod_kernel(c_ref, w_ref, b_ref, o_ref):
    c = c_ref[...]
    a = c * jax.nn.sigmoid(c)
    o_ref[...] = jnp.dot(a, w_ref[...], preferred_element_type=F32, precision=HIGHEST) + b_ref[...]


def _mod_vectors(c8, w_mod, b_mod):
    depth, d, n6 = w_mod.shape
    tn = next(t for t in (1024, 512, 256, 128) if n6 % t == 0)
    return pl.pallas_call(
        _mod_kernel,
        grid=(depth, n6 // tn),
        in_specs=[
            pl.BlockSpec((8, d), lambda l, j: (0, 0)),
            pl.BlockSpec((None, d, tn), lambda l, j: (l, 0, j)),
            pl.BlockSpec((None, 1, tn), lambda l, j: (l, 0, j)),
        ],
        out_specs=pl.BlockSpec((None, 8, tn), lambda l, j: (l, 0, j)),
        out_shape=jax.ShapeDtypeStruct((depth, 8, n6), F32),
        compiler_params=_cparams("arbitrary", "arbitrary"),
        name="mod_vectors",
    )(c8, w_mod, b_mod.reshape(depth, 1, n6))


def _chdft_kernel(x_ref, g_ref, sh_ref, sc_ref, cs_ref, a_ref, b_ref, *, groups):
    h = _norm_mod(x_ref[...], g_ref[...], sh_ref[...], sc_ref[...]).astype(BF16)
    dg = h.shape[-1] // groups
    for gi in range(groups):
        r = jnp.dot(h[:, gi * dg:(gi + 1) * dg], cs_ref[...], preferred_element_type=F32)
        a_ref[:, gi * dg:(gi + 1) * dg] = r[:, :dg].astype(BF16)
        b_ref[:, gi * dg:(gi + 1) * dg] = r[:, dg:].astype(BF16)


def _channel_dft(x2d, g, modv, layer, row_fn, cs, tm):
    t, d = x2d.shape
    spec = pl.BlockSpec((tm, d), lambda i: (i, 0))
    return pl.pallas_call(
        functools.partial(_chdft_kernel, groups=FOURIER_GROUPS),
        grid=(t // tm,),
        in_specs=[
            spec,
            pl.BlockSpec((1, d), lambda i: (0, 0)),
            _mod_spec(d, 0, row_fn, layer),
            _mod_spec(d, 1, row_fn, layer),
            pl.BlockSpec(cs.shape, lambda i: (0, 0)),
        ],
        out_specs=[spec, spec],
        out_shape=[jax.ShapeDtypeStruct((t, d), BF16)] * 2,
        compiler_params=_cparams("arbitrary"),
        name="channel_dft",
    )(x2d, g, modv, modv, cs)


def _seqdft_kernel(cn_ref, sn_ref, a_ref, b_ref, y_ref, acc_ref, *, scale):
    k = pl.program_id(2)

    @pl.when(k == 0)
    def _():
        acc_ref[...] = jnp.zeros_like(acc_ref)

    acc_ref[...] += (jnp.dot(cn_ref[...], a_ref[...], preferred_element_type=F32)
                     + jnp.dot(sn_ref[...], b_ref[...], preferred_element_type=F32))

    @pl.when(k == pl.num_programs(2) - 1)
    def _():
        y_ref[...] = (acc_ref[...] * scale).astype(y_ref.dtype)


def _seq_dft(a, b, cn, sn_neg, scale):
    bsz, n, d = a.shape
    tm = min(1024, n)
    tk = min(512, n)
    mat = pl.BlockSpec((tm, tk), lambda bi, i, k: (i, k))
    rhs = pl.BlockSpec((None, tk, d), lambda bi, i, k: (bi, k, 0))
    return pl.pallas_call(
        functools.partial(_seqdft_kernel, scale=scale),
        grid=(bsz, n // tm, n // tk),
        in_specs=[mat, mat, rhs, rhs],
        out_specs=pl.BlockSpec((None, tm, d), lambda bi, i, k: (bi, i, 0)),
        out_shape=jax.ShapeDtypeStruct((bsz, n, d), BF16),
        scratch_shapes=[pltpu.VMEM((tm, d), F32)],
        compiler_params=_cparams("arbitrary", "arbitrary", "arbitrary"),
        name="sequence_dft",
    )(cn, sn_neg, a, b)


def _mm_res_kernel(y_ref, w_ref, b_ref, r_ref, g_ref, o_ref):
    acc = jnp.dot(y_ref[...], w_ref[...], preferred_element_type=F32)
    o_ref[...] = r_ref[...] + g_ref[...] * (acc + b_ref[...])


def _mm_residual(y2d, w_bf, bias, res2d, modv, layer, piece, row_fn, tm):
    t, k = y2d.shape
    d = w_bf.shape[1]
    tn = min(1024, d)
    nj = d // tn
    return pl.pallas_call(
        _mm_res_kernel,
        grid=(nj, t // tm),
        in_specs=[
            pl.BlockSpec((tm, k), lambda j, i: (i, 0)),
            pl.BlockSpec((k, tn), lambda j, i: (0, j)),
            pl.BlockSpec((1, tn), lambda j, i: (0, j)),
            pl.BlockSpec((tm, tn), lambda j, i: (i, j)),
            pl.BlockSpec((None, None, 1, tn), lambda j, i: (layer, row_fn(i), 0, piece * nj + j)),
        ],
        out_specs=pl.BlockSpec((tm, tn), lambda j, i: (i, j)),
        out_shape=jax.ShapeDtypeStruct((t, d), F32),
        compiler_params=_cparams("arbitrary", "arbitrary"),
        name="mixer_out_residual",
    )(y2d, w_bf, bias, res2d, modv)


def _route_kernel(x_ref, g_ref, sh_ref, sc_ref, wr_ref, br_ref, f_ref, r_ref):
    f = _norm_mod(x_ref[...], g_ref[...], sh_ref[...], sc_ref[...])
    f_ref[...] = f
    logits = jnp.dot(f, wr_ref[...], preferred_element_type=F32, precision=HIGHEST) + br_ref[...]
    tm, ne = logits.shape
    col = lax.broadcasted_iota(jnp.int32, (tm, ne), 1).astype(F32)
    lane = lax.broadcasted_iota(jnp.int32, (tm, LANES), 1)
    out = jnp.zeros((tm, LANES), F32)
    vals = []
    for k in range(TOP_K):
        m = jnp.max(logits, axis=-1, keepdims=True)
        idx = jnp.min(jnp.where(logits == m, col, float(ne)), axis=-1, keepdims=True)
        logits = jnp.where(col == idx, -jnp.inf, logits)
        out = jnp.where(lane == k, idx, out)
        vals.append(m)
    es = [jnp.exp(v - vals[0]) for v in vals]
    den = es[0]
    for e in es[1:]:
        den = den + e
    for k in range(TOP_K):
        out = jnp.where(lane == TOP_K + k, es[k] / den, out)
    r_ref[...] = out


def _ffn_route(x2d, g, modv, layer, row_fn, w_router, b_router, tm):
    t, d = x2d.shape
    ne = w_router.shape[1]
    spec = pl.BlockSpec((tm, d), lambda i: (i, 0))
    return pl.pallas_call(
        _route_kernel,
        grid=(t // tm,),
        in_specs=[
            spec,
            pl.BlockSpec((1, d), lambda i: (0, 0)),
            _mod_spec(d, 3, row_fn, layer),
            _mod_spec(d, 4, row_fn, layer),
            pl.BlockSpec((d, ne), lambda i: (0, 0)),
            pl.BlockSpec((1, ne), lambda i: (0, 0)),
        ],
        out_specs=[spec, pl.BlockSpec((tm, LANES), lambda i: (i, 0))],
        out_shape=[jax.ShapeDtypeStruct((t, d), F32), jax.ShapeDtypeStruct((t, LANES), F32)],
        compiler_params=_cparams("arbitrary"),
        name="ffn_norm_route",
    )(x2d, g, modv, modv, w_router, b_router)


def _gather_kernel(tok_ref, f_hbm, xs_hbm, sem, *, rows):
    base = pl.program_id(0) * rows

    def issue(r, _):
        pltpu.make_async_copy(f_hbm.at[pl.ds(tok_ref[0, 0, r], 1)],
                              xs_hbm.at[pl.ds(base + r, 1)], sem).start()
        return 0

    lax.fori_loop(0, rows, issue, 0)

    def drain(r, _):
        pltpu.make_async_copy(f_hbm.at[pl.ds(0, 1)], xs_hbm.at[pl.ds(base, 1)], sem).wait()
        return 0

    lax.fori_loop(0, rows, drain, 0)


def _moe_gather(f2d, slot_tok):
    s = slot_tok.shape[0]
    rows = 2 * MOE_ROWS if (s // MOE_ROWS) % 2 == 0 else MOE_ROWS
    d = f2d.shape[1]
    return pl.pallas_call(
        functools.partial(_gather_kernel, rows=rows),
        grid=(s // rows,),
        in_specs=[
            pl.BlockSpec((1, 1, rows), lambda i: (i, 0, 0), memory_space=pltpu.SMEM),
            pl.BlockSpec(memory_space=pl.ANY),
        ],
        out_specs=pl.BlockSpec(memory_space=pl.ANY),
        out_shape=jax.ShapeDtypeStruct((s, d), f2d.dtype),
        scratch_shapes=[pltpu.SemaphoreType.DMA(())],
        compiler_params=_cparams("arbitrary"),
        name="moe_gather",
    )(slot_tok.reshape(s // rows, 1, rows), f2d)


def _moe_up_kernel(be_ref, nu_ref, x_ref, wg_ref, wl_ref, bg_ref, bl_ref, o_ref):
    b = pl.program_id(1)

    @pl.when(b < nu_ref[0])
    def _():
        x = x_ref[...].astype(BF16)
        glu = jnp.dot(x, wg_ref[...], preferred_element_type=F32) + bg_ref[...]
        lin = jnp.dot(x, wl_ref[...], preferred_element_type=F32) + bl_ref[...]
        glu = jnp.minimum(glu, SWIGLU_LIMIT)
        lin = jnp.clip(lin, -SWIGLU_LIMIT, SWIGLU_LIMIT)
        o_ref[...] = (glu * jax.nn.sigmoid(SWIGLU_ALPHA * glu) * (lin + 1.0)).astype(o_ref.dtype)

    @pl.when(b >= nu_ref[0])
    def _():
        o_ref[...] = jnp.zeros_like(o_ref)


def _moe_up(xs, block_expert, n_used, w_glu, w_lin, b_glu, b_lin):
    s, d = xs.shape
    ne, _, de = w_glu.shape
    nb = s // MOE_ROWS
    tn = min(1024, de)
    wspec = pl.BlockSpec((None, d, tn), lambda j, b, be, nu: (be[b], 0, j))
    bspec = pl.BlockSpec((None, 1, tn), lambda j, b, be, nu: (be[b], 0, j))
    return pl.pallas_call(
        _moe_up_kernel,
        grid_spec=pltpu.PrefetchScalarGridSpec(
            num_scalar_prefetch=2,
            grid=(de // tn, nb),
            in_specs=[pl.BlockSpec((MOE_ROWS, d), lambda j, b, be, nu: (b, 0)),
                      wspec, wspec, bspec, bspec],
            out_specs=pl.BlockSpec((MOE_ROWS, tn), lambda j, b, be, nu: (b, j)),
        ),
        out_shape=jax.ShapeDtypeStruct((s, de), BF16),
        compiler_params=_cparams("arbitrary", "arbitrary"),
        name="moe_up_swiglu",
    )(block_expert, n_used, xs, w_glu, w_lin, b_glu, b_lin)


def _moe_down_kernel(be_ref, nu_ref, a_ref, w_ref, b_ref, o_ref):
    b = pl.program_id(0)

    @pl.when(b < nu_ref[0])
    def _():
        o_ref[...] = jnp.dot(a_ref[...], w_ref[...], preferred_element_type=F32) + b_ref[...]

    @pl.when(b >= nu_ref[0])
    def _():
        o_ref[...] = jnp.zeros_like(o_ref)


def _moe_down(act, block_expert, n_used, w_down, b_down):
    s, de = act.shape
    ne, _, d = w_down.shape
    nb = s // MOE_ROWS
    return pl.pallas_call(
        _moe_down_kernel,
        grid_spec=pltpu.PrefetchScalarGridSpec(
            num_scalar_prefetch=2,
            grid=(nb,),
            in_specs=[pl.BlockSpec((MOE_ROWS, de), lambda b, be, nu: (b, 0)),
                      pl.BlockSpec((None, de, d), lambda b, be, nu: (be[b], 0, 0)),
                      pl.BlockSpec((None, 1, d), lambda b, be, nu: (be[b], 0, 0))],
            out_specs=pl.BlockSpec((MOE_ROWS, d), lambda b, be, nu: (b, 0)),
        ),
        out_shape=jax.ShapeDtypeStruct((s, d), F32),
        compiler_params=_cparams("arbitrary"),
        name="moe_down",
    )(block_expert, n_used, act, w_down, b_down)


def _combine_kernel(dest_ref, ys_hbm, r_ref, x_ref, g_ref, o_ref, buf, sem, *, tm):
    def issue(r, _):
        for k in range(TOP_K):
            pltpu.make_async_copy(ys_hbm.at[pl.ds(dest_ref[0, 0, r * TOP_K + k], 1)],
                                  buf.at[k, pl.ds(r, 1)], sem).start()
        return 0

    lax.fori_loop(0, tm, issue, 0)

    def drain(r, _):
        pltpu.make_async_copy(ys_hbm.at[pl.ds(0, 1)], buf.at[0, pl.ds(0, 1)], sem).wait()
        return 0

    lax.fori_loop(0, tm * TOP_K, drain, 0)
    route = r_ref[...]
    acc = route[:, TOP_K:TOP_K + 1] * buf[0]
    for k in range(1, TOP_K):
        acc = acc + route[:, TOP_K + k:TOP_K + k + 1] * buf[k]
    o_ref[...] = x_ref[...] + g_ref[...] * acc


def _moe_combine(ys, dest, route, x2d, modv, layer, row_fn, tok_off, tm):
    t, d = x2d.shape
    off = tok_off // tm
    return pl.pallas_call(
        functools.partial(_combine_kernel, tm=tm),
        grid=(t // tm,),
        in_specs=[
            pl.BlockSpec((1, 1, tm * TOP_K), lambda i: (i + off, 0, 0), memory_space=pltpu.SMEM),
            pl.BlockSpec(memory_space=pl.ANY),
            pl.BlockSpec((tm, LANES), lambda i: (i + off, 0)),
            pl.BlockSpec((tm, d), lambda i: (i, 0)),
            _mod_spec(d, 5, lambda i: row_fn(i + off), layer),
        ],
        out_specs=pl.BlockSpec((tm, d), lambda i: (i, 0)),
        out_shape=jax.ShapeDtypeStruct((t, d), F32),
        scratch_shapes=[pltpu.VMEM((TOP_K, tm, d), F32), pltpu.SemaphoreType.DMA(())],
        compiler_params=_cparams("arbitrary"),
        name="moe_combine",
    )(dest.reshape(-1, 1, tm * TOP_K), ys, route, x2d, modv)


def _routing_tables(route, ne):
    t = route.shape[0]
    flat_e = route[:, :TOP_K].astype(jnp.int32).reshape(-1)
    n_rows = t * TOP_K
    onehot = (flat_e[:, None] == jnp.arange(ne, dtype=jnp.int32)[None, :]).astype(jnp.int32)
    csum = jnp.cumsum(onehot, axis=0)
    rank = jnp.take_along_axis(csum, flat_e[:, None], axis=1)[:, 0] - 1
    counts = csum[-1]
    padded = (counts + MOE_ROWS - 1) // MOE_ROWS * MOE_ROWS
    pad_end = jnp.cumsum(padded)
    pad_start = pad_end - padded
    dest = pad_start[flat_e] + rank
    n_blocks = -(-n_rows // MOE_ROWS) + ne
    slot_tok = jnp.zeros((n_blocks * MOE_ROWS,), jnp.int32).at[dest].set(
        jnp.arange(n_rows, dtype=jnp.int32) // TOP_K)
    block_expert = jnp.minimum(
        jnp.searchsorted(pad_end, jnp.arange(n_blocks, dtype=jnp.int32) * MOE_ROWS, side="right"),
        ne - 1).astype(jnp.int32)
    n_used = (pad_end[-1:] // MOE_ROWS).astype(jnp.int32)
    return dest.astype(jnp.int32), slot_tok, block_expert, n_used


def _moe_experts(f2d, route, w_glu, w_lin, b_glu, b_lin, w_down, b_down):
    ne = w_glu.shape[0]
    dest, slot_tok, block_expert, n_used = _routing_tables(route, ne)
    xs = _moe_gather(f2d, slot_tok)
    act = _moe_up(xs, block_expert, n_used, w_glu, w_lin, b_glu, b_lin)
    ys = _moe_down(act, block_expert, n_used, w_down, b_down)
    return ys, dest


def _mla_proj_kernel(x_ref, g_ref, sh_ref, sc_ref, win_ref, gq_ref, gkv_ref, wq_ref, wkv_ref,
                     q_ref, kv_ref, kpe_ref, *, qp, kvp):
    h = _norm_mod(x_ref[...], g_ref[...], sh_ref[...], sc_ref[...]).astype(BF16)
    a = jnp.dot(h, win_ref[...], preferred_element_type=F32)
    cq = a[:, :qp]
    ckv = a[:, qp:qp + kvp]
    kpe_ref[...] = a[:, qp + kvp:]
    cqn = cq * lax.rsqrt(jnp.sum(cq * cq, axis=-1, keepdims=True) * (1.0 / Q_LORA_RANK) + NORM_EPS)
    ckvn = ckv * lax.rsqrt(jnp.sum(ckv * ckv, axis=-1, keepdims=True) * (1.0 / KV_LORA_RANK)
                           + NORM_EPS)
    q_ref[...] = jnp.dot((cqn * gq_ref[...]).astype(BF16), wq_ref[...], preferred_element_type=F32)
    kv_ref[...] = jnp.dot((ckvn * gkv_ref[...]).astype(BF16), wkv_ref[...],
                          preferred_element_type=F32)


def _mla_proj(x2d, g, modv, layer, row_fn, win_p, gq_p, gkv_p, wq_p, wkv_p, tm):
    t, d = x2d.shape
    qp, kvp = gq_p.shape[1], gkv_p.shape[1]
    na = win_p.shape[1]
    nq, nkv = wq_p.shape[1], wkv_p.shape[1]
    full = lambda arr: pl.BlockSpec(arr.shape, lambda i: (0, 0))
    return pl.pallas_call(
        functools.partial(_mla_proj_kernel, qp=qp, kvp=kvp),
        grid=(t // tm,),
        in_specs=[
            pl.BlockSpec((tm, d), lambda i: (i, 0)),
            pl.BlockSpec((1, d), lambda i: (0, 0)),
            _mod_spec(d, 0, row_fn, layer),
            _mod_spec(d, 1, row_fn, layer),
            full(win_p), full(gq_p), full(gkv_p), full(wq_p), full(wkv_p),
        ],
        out_specs=[pl.BlockSpec((tm, nq), lambda i: (i, 0)),
                   pl.BlockSpec((tm, nkv), lambda i: (i, 0)),
                   pl.BlockSpec((tm, na - qp - kvp), lambda i: (i, 0))],
        out_shape=[jax.ShapeDtypeStruct((t, nq), F32), jax.ShapeDtypeStruct((t, nkv), F32),
                   jax.ShapeDtypeStruct((t, na - qp - kvp), F32)],
        compiler_params=_cparams("arbitrary"),
        name="mla_projections",
    )(x2d, g, modv, modv, win_p, gq_p, gkv_p, wq_p, wkv_p)


def _rope(x, cos, sin_lo, sin_hi):
    quarter = QK_ROPE_DIM // 4
    return (x * cos + pltpu.roll(x, LANES - quarter, 1) * sin_lo
            + pltpu.roll(x, quarter, 1) * sin_hi)


def _headnorm_kernel(q_ref, kv_ref, kpe_ref, gq_ref, gk_ref, cos_ref, slo_ref, shi_ref,
                     qo_ref, ko_ref, vo_ref, *, heads, q_scale):
    inv = 1.0 / (QK_NOPE_DIM + QK_ROPE_DIM)
    cos, slo, shi = cos_ref[...], slo_ref[...], shi_ref[...]
    kpe = kpe_ref[...]
    kpe_ss = jnp.sum(kpe * kpe, axis=-1, keepdims=True)
    gq = gq_ref[...]
    gk = gk_ref[...]
    for h in range(heads):
        qh = q_ref[:, h * HEAD_SLOT:(h + 1) * HEAD_SLOT]
        rs = lax.rsqrt(jnp.sum(qh * qh, axis=-1, keepdims=True) * inv + NORM_EPS) * q_scale
        qn = qh * rs * gq
        qo_ref[h, :, :LANES] = qn[:, :LANES].astype(BF16)
        qo_ref[h, :, LANES:] = _rope(qn[:, LANES:], cos, slo, shi).astype(BF16)
        kn = kv_ref[:, h * HEAD_SLOT:h * HEAD_SLOT + QK_NOPE_DIM]
        rk = lax.rsqrt((jnp.sum(kn * kn, axis=-1, keepdims=True) + kpe_ss) * inv + NORM_EPS)
        ko_ref[h, :, :LANES] = (kn * rk * gk[:, :LANES]).astype(BF16)
        ko_ref[h, :, LANES:] = _rope(kpe * rk * gk[:, LANES:], cos, slo, shi).astype(BF16)
        vo_ref[h] = kv_ref[:, h * HEAD_SLOT + QK_NOPE_DIM:(h + 1) * HEAD_SLOT].astype(BF16)


def _head_norm_rope(q_raw, kv_raw, kpe, gq_slot, gk_slot, cos, slo, shi, bsz, n_ctx, n_lat, tm):
    heads = MLA_HEADS
    n_keys = n_ctx + n_lat
    ctx_tiles = bsz * n_ctx // tm
    per_ctx = n_ctx // tm
    per_lat = n_lat // tm

    def batch_of(i):
        return jnp.where(i < ctx_tiles, i // per_ctx, (i - ctx_tiles) // per_lat)

    def key_blk(i):
        return jnp.where(i < ctx_tiles, i % per_ctx, per_ctx + (i - ctx_tiles) % per_lat)

    tab = pl.BlockSpec((tm, LANES), lambda i: (key_blk(i), 0))
    t = q_raw.shape[0]
    out_map = lambda i: (batch_of(i), 0, key_blk(i), 0)
    return pl.pallas_call(
        functools.partial(_headnorm_kernel, heads=heads,
                          q_scale=float(QK_NOPE_DIM + QK_ROPE_DIM) ** -0.5),
        grid=(t // tm,),
        in_specs=[
            pl.BlockSpec((tm, heads * HEAD_SLOT), lambda i: (i, 0)),
            pl.BlockSpec((tm, heads * HEAD_SLOT), lambda i: (i, 0)),
            pl.BlockSpec((tm, LANES), lambda i: (i, 0)),
            pl.BlockSpec((1, HEAD_SLOT), lambda i: (0, 0)),
            pl.BlockSpec((1, HEAD_SLOT), lambda i: (0, 0)),
            tab, tab, tab,
        ],
        out_specs=[pl.BlockSpec((None, heads, tm, HEAD_SLOT), out_map),
                   pl.BlockSpec((None, heads, tm, HEAD_SLOT), out_map),
                   pl.BlockSpec((None, heads, tm, V_HEAD_DIM), out_map)],
        out_shape=[jax.ShapeDtypeStruct((bsz, heads, n_keys, HEAD_SLOT), BF16),
                   jax.ShapeDtypeStruct((bsz, heads, n_keys, HEAD_SLOT), BF16),
                   jax.ShapeDtypeStruct((bsz, heads, n_keys, V_HEAD_DIM), BF16)],
        compiler_params=_cparams("arbitrary"),
        name="mla_headnorm_rope",
    )(q_raw, kv_raw, kpe, gq_slot, gk_slot, cos, slo, shi)


def _attn_kernel(q_ref, k_ref, v_ref, o_ref):
    s = lax.dot_general(q_ref[...], k_ref[...], (((1,), (1,)), ((), ())),
                        preferred_element_type=F32)
    m = jnp.max(s, axis=-1, keepdims=True)
    p = jnp.exp(s - m)
    l = jnp.sum(p, axis=-1, keepdims=True)
    o = jnp.dot(p.astype(BF16), v_ref[...], preferred_element_type=F32)
    o_ref[...] = (o / l).astype(o_ref.dtype)


def _attention(q, k, v, n_ctx, tq):
    bsz, heads, n_keys, _ = k.shape
    n_lat = n_keys - n_ctx
    q_off = n_ctx // tq
    return pl.pallas_call(
        _attn_kernel,
        grid=(bsz, heads, n_lat // tq),
        in_specs=[
            pl.BlockSpec((None, None, tq, HEAD_SLOT), lambda b, h, i: (b, h, i + q_off, 0)),
            pl.BlockSpec((None, None, n_keys, HEAD_SLOT), lambda b, h, i: (b, h, 0, 0)),
            pl.BlockSpec((None, None, n_keys, V_HEAD_DIM), lambda b, h, i: (b, h, 0, 0)),
        ],
        out_specs=pl.BlockSpec((None, tq, V_HEAD_DIM), lambda b, h, i: (b, i, h)),
        out_shape=jax.ShapeDtypeStruct((bsz, n_lat, heads * V_HEAD_DIM), BF16),
        compiler_params=_cparams("arbitrary", "arbitrary", "arbitrary"),
        name="mla_attention",
    )(q, k, v)


def _dft_tables(n):
    k = jnp.arange(n, dtype=jnp.int32)
    ang = ((k[:, None] * k[None, :]) % n).astype(F32) * (2.0 * math.pi / n)
    return jnp.cos(ang), jnp.sin(ang)


def _rope_tables(n_ctx, n_lat):
    rows = n_lat // GRID_W
    pairs = QK_ROPE_DIM // 4
    row = jnp.repeat(jnp.arange(rows, dtype=F32), GRID_W)
    col = jnp.tile(jnp.arange(GRID_W, dtype=F32), rows)
    inv = ROPE_THETA ** (-jnp.arange(pairs, dtype=F32) / pairs)
    ang = jnp.stack([row[:, None] * inv, col[:, None] * inv], axis=1)
    cos, sin = jnp.cos(ang), jnp.sin(ang)
    zero = jnp.zeros_like(sin)
    cos_t = jnp.stack([cos, cos], axis=2).reshape(n_lat, QK_ROPE_DIM)
    slo_t = jnp.stack([-sin, zero], axis=2).reshape(n_lat, QK_ROPE_DIM)
    shi_t = jnp.stack([zero, sin], axis=2).reshape(n_lat, QK_ROPE_DIM)

    def full(tab, ctx_val):
        tab = jnp.pad(tab, ((0, 0), (0, LANES - QK_ROPE_DIM)))
        return jnp.concatenate([jnp.full((n_ctx, LANES), ctx_val, F32), tab], axis=0)

    return full(cos_t, 1.0), full(slo_t, 0.0), full(shi_t, 0.0)


def _pad_cols(w, n):
    return jnp.pad(w, ((0, 0), (0, n - w.shape[1])))


def _mla_weights(w_in, g_q_lora, w_q_up, g_kv_lora, w_kv_up, g_q_head, g_k_head):
    qp = -(-Q_LORA_RANK // LANES) * LANES
    kvp = -(-KV_LORA_RANK // LANES) * LANES
    qr, kvr = Q_LORA_RANK, KV_LORA_RANK
    win_p = jnp.concatenate([_pad_cols(w_in[:, :qr], qp), _pad_cols(w_in[:, qr:qr + kvr], kvp),
                             _pad_cols(w_in[:, qr + kvr:], LANES)], axis=1).astype(BF16)
    gq_p = _pad_cols(g_q_lora[None, :], qp)
    gkv_p = _pad_cols(g_kv_lora[None, :], kvp)
    hd = QK_NOPE_DIM + QK_ROPE_DIM
    wq = w_q_up.reshape(qr, MLA_HEADS, hd)
    wq = jnp.pad(wq, ((0, qp - qr), (0, 0), (0, HEAD_SLOT - hd)))
    wq_p = wq.reshape(qp, MLA_HEADS * HEAD_SLOT).astype(BF16)
    wkv_p = jnp.pad(w_kv_up, ((0, kvp - kvr), (0, 0))).astype(BF16)
    gq_slot = _pad_cols(g_q_head[None, :], HEAD_SLOT)
    gk_slot = _pad_cols(g_k_head[None, :], HEAD_SLOT)
    return win_p, gq_p, gkv_p, wq_p, wkv_p, gq_slot, gk_slot


def _expert_weights(w_up, b_up, w_down, b_down):
    ne, d, de2 = w_up.shape
    w = w_up.reshape(ne, d, de2 // 2, 2)
    b = b_up.reshape(ne, 1, de2 // 2, 2)
    return (w[..., 0].astype(BF16), w[..., 1].astype(BF16), b[..., 0], b[..., 1],
            w_down.astype(BF16), b_down.reshape(ne, 1, -1))


def kernel(x, c, ctx, c_ctx, w_mod, b_mod, g_mix, g_ffn, fourier_w_out, fourier_b_out, mla_w_in, mla_g_q_lora, mla_w_q_up, mla_g_kv_lora, mla_w_kv_up, mla_g_q_head, mla_g_k_head, mla_w_out, router_w, router_b, expert_w_up, expert_b_up, expert_w_down, expert_b_down):
    bsz, n, d = x.shape
    l = ctx.shape[1]
    assert bsz < 8 and n % GRID_W == 0
    tl = min(256, l)
    assert l % tl == 0 and n % tl == 0
    ctx_row = bsz

    c8 = jnp.zeros((8, d), F32).at[:bsz].set(c).at[ctx_row].set(c_ctx)
    modv = _mod_vectors(c8, w_mod, b_mod).reshape(w_mod.shape[0], 8, 1, 6 * d)

    def lat_row(tm):
        return lambda i: i // (n // tm)

    def all_row(tm):
        nct = bsz * l // tm
        return lambda i: jnp.where(i < nct, ctx_row, (i - nct) // (n // tm))

    def ctx_only_row(i):
        return ctx_row

    dg = d // FOURIER_GROUPS
    cc, sc_ = _dft_tables(dg)
    cs = jnp.concatenate([cc, sc_], axis=1).astype(BF16)
    wf = fourier_w_out[0].astype(BF16)
    bf = fourier_b_out[0][None, :]
    g0 = g_mix[0][None, :]

    def fourier(x3, row_fn_of, tm):
        bb, nn, _ = x3.shape
        x2 = x3.reshape(bb * nn, d)
        a, b = _channel_dft(x2, g0, modv, 0, row_fn_of, cs, tm)
        cn, sn = _dft_tables(nn)
        y = _seq_dft(a.reshape(bb, nn, d), b.reshape(bb, nn, d), cn.astype(BF16),
                     (-sn).astype(BF16), 1.0 / math.sqrt(nn * dg))
        return _mm_residual(y.reshape(bb * nn, d), wf, bf, x2, modv, 0, 2, row_fn_of, tm)

    tm_lat = min(512, n)
    x_lat1 = fourier(x, lat_row(tm_lat), tm_lat)
    x_ctx1 = fourier(ctx, ctx_only_row, tl)
    x_all = jnp.concatenate([x_ctx1, x_lat1], axis=0)
    n_ctx_tok = bsz * l

    ew0 = _expert_weights(expert_w_up[0], expert_b_up[0], expert_w_down[0], expert_b_down[0])
    f0, route0 = _ffn_route(x_all, g_ffn[0][None, :], modv, 0, all_row(tl), router_w[0],
                            router_b[0][None, :], tl)
    ys0, dest0 = _moe_experts(f0, route0, *ew0)
    tc = min(128, tl)
    x_all2 = _moe_combine(ys0, dest0, route0, x_all, modv, 0, all_row(tc), 0, tc)

    win_p, gq_p, gkv_p, wq_p, wkv_p, gq_slot, gk_slot = _mla_weights(
        mla_w_in[0], mla_g_q_lora[0], mla_w_q_up[0], mla_g_kv_lora[0], mla_w_kv_up[0],
        mla_g_q_head[0], mla_g_k_head[0])
    q_raw, kv_raw, kpe = _mla_proj(x_all2, g_mix[1][None, :], modv, 1, all_row(tl),
                                   win_p, gq_p, gkv_p, wq_p, wkv_p, tl)
    cos, slo, shi = _rope_tables(l, n)
    qh, kh, vh = _head_norm_rope(q_raw, kv_raw, kpe, gq_slot, gk_slot, cos, slo, shi,
                                 bsz, l, n, tl)
    attn = _attention(qh, kh, vh, l, tl)
    x_lat2 = x_all2[n_ctx_tok:]
    zero_bias = jnp.zeros((1, d), F32)
    x_lat3 = _mm_residual(attn.reshape(bsz * n, -1), mla_w_out[0].astype(BF16), zero_bias,
                          x_lat2, modv, 1, 2, lat_row(tm_lat), tm_lat)

    ew1 = _expert_weights(expert_w_up[1], expert_b_up[1], expert_w_down[1], expert_b_down[1])
    f1, route1 = _ffn_route(x_lat3, g_ffn[1][None, :], modv, 1, lat_row(tl), router_w[1],
                            router_b[1][None, :], tl)
    ys1, dest1 = _moe_experts(f1, route1, *ew1)
    out = _moe_combine(ys1, dest1, route1, x_lat3, modv, 1, lat_row(tc), 0, tc)
    return out.reshape(bsz, n, d)
```

```python
import functools
import math

import jax
import jax.numpy as jnp
from jax import lax
from jax.experimental import pallas as pl
from jax.experimental.pallas import tpu as pltpu

GRID_W = 64
FOURIER_GROUPS = 4
MLA_HEADS = 16
Q_LORA_RANK = 448
KV_LORA_RANK = 512
QK_NOPE_DIM = 128
QK_ROPE_DIM = 64
V_HEAD_DIM = 128
ROPE_THETA = 10000.0
TOP_K = 4
SWIGLU_ALPHA = 1.702
SWIGLU_LIMIT = 7.0
NORM_EPS = 1e-6

LANES = 128
MOE_ROWS = 256
HEAD_SLOT = 256
VMEM_LIMIT = 56 << 20

F32 = jnp.float32
BF16 = jnp.bfloat16
HIGHEST = lax.Precision.HIGHEST


def _cparams(*sem, vmem=VMEM_LIMIT):
    return pltpu.CompilerParams(dimension_semantics=sem, vmem_limit_bytes=vmem)


def _norm_mod(x, g, sh, sc):
    ms = jnp.mean(x * x, axis=-1, keepdims=True)
    return (x * lax.rsqrt(ms + NORM_EPS)) * g * (1.0 + sc) + sh


def _mod_spec(d, piece, row_fn, layer):
    return pl.BlockSpec((None, None, 1, d), lambda *ids: (layer, row_fn(*ids), 0, piece))


def _mod_kernel(c_ref, w_ref, b_ref, o_ref):
    c = c_ref[...]
    a = c * jax.nn.sigmoid(c)
    o_ref[...] = jnp.dot(a, w_ref[...], preferred_element_type=F32, precision=HIGHEST) + b_ref[...]


def _mod_vectors(c8, w_mod, b_mod):
    depth, d, n6 = w_mod.shape
    tn = next(t for t in (1024, 512, 256, 128) if n6 % t == 0)
    return pl.pallas_call(
        _mod_kernel,
        grid=(depth, n6 // tn),
        in_specs=[
            pl.BlockSpec((8, d), lambda l, j: (0, 0)),
            pl.BlockSpec((None, d, tn), lambda l, j: (l, 0, j)),
            pl.BlockSpec((None, 1, tn), lambda l, j: (l, 0, j)),
        ],
        out_specs=pl.BlockSpec((None, 8, tn), lambda l, j: (l, 0, j)),
        out_shape=jax.ShapeDtypeStruct((depth, 8, n6), F32),
        compiler_params=_cparams("arbitrary", "arbitrary"),
        name="mod_vectors",
    )(c8, w_mod, b_mod.reshape(depth, 1, n6))


def _chdft_kernel(x_ref, g_ref, sh_ref, sc_ref, cs_ref, a_ref, b_ref, *, groups):
    h = _norm_mod(x_ref[...], g_ref[...], sh_ref[...], sc_ref[...]).astype(BF16)
    dg = h.shape[-1] // groups
    for gi in range(groups):
        r = jnp.dot(h[:, gi * dg:(gi + 1) * dg], cs_ref[...], preferred_element_type=F32)
        a_ref[:, gi * dg:(gi + 1) * dg] = r[:, :dg].astype(BF16)
        b_ref[:, gi * dg:(gi + 1) * dg] = r[:, dg:].astype(BF16)


def _channel_dft(x2d, g, modv, layer, row_fn, cs, tm):
    t, d = x2d.shape
    spec = pl.BlockSpec((tm, d), lambda i: (i, 0))
    return pl.pallas_call(
        functools.partial(_chdft_kernel, groups=FOURIER_GROUPS),
        grid=(t // tm,),
        in_specs=[
            spec,
            pl.BlockSpec((1, d), lambda i: (0, 0)),
            _mod_spec(d, 0, row_fn, layer),
            _mod_spec(d, 1, row_fn, layer),
            pl.BlockSpec(cs.shape, lambda i: (0, 0)),
        ],
        out_specs=[spec, spec],
        out_shape=[jax.ShapeDtypeStruct((t, d), BF16)] * 2,
        compiler_params=_cparams("arbitrary"),
        name="channel_dft",
    )(x2d, g, modv, modv, cs)


def _seqdft_kernel(cn_ref, sn_ref, a_ref, b_ref, y_ref, acc_ref, *, scale):
    k = pl.program_id(2)

    @pl.when(k == 0)
    def _():
        acc_ref[...] = jnp.zeros_like(acc_ref)

    acc_ref[...] += (jnp.dot(cn_ref[...], a_ref[...], preferred_element_type=F32)
                     + jnp.dot(sn_ref[...], b_ref[...], preferred_element_type=F32))

    @pl.when(k == pl.num_programs(2) - 1)
    def _():
        y_ref[...] = (acc_ref[...] * scale).astype(y_ref.dtype)


def _seq_dft(a, b, cn, sn_neg, scale):
    bsz, n, d = a.shape
    tm = min(1024, n)
    tk = min(512, n)
    mat = pl.BlockSpec((tm, tk), lambda bi, i, k: (i, k))
    rhs = pl.BlockSpec((None, tk, d), lambda bi, i, k: (bi, k, 0))
    return pl.pallas_call(
        functools.partial(_seqdft_kernel, scale=scale),
        grid=(bsz, n // tm, n // tk),
        in_specs=[mat, mat, rhs, rhs],
        out_specs=pl.BlockSpec((None, tm, d), lambda bi, i, k: (bi, i, 0)),
        out_shape=jax.ShapeDtypeStruct((bsz, n, d), BF16),
        scratch_shapes=[pltpu.VMEM((tm, d), F32)],
        compiler_params=_cparams("arbitrary", "arbitrary", "arbitrary"),
        name="sequence_dft",
    )(cn, sn_neg, a, b)


def _mm_res_kernel(y_ref, w_ref, b_ref, r_ref, g_ref, o_ref):
    acc = jnp.dot(y_ref[...], w_ref[...], preferred_element_type=F32)
    o_ref[...] = r_ref[...] + g_ref[...] * (acc + b_ref[...])


def _mm_residual(y2d, w_bf, bias, res2d, modv, layer, piece, row_fn, tm):
    t, k = y2d.shape
    d = w_bf.shape[1]
    tn = min(1024, d)
    nj = d // tn
    return pl.pallas_call(
        _mm_res_kernel,
        grid=(nj, t // tm),
        in_specs=[
            pl.BlockSpec((tm, k), lambda j, i: (i, 0)),
            pl.BlockSpec((k, tn), lambda j, i: (0, j)),
            pl.BlockSpec((1, tn), lambda j, i: (0, j)),
            pl.BlockSpec((tm, tn), lambda j, i: (i, j)),
            pl.BlockSpec((None, None, 1, tn), lambda j, i: (layer, row_fn(i), 0, piece * nj + j)),
        ],
        out_specs=pl.BlockSpec((tm, tn), lambda j, i: (i, j)),
        out_shape=jax.ShapeDtypeStruct((t, d), F32),
        compiler_params=_cparams("arbitrary", "arbitrary"),
        name="mixer_out_residual",
    )(y2d, w_bf, bias, res2d, modv)


def _route_kernel(x_ref, g_ref, sh_ref, sc_ref, wr_ref, br_ref, f_ref, r_ref):
    f = _norm_mod(x_ref[...], g_ref[...], sh_ref[...], sc_ref[...])
    f_ref[...] = f
    logits = jnp.dot(f, wr_ref[...], preferred_element_type=F32, precision=HIGHEST) + br_ref[...]
    tm, ne = logits.shape
    col = lax.broadcasted_iota(jnp.int32, (tm, ne), 1).astype(F32)
    lane = lax.broadcasted_iota(jnp.int32, (tm, LANES), 1)
    out = jnp.zeros((tm, LANES), F32)
    vals = []
    for k in range(TOP_K):
        m = jnp.max(logits, axis=-1, keepdims=True)
        idx = jnp.min(jnp.where(logits == m, col, float(ne)), axis=-1, keepdims=True)
        logits = jnp.where(col == idx, -jnp.inf, logits)
        out = jnp.where(lane == k, idx, out)
        vals.append(m)
    es = [jnp.exp(v - vals[0]) for v in vals]
    den = es[0]
    for e in es[1:]:
        den = den + e
    for k in range(TOP_K):
        out = jnp.where(lane == TOP_K + k, es[k] / den, out)
    r_ref[...] = out


def _ffn_route(x2d, g, modv, layer, row_fn, w_router, b_router, tm):
    t, d = x2d.shape
    ne = w_router.shape[1]
    spec = pl.BlockSpec((tm, d), lambda i: (i, 0))
    return pl.pallas_call(
        _route_kernel,
        grid=(t // tm,),
        in_specs=[
            spec,
            pl.BlockSpec((1, d), lambda i: (0, 0)),
            _mod_spec(d, 3, row_fn, layer),
            _mod_spec(d, 4, row_fn, layer),
            pl.BlockSpec((d, ne), lambda i: (0, 0)),
            pl.BlockSpec((1, ne), lambda i: (0, 0)),
        ],
        out_specs=[spec, pl.BlockSpec((tm, LANES), lambda i: (i, 0))],
        out_shape=[jax.ShapeDtypeStruct((t, d), F32), jax.ShapeDtypeStruct((t, LANES), F32)],
        compiler_params=_cparams("arbitrary"),
        name="ffn_norm_route",
    )(x2d, g, modv, modv, w_router, b_router)


def _gather_rows(tok_ref, f_hbm, xbuf, sem, slot):
    def body(r, _):
        pltpu.make_async_copy(f_hbm.at[pl.ds(tok_ref[0, 0, r], 1)],
                              xbuf.at[slot, pl.ds(r, 1)], sem.at[slot]).start()
        return 0

    lax.fori_loop(0, MOE_ROWS, body, 0, unroll=8)


def _wait_rows(f_hbm, xbuf, sem, slot):
    def body(r, _):
        pltpu.make_async_copy(f_hbm.at[pl.ds(0, 1)], xbuf.at[slot, pl.ds(0, 1)],
                              sem.at[slot]).wait()
        return 0

    lax.fori_loop(0, MOE_ROWS, body, 0)


def _new_expert(be_ref, b):
    return (b == 0) | (be_ref[b] != be_ref[jnp.maximum(b - 1, 0)])


def _moe_up_kernel(be_ref, nu_ref, tok0_ref, tokn_ref, f_hbm, w_ref, bg_ref, bl_ref, perm_ref,
                   o_ref, xbuf, wg_ref, wl_ref, sem):
    j, b = pl.program_id(0), pl.program_id(1)
    nb = pl.num_programs(1)
    step = j * nb + b
    slot = step % 2

    @pl.when(step == 0)
    def _():
        _gather_rows(tok0_ref, f_hbm, xbuf, sem, 0)

    _wait_rows(f_hbm, xbuf, sem, slot)
    _gather_rows(tokn_ref, f_hbm, xbuf, sem, 1 - slot)

    @pl.when(_new_expert(be_ref, b))
    def _():
        for q in range(w_ref.shape[1] // (2 * LANES)):
            wq = w_ref[:, q * 2 * LANES:(q + 1) * 2 * LANES].astype(BF16)
            r = jnp.dot(wq, perm_ref[...], preferred_element_type=F32)
            wg_ref[:, q * LANES:(q + 1) * LANES] = r[:, :LANES].astype(BF16)
            wl_ref[:, q * LANES:(q + 1) * LANES] = r[:, LANES:].astype(BF16)

    @pl.when(b < nu_ref[0])
    def _():
        x = xbuf[slot].astype(BF16)
        glu = jnp.dot(x, wg_ref[...], preferred_element_type=F32) + bg_ref[...]
        lin = jnp.dot(x, wl_ref[...], preferred_element_type=F32) + bl_ref[...]
        glu = jnp.minimum(glu, SWIGLU_LIMIT)
        lin = jnp.clip(lin, -SWIGLU_LIMIT, SWIGLU_LIMIT)
        o_ref[...] = (glu * jax.nn.sigmoid(SWIGLU_ALPHA * glu) * (lin + 1.0)).astype(o_ref.dtype)

    @pl.when(b >= nu_ref[0])
    def _():
        o_ref[...] = jnp.zeros_like(o_ref)

    @pl.when(step == pl.num_programs(0) * nb - 1)
    def _():
        _wait_rows(f_hbm, xbuf, sem, 1 - slot)


def _moe_up(f2d, slot_tok, block_expert, n_used, w_up, b_glu, b_lin, perm):
    d = f2d.shape[1]
    nb = slot_tok.shape[0] // MOE_ROWS
    de = w_up.shape[2] // 2
    tn = min(1024, de)
    tok_spec = lambda fn: pl.BlockSpec((1, 1, MOE_ROWS), fn, memory_space=pltpu.SMEM)
    bspec = pl.BlockSpec((None, 1, tn), lambda j, b, be, nu: (be[b], 0, j))
    return pl.pallas_call(
        _moe_up_kernel,
        grid_spec=pltpu.PrefetchScalarGridSpec(
            num_scalar_prefetch=2,
            grid=(de // tn, nb),
            in_specs=[tok_spec(lambda j, b, be, nu: (0, 0, 0)),
                      tok_spec(lambda j, b, be, nu: ((b + 1) % nb, 0, 0)),
                      pl.BlockSpec(memory_space=pl.ANY),
                      pl.BlockSpec((None, d, 2 * tn), lambda j, b, be, nu: (be[b], 0, j)),
                      bspec, bspec,
                      pl.BlockSpec(perm.shape, lambda j, b, be, nu: (0, 0))],
            out_specs=pl.BlockSpec((MOE_ROWS, tn), lambda j, b, be, nu: (b, j)),
            scratch_shapes=[pltpu.VMEM((2, MOE_ROWS, d), F32), pltpu.VMEM((d, tn), BF16),
                            pltpu.VMEM((d, tn), BF16), pltpu.SemaphoreType.DMA((2,))],
        ),
        out_shape=jax.ShapeDtypeStruct((nb * MOE_ROWS, de), BF16),
        compiler_params=_cparams("arbitrary", "arbitrary"),
        name="moe_up_swiglu",
    )(block_expert, n_used, slot_tok.reshape(nb, 1, MOE_ROWS), slot_tok.reshape(nb, 1, MOE_ROWS),
      f2d, w_up, b_glu, b_lin, perm)


def _moe_down_kernel(be_ref, nu_ref, a_ref, w_ref, b_ref, o_ref, wbf_ref):
    b = pl.program_id(1)

    @pl.when(_new_expert(be_ref, b))
    def _():
        wbf_ref[...] = w_ref[...].astype(BF16)

    @pl.when(b < nu_ref[0])
    def _():
        o_ref[...] = jnp.dot(a_ref[...], wbf_ref[...], preferred_element_type=F32) + b_ref[...]

    @pl.when(b >= nu_ref[0])
    def _():
        o_ref[...] = jnp.zeros_like(o_ref)


def _moe_down(act, block_expert, n_used, w_down, b_down):
    s, de = act.shape
    ne, _, d = w_down.shape
    nb = s // MOE_ROWS
    tn = min(1024, d)
    return pl.pallas_call(
        _moe_down_kernel,
        grid_spec=pltpu.PrefetchScalarGridSpec(
            num_scalar_prefetch=2,
            grid=(d // tn, nb),
            in_specs=[pl.BlockSpec((MOE_ROWS, de), lambda j, b, be, nu: (b, 0)),
                      pl.BlockSpec((None, de, tn), lambda j, b, be, nu: (be[b], 0, j)),
                      pl.BlockSpec((None, 1, tn), lambda j, b, be, nu: (be[b], 0, j))],
            out_specs=pl.BlockSpec((MOE_ROWS, tn), lambda j, b, be, nu: (b, j)),
            scratch_shapes=[pltpu.VMEM((de, tn), BF16)],
        ),
        out_shape=jax.ShapeDtypeStruct((s, d), F32),
        compiler_params=_cparams("arbitrary", "arbitrary"),
        name="moe_down",
    )(block_expert, n_used, act, w_down, b_down)


def _combine_kernel(dest0_ref, destn_ref, ys_hbm, r_ref, x_ref, g_ref, o_ref, buf, sem, *, tm):
    i = pl.program_id(0)
    slot = i % 2

    def fetch(dest_ref, slot):
        def body(r, _):
            for k in range(TOP_K):
                pltpu.make_async_copy(ys_hbm.at[pl.ds(dest_ref[0, 0, r * TOP_K + k], 1)],
                                      buf.at[slot, k, pl.ds(r, 1)], sem.at[slot]).start()
            return 0

        lax.fori_loop(0, tm, body, 0, unroll=4)

    def drain(slot):
        def body(r, _):
            pltpu.make_async_copy(ys_hbm.at[pl.ds(0, 1)], buf.at[slot, 0, pl.ds(0, 1)],
                                  sem.at[slot]).wait()
            return 0

        lax.fori_loop(0, tm * TOP_K, body, 0)

    @pl.when(i == 0)
    def _():
        fetch(dest0_ref, 0)

    drain(slot)
    fetch(destn_ref, 1 - slot)
    route = r_ref[...]
    acc = route[:, TOP_K:TOP_K + 1] * buf[slot, 0]
    for k in range(1, TOP_K):
        acc = acc + route[:, TOP_K + k:TOP_K + k + 1] * buf[slot, k]
    o_ref[...] = x_ref[...] + g_ref[...] * acc

    @pl.when(i == pl.num_programs(0) - 1)
    def _():
        drain(1 - slot)


def _moe_combine(ys, dest, route, x2d, modv, layer, row_fn, tm):
    t, d = x2d.shape
    nt = t // tm
    dest3 = dest.reshape(nt, 1, tm * TOP_K)
    dspec = lambda fn: pl.BlockSpec((1, 1, tm * TOP_K), fn, memory_space=pltpu.SMEM)
    return pl.pallas_call(
        functools.partial(_combine_kernel, tm=tm),
        grid=(nt,),
        in_specs=[
            dspec(lambda i: (0, 0, 0)),
            dspec(lambda i: ((i + 1) % nt, 0, 0)),
            pl.BlockSpec(memory_space=pl.ANY),
            pl.BlockSpec((tm, LANES), lambda i: (i, 0)),
            pl.BlockSpec((tm, d), lambda i: (i, 0)),
            _mod_spec(d, 5, row_fn, layer),
        ],
        out_specs=pl.BlockSpec((tm, d), lambda i: (i, 0)),
        out_shape=jax.ShapeDtypeStruct((t, d), F32),
        scratch_shapes=[pltpu.VMEM((2, TOP_K, tm, d), F32), pltpu.SemaphoreType.DMA((2,))],
        compiler_params=_cparams("arbitrary"),
        name="moe_combine",
    )(dest3, dest3, ys, route, x2d, modv)


def _routing_tables(route, ne):
    t = route.shape[0]
    flat_e = route[:, :TOP_K].astype(jnp.int32).reshape(-1)
    n_rows = t * TOP_K
    onehot = (flat_e[:, None] == jnp.arange(ne, dtype=jnp.int32)[None, :]).astype(jnp.int32)
    csum = jnp.cumsum(onehot, axis=0)
    rank = jnp.take_along_axis(csum, flat_e[:, None], axis=1)[:, 0] - 1
    counts = csum[-1]
    padded = (counts + MOE_ROWS - 1) // MOE_ROWS * MOE_ROWS
    pad_end = jnp.cumsum(padded)
    pad_start = pad_end - padded
    dest = pad_start[flat_e] + rank
    n_blocks = -(-n_rows // MOE_ROWS) + ne
    slot_tok = jnp.zeros((n_blocks * MOE_ROWS,), jnp.int32).at[dest].set(
        jnp.arange(n_rows, dtype=jnp.int32) // TOP_K)
    block_expert = jnp.minimum(
        jnp.searchsorted(pad_end, jnp.arange(n_blocks, dtype=jnp.int32) * MOE_ROWS, side="right"),
        ne - 1).astype(jnp.int32)
    n_used = (pad_end[-1:] // MOE_ROWS).astype(jnp.int32)
    return dest.astype(jnp.int32), slot_tok, block_expert, n_used


def _moe_experts(f2d, route, w_up, b_up, w_down, b_down):
    ne = w_up.shape[0]
    dest, slot_tok, block_expert, n_used = _routing_tables(route, ne)
    b_pairs = b_up.reshape(ne, 1, -1, 2)
    col = jnp.arange(2 * LANES)
    src = jnp.where(col < LANES, 2 * col, 2 * (col - LANES) + 1)
    perm = (jnp.arange(2 * LANES)[:, None] == src[None, :]).astype(BF16)
    act = _moe_up(f2d, slot_tok, block_expert, n_used, w_up, b_pairs[..., 0], b_pairs[..., 1],
                  perm)
    ys = _moe_down(act, block_expert, n_used, w_down, b_down.reshape(ne, 1, -1))
    return ys, dest


def _mla_proj_kernel(x_ref, g_ref, sh_ref, sc_ref, win_ref, gq_ref, gkv_ref, wq_ref, wkv_ref,
                     q_ref, kv_ref, kpe_ref, *, qp, kvp):
    h = _norm_mod(x_ref[...], g_ref[...], sh_ref[...], sc_ref[...]).astype(BF16)
    a = jnp.dot(h, win_ref[...], preferred_element_type=F32)
    cq = a[:, :qp]
    ckv = a[:, qp:qp + kvp]
    kpe_ref[...] = a[:, qp + kvp:]
    cqn = cq * lax.rsqrt(jnp.sum(cq * cq, axis=-1, keepdims=True) * (1.0 / Q_LORA_RANK) + NORM_EPS)
    ckvn = ckv * lax.rsqrt(jnp.sum(ckv * ckv, axis=-1, keepdims=True) * (1.0 / KV_LORA_RANK)
                           + NORM_EPS)
    q_ref[...] = jnp.dot((cqn * gq_ref[...]).astype(BF16), wq_ref[...], preferred_element_type=F32)
    kv_ref[...] = jnp.dot((ckvn * gkv_ref[...]).astype(BF16), wkv_ref[...],
                          preferred_element_type=F32)


def _mla_proj(x2d, g, modv, layer, row_fn, win_p, gq_p, gkv_p, wq_p, wkv_p, tm):
    t, d = x2d.shape
    qp, kvp = gq_p.shape[1], gkv_p.shape[1]
    na = win_p.shape[1]
    nq, nkv = wq_p.shape[1], wkv_p.shape[1]
    full = lambda arr: pl.BlockSpec(arr.shape, lambda i: (0, 0))
    return pl.pallas_call(
        functools.partial(_mla_proj_kernel, qp=qp, kvp=kvp),
        grid=(t // tm,),
        in_specs=[
            pl.BlockSpec((tm, d), lambda i: (i, 0)),
            pl.BlockSpec((1, d), lambda i: (0, 0)),
            _mod_spec(d, 0, row_fn, layer),
            _mod_spec(d, 1, row_fn, layer),
            full(win_p), full(gq_p), full(gkv_p), full(wq_p), full(wkv_p),
        ],
        out_specs=[pl.BlockSpec((tm, nq), lambda i: (i, 0)),
                   pl.BlockSpec((tm, nkv), lambda i: (i, 0)),
                   pl.BlockSpec((tm, na - qp - kvp), lambda i: (i, 0))],
        out_shape=[jax.ShapeDtypeStruct((t, nq), F32), jax.ShapeDtypeStruct((t, nkv), F32),
                   jax.ShapeDtypeStruct((t, na - qp - kvp), F32)],
        compiler_params=_cparams("arbitrary"),
        name="mla_projections",
    )(x2d, g, modv, modv, win_p, gq_p, gkv_p, wq_p, wkv_p)


def _rope(x, cos, sin_lo, sin_hi):
    quarter = QK_ROPE_DIM // 4
    return (x * cos + pltpu.roll(x, LANES - quarter, 1) * sin_lo
            + pltpu.roll(x, quarter, 1) * sin_hi)


def _headnorm_kernel(q_ref, kv_ref, kpe_ref, gq_ref, gk_ref, cos_ref, slo_ref, shi_ref,
                     qo_ref, ko_ref, vo_ref, *, heads, q_scale):
    inv = 1.0 / (QK_NOPE_DIM + QK_ROPE_DIM)
    cos, slo, shi = cos_ref[...], slo_ref[...], shi_ref[...]
    kpe = kpe_ref[...]
    kpe_ss = jnp.sum(kpe * kpe, axis=-1, keepdims=True)
    gq = gq_ref[...]
    gk = gk_ref[...]
    for h in range(heads):
        qh = q_ref[:, h * HEAD_SLOT:(h + 1) * HEAD_SLOT]
        rs = lax.rsqrt(jnp.sum(qh * qh, axis=-1, keepdims=True) * inv + NORM_EPS) * q_scale
        qn = qh * rs * gq
        qo_ref[h, :, :LANES] = qn[:, :LANES].astype(BF16)
        qo_ref[h, :, LANES:] = _rope(qn[:, LANES:], cos, slo, shi).astype(BF16)
        kn = kv_ref[:, h * HEAD_SLOT:h * HEAD_SLOT + QK_NOPE_DIM]
        rk = lax.rsqrt((jnp.sum(kn * kn, axis=-1, keepdims=True) + kpe_ss) * inv + NORM_EPS)
        ko_ref[h, :, :LANES] = (kn * rk * gk[:, :LANES]).astype(BF16)
        ko_ref[h, :, LANES:] = _rope(kpe * rk * gk[:, LANES:], cos, slo, shi).astype(BF16)
        vo_ref[h] = kv_ref[:, h * HEAD_SLOT + QK_NOPE_DIM:(h + 1) * HEAD_SLOT].astype(BF16)


def _head_norm_rope(q_raw, kv_raw, kpe, gq_slot, gk_slot, cos, slo, shi, bsz, n_ctx, n_lat, tm):
    heads = MLA_HEADS
    n_keys = n_ctx + n_lat
    ctx_tiles = bsz * n_ctx // tm
    per_ctx = n_ctx // tm
    per_lat = n_lat // tm

    def batch_of(i):
        return jnp.where(i < ctx_tiles, i // per_ctx, (i - ctx_tiles) // per_lat)

    def key_blk(i):
        return jnp.where(i < ctx_tiles, i % per_ctx, per_ctx + (i - ctx_tiles) % per_lat)

    tab = pl.BlockSpec((tm, LANES), lambda i: (key_blk(i), 0))
    t = q_raw.shape[0]
    out_map = lambda i: (batch_of(i), 0, key_blk(i), 0)

    def q_map(i):
        blk = jnp.where(i < ctx_tiles, per_lat + i % per_ctx, (i - ctx_tiles) % per_lat)
        return (batch_of(i), 0, blk, 0)

    return pl.pallas_call(
        functools.partial(_headnorm_kernel, heads=heads,
                          q_scale=float(QK_NOPE_DIM + QK_ROPE_DIM) ** -0.5),
        grid=(t // tm,),
        in_specs=[
            pl.BlockSpec((tm, heads * HEAD_SLOT), lambda i: (i, 0)),
            pl.BlockSpec((tm, heads * HEAD_SLOT), lambda i: (i, 0)),
            pl.BlockSpec((tm, LANES), lambda i: (i, 0)),
            pl.BlockSpec((1, HEAD_SLOT), lambda i: (0, 0)),
            pl.BlockSpec((1, HEAD_SLOT), lambda i: (0, 0)),
            tab, tab, tab,
        ],
        out_specs=[pl.BlockSpec((None, heads, tm, HEAD_SLOT), q_map),
                   pl.BlockSpec((None, heads, tm, HEAD_SLOT), out_map),
                   pl.BlockSpec((None, heads, tm, V_HEAD_DIM), out_map)],
        out_shape=[jax.ShapeDtypeStruct((bsz, heads, n_keys, HEAD_SLOT), BF16),
                   jax.ShapeDtypeStruct((bsz, heads, n_keys, HEAD_SLOT), BF16),
                   jax.ShapeDtypeStruct((bsz, heads, n_keys, V_HEAD_DIM), BF16)],
        compiler_params=_cparams("arbitrary"),
        name="mla_headnorm_rope",
    )(q_raw, kv_raw, kpe, gq_slot, gk_slot, cos, slo, shi)


def _attn_kernel(q_ref, k_ref, v_ref, o_ref, *, sub):
    for r in range(q_ref.shape[0] // sub):
        rows = pl.ds(r * sub, sub)
        s = lax.dot_general(q_ref[rows, :], k_ref[...], (((1,), (1,)), ((), ())),
                            preferred_element_type=F32)
        m = jnp.max(s, axis=-1, keepdims=True)
        p = jnp.exp(s - m)
        l = jnp.sum(p, axis=-1, keepdims=True)
        o = jnp.dot(p.astype(BF16), v_ref[...], preferred_element_type=F32)
        o_ref[rows, :] = (o / l).astype(o_ref.dtype)


def _attention(q, k, v, n_ctx, tq):
    bsz, heads, n_keys, _ = k.shape
    n_lat = n_keys - n_ctx
    return pl.pallas_call(
        functools.partial(_attn_kernel, sub=min(256, tq)),
        grid=(bsz, heads, n_lat // tq),
        in_specs=[
            pl.BlockSpec((None, None, tq, HEAD_SLOT), lambda b, h, i: (b, h, i, 0)),
            pl.BlockSpec((None, None, n_keys, HEAD_SLOT), lambda b, h, i: (b, h, 0, 0)),
            pl.BlockSpec((None, None, n_keys, V_HEAD_DIM), lambda b, h, i: (b, h, 0, 0)),
        ],
        out_specs=pl.BlockSpec((None, tq, V_HEAD_DIM), lambda b, h, i: (b, i, h)),
        out_shape=jax.ShapeDtypeStruct((bsz, n_lat, heads * V_HEAD_DIM), BF16),
        compiler_params=_cparams("arbitrary", "arbitrary", "arbitrary"),
        name="mla_attention",
    )(q, k, v)


def _dft_tables(n):
    k = jnp.arange(n, dtype=jnp.int32)
    ang = ((k[:, None] * k[None, :]) % n).astype(F32) * (2.0 * math.pi / n)
    return jnp.cos(ang), jnp.sin(ang)


def _rope_tables(n_ctx, n_lat):
    rows = n_lat // GRID_W
    pairs = QK_ROPE_DIM // 4
    row = jnp.repeat(jnp.arange(rows, dtype=F32), GRID_W)
    col = jnp.tile(jnp.arange(GRID_W, dtype=F32), rows)
    inv = ROPE_THETA ** (-jnp.arange(pairs, dtype=F32) / pairs)
    ang = jnp.stack([row[:, None] * inv, col[:, None] * inv], axis=1)
    cos, sin = jnp.cos(ang), jnp.sin(ang)
    zero = jnp.zeros_like(sin)
    cos_t = jnp.stack([cos, cos], axis=2).reshape(n_lat, QK_ROPE_DIM)
    slo_t = jnp.stack([-sin, zero], axis=2).reshape(n_lat, QK_ROPE_DIM)
    shi_t = jnp.stack([zero, sin], axis=2).reshape(n_lat, QK_ROPE_DIM)

    def full(tab, ctx_val):
        tab = jnp.pad(tab, ((0, 0), (0, LANES - QK_ROPE_DIM)))
        return jnp.concatenate([jnp.full((n_ctx, LANES), ctx_val, F32), tab], axis=0)

    return full(cos_t, 1.0), full(slo_t, 0.0), full(shi_t, 0.0)


def _pad_cols(w, n):
    return jnp.pad(w, ((0, 0), (0, n - w.shape[1])))


def _mla_weights(w_in, g_q_lora, w_q_up, g_kv_lora, w_kv_up, g_q_head, g_k_head):
    qp = -(-Q_LORA_RANK // LANES) * LANES
    kvp = -(-KV_LORA_RANK // LANES) * LANES
    qr, kvr = Q_LORA_RANK, KV_LORA_RANK
    win_p = jnp.concatenate([_pad_cols(w_in[:, :qr], qp), _pad_cols(w_in[:, qr:qr + kvr], kvp),
                             _pad_cols(w_in[:, qr + kvr:], LANES)], axis=1).astype(BF16)
    gq_p = _pad_cols(g_q_lora[None, :], qp)
    gkv_p = _pad_cols(g_kv_lora[None, :], kvp)
    hd = QK_NOPE_DIM + QK_ROPE_DIM
    wq = w_q_up.reshape(qr, MLA_HEADS, hd)
    wq = jnp.pad(wq, ((0, qp - qr), (0, 0), (0, HEAD_SLOT - hd)))
    wq_p = wq.reshape(qp, MLA_HEADS * HEAD_SLOT).astype(BF16)
    wkv_p = jnp.pad(w_kv_up, ((0, kvp - kvr), (0, 0))).astype(BF16)
    gq_slot = _pad_cols(g_q_head[None, :], HEAD_SLOT)
    gk_slot = _pad_cols(g_k_head[None, :], HEAD_SLOT)
    return win_p, gq_p, gkv_p, wq_p, wkv_p, gq_slot, gk_slot


def kernel(x, c, ctx, c_ctx, w_mod, b_mod, g_mix, g_ffn, fourier_w_out, fourier_b_out, mla_w_in, mla_g_q_lora, mla_w_q_up, mla_g_kv_lora, mla_w_kv_up, mla_g_q_head, mla_g_k_head, mla_w_out, router_w, router_b, expert_w_up, expert_b_up, expert_w_down, expert_b_down):
    bsz, n, d = x.shape
    l = ctx.shape[1]
    assert bsz < 8 and n % GRID_W == 0
    tl = min(256, l)
    assert l % tl == 0 and n % tl == 0
    ctx_row = bsz

    c8 = jnp.zeros((8, d), F32).at[:bsz].set(c).at[ctx_row].set(c_ctx)
    modv = _mod_vectors(c8, w_mod, b_mod).reshape(w_mod.shape[0], 8, 1, 6 * d)

    def lat_row(tm):
        return lambda i: i // (n // tm)

    def all_row(tm):
        nct = bsz * l // tm
        return lambda i: jnp.where(i < nct, ctx_row, (i - nct) // (n // tm))

    def ctx_only_row(i):
        return ctx_row

    dg = d // FOURIER_GROUPS
    cc, sc_ = _dft_tables(dg)
    cs = jnp.concatenate([cc, sc_], axis=1).astype(BF16)
    wf = fourier_w_out[0].astype(BF16)
    bf = fourier_b_out[0][None, :]
    g0 = g_mix[0][None, :]

    def fourier(x3, row_fn_of, tm):
        bb, nn, _ = x3.shape
        x2 = x3.reshape(bb * nn, d)
        a, b = _channel_dft(x2, g0, modv, 0, row_fn_of, cs, tm)
        cn, sn = _dft_tables(nn)
        y = _seq_dft(a.reshape(bb, nn, d), b.reshape(bb, nn, d), cn.astype(BF16),
                     (-sn).astype(BF16), 1.0 / math.sqrt(nn * dg))
        return _mm_residual(y.reshape(bb * nn, d), wf, bf, x2, modv, 0, 2, row_fn_of, tm)

    tm_lat = min(512, n)
    x_lat1 = fourier(x, lat_row(tm_lat), tm_lat)
    x_ctx1 = fourier(ctx, ctx_only_row, tl)
    x_all = jnp.concatenate([x_ctx1, x_lat1], axis=0)
    n_ctx_tok = bsz * l

    f0, route0 = _ffn_route(x_all, g_ffn[0][None, :], modv, 0, all_row(tl), router_w[0],
                            router_b[0][None, :], tl)
    ys0, dest0 = _moe_experts(f0, route0, expert_w_up[0], expert_b_up[0], expert_w_down[0],
                              expert_b_down[0])
    tc = min(128, tl)
    x_all2 = _moe_combine(ys0, dest0, route0, x_all, modv, 0, all_row(tc), tc)

    win_p, gq_p, gkv_p, wq_p, wkv_p, gq_slot, gk_slot = _mla_weights(
        mla_w_in[0], mla_g_q_lora[0], mla_w_q_up[0], mla_g_kv_lora[0], mla_w_kv_up[0],
        mla_g_q_head[0], mla_g_k_head[0])
    q_raw, kv_raw, kpe = _mla_proj(x_all2, g_mix[1][None, :], modv, 1, all_row(tl),
                                   win_p, gq_p, gkv_p, wq_p, wkv_p, tl)
    cos, slo, shi = _rope_tables(l, n)
    qh, kh, vh = _head_norm_rope(q_raw, kv_raw, kpe, gq_slot, gk_slot, cos, slo, shi,
                                 bsz, l, n, tl)
    attn = _attention(qh, kh, vh, l, min(512, n))
    x_lat2 = x_all2[n_ctx_tok:]
    zero_bias = jnp.zeros((1, d), F32)
    x_lat3 = _mm_residual(attn.reshape(bsz * n, -1), mla_w_out[0].astype(BF16), zero_bias,
                          x_lat2, modv, 1, 2, lat_row(tm_lat), tm_lat)

    f1, route1 = _ffn_route(x_lat3, g_ffn[1][None, :], modv, 1, lat_row(tl), router_w[1],
                            router_b[1][None, :], tl)
    ys1, dest1 = _moe_experts(f1, route1, expert_w_up[1], expert_b_up[1], expert_w_down[1],
                              expert_b_down[1])
    out = _moe_combine(ys1, dest1, route1, x_lat3, modv, 1, lat_row(tc), tc)
    return out.reshape(bsz, n, d)
```

```python
import functools
import math

import jax
import jax.numpy as jnp
from jax import lax
from jax.experimental import pallas as pl
from jax.experimental.pallas import tpu as pltpu

GRID_W = 64
FOURIER_GROUPS = 4
MLA_HEADS = 16
Q_LORA_RANK = 448
KV_LORA_RANK = 512
QK_NOPE_DIM = 128
QK_ROPE_DIM = 64
V_HEAD_DIM = 128
ROPE_THETA = 10000.0
TOP_K = 4
SWIGLU_ALPHA = 1.702
SWIGLU_LIMIT = 7.0
NORM_EPS = 1e-6

LANES = 128
MOE_ROWS = 256
HEAD_SLOT = 256
VMEM_LIMIT = 56 << 20

F32 = jnp.float32
BF16 = jnp.bfloat16
HIGHEST = lax.Precision.HIGHEST


def _cparams(*sem, vmem=VMEM_LIMIT):
    return pltpu.CompilerParams(dimension_semantics=sem, vmem_limit_bytes=vmem)


def _norm_mod(x, g, sh, sc):
    ms = jnp.mean(x * x, axis=-1, keepdims=True)
    return (x * lax.rsqrt(ms + NORM_EPS)) * g * (1.0 + sc) + sh


def _mod_spec(d, piece, row_fn, layer):
    return pl.BlockSpec((None, None, 1, d), lambda *ids: (layer, row_fn(*ids), 0, piece))


def _mod_kernel(c_ref, w_ref, b_ref, o_ref):
    c = c_ref[...]
    a = c * jax.nn.sigmoid(c)
    o_ref[...] = jnp.dot(a, w_ref[...], preferred_element_type=F32, precision=HIGHEST) + b_ref[...]


def _mod_vectors(c8, w_mod, b_mod):
    depth, d, n6 = w_mod.shape
    tn = next(t for t in (1024, 512, 256, 128) if n6 % t == 0)
    return pl.pallas_call(
        _mod_kernel,
        grid=(depth, n6 // tn),
        in_specs=[
            pl.BlockSpec((8, d), lambda l, j: (0, 0)),
            pl.BlockSpec((None, d, tn), lambda l, j: (l, 0, j)),
            pl.BlockSpec((None, 1, tn), lambda l, j: (l, 0, j)),
        ],
        out_specs=pl.BlockSpec((None, 8, tn), lambda l, j: (l, 0, j)),
        out_shape=jax.ShapeDtypeStruct((depth, 8, n6), F32),
        compiler_params=_cparams("arbitrary", "arbitrary"),
        name="mod_vectors",
    )(c8, w_mod, b_mod.reshape(depth, 1, n6))


def _chdft_kernel(x_ref, g_ref, sh_ref, sc_ref, cs_ref, a_ref, b_ref, *, groups):
    h = _norm_mod(x_ref[...], g_ref[...], sh_ref[...], sc_ref[...]).astype(BF16)
    dg = h.shape[-1] // groups
    for gi in range(groups):
        r = jnp.dot(h[:, gi * dg:(gi + 1) * dg], cs_ref[...], preferred_element_type=F32)
        a_ref[:, gi * dg:(gi + 1) * dg] = r[:, :dg].astype(BF16)
        b_ref[:, gi * dg:(gi + 1) * dg] = r[:, dg:].astype(BF16)


def _channel_dft(x2d, g, modv, layer, row_fn, cs, tm):
    t, d = x2d.shape
    spec = pl.BlockSpec((tm, d), lambda i: (i, 0))
    return pl.pallas_call(
        functools.partial(_chdft_kernel, groups=FOURIER_GROUPS),
        grid=(t // tm,),
        in_specs=[
            spec,
            pl.BlockSpec((1, d), lambda i: (0, 0)),
            _mod_spec(d, 0, row_fn, layer),
            _mod_spec(d, 1, row_fn, layer),
            pl.BlockSpec(cs.shape, lambda i: (0, 0)),
        ],
        out_specs=[spec, spec],
        out_shape=[jax.ShapeDtypeStruct((t, d), BF16)] * 2,
        compiler_params=_cparams("arbitrary"),
        name="channel_dft",
    )(x2d, g, modv, modv, cs)


def _seqdft_kernel(cn_ref, sn_ref, a_ref, b_ref, y_ref, acc_ref, *, scale):
    k = pl.program_id(2)

    @pl.when(k == 0)
    def _():
        acc_ref[...] = jnp.zeros_like(acc_ref)

    acc_ref[...] += (jnp.dot(cn_ref[...], a_ref[...], preferred_element_type=F32)
                     + jnp.dot(sn_ref[...], b_ref[...], preferred_element_type=F32))

    @pl.when(k == pl.num_programs(2) - 1)
    def _():
        y_ref[...] = (acc_ref[...] * scale).astype(y_ref.dtype)


def _seq_dft(a, b, cn, sn_neg, scale):
    bsz, n, d = a.shape
    tm = min(1024, n)
    tk = min(512, n)
    mat = pl.BlockSpec((tm, tk), lambda bi, i, k: (i, k))
    rhs = pl.BlockSpec((None, tk, d), lambda bi, i, k: (bi, k, 0))
    return pl.pallas_call(
        functools.partial(_seqdft_kernel, scale=scale),
        grid=(bsz, n // tm, n // tk),
        in_specs=[mat, mat, rhs, rhs],
        out_specs=pl.BlockSpec((None, tm, d), lambda bi, i, k: (bi, i, 0)),
        out_shape=jax.ShapeDtypeStruct((bsz, n, d), BF16),
        scratch_shapes=[pltpu.VMEM((tm, d), F32)],
        compiler_params=_cparams("arbitrary", "arbitrary", "arbitrary"),
        name="sequence_dft",
    )(cn, sn_neg, a, b)


def _mm_res_kernel(y_ref, w_ref, b_ref, r_ref, g_ref, o_ref):
    acc = jnp.dot(y_ref[...], w_ref[...], preferred_element_type=F32)
    o_ref[...] = r_ref[...] + g_ref[...] * (acc + b_ref[...])


def _mm_residual(y2d, w_bf, bias, res2d, modv, layer, piece, row_fn, tm):
    t, k = y2d.shape
    d = w_bf.shape[1]
    tn = min(1024, d)
    nj = d // tn
    return pl.pallas_call(
        _mm_res_kernel,
        grid=(nj, t // tm),
        in_specs=[
            pl.BlockSpec((tm, k), lambda j, i: (i, 0)),
            pl.BlockSpec((k, tn), lambda j, i: (0, j)),
            pl.BlockSpec((1, tn), lambda j, i: (0, j)),
            pl.BlockSpec((tm, tn), lambda j, i: (i, j)),
            pl.BlockSpec((None, None, 1, tn), lambda j, i: (layer, row_fn(i), 0, piece * nj + j)),
        ],
        out_specs=pl.BlockSpec((tm, tn), lambda j, i: (i, j)),
        out_shape=jax.ShapeDtypeStruct((t, d), F32),
        compiler_params=_cparams("arbitrary", "arbitrary"),
        name="mixer_out_residual",
    )(y2d, w_bf, bias, res2d, modv)


def _route_kernel(x_ref, g_ref, sh_ref, sc_ref, wr_ref, br_ref, f_ref, r_ref):
    f = _norm_mod(x_ref[...], g_ref[...], sh_ref[...], sc_ref[...])
    f_ref[...] = f
    logits = jnp.dot(f, wr_ref[...], preferred_element_type=F32, precision=HIGHEST) + br_ref[...]
    tm, ne = logits.shape
    col = lax.broadcasted_iota(jnp.int32, (tm, ne), 1).astype(F32)
    lane = lax.broadcasted_iota(jnp.int32, (tm, LANES), 1)
    out = jnp.zeros((tm, LANES), F32)
    vals = []
    for k in range(TOP_K):
        m = jnp.max(logits, axis=-1, keepdims=True)
        idx = jnp.min(jnp.where(logits == m, col, float(ne)), axis=-1, keepdims=True)
        logits = jnp.where(col == idx, -jnp.inf, logits)
        out = jnp.where(lane == k, idx, out)
        vals.append(m)
    es = [jnp.exp(v - vals[0]) for v in vals]
    den = es[0]
    for e in es[1:]:
        den = den + e
    for k in range(TOP_K):
        out = jnp.where(lane == TOP_K + k, es[k] / den, out)
    r_ref[...] = out


def _ffn_route(x2d, g, modv, layer, row_fn, w_router, b_router, tm):
    t, d = x2d.shape
    ne = w_router.shape[1]
    spec = pl.BlockSpec((tm, d), lambda i: (i, 0))
    return pl.pallas_call(
        _route_kernel,
        grid=(t // tm,),
        in_specs=[
            spec,
            pl.BlockSpec((1, d), lambda i: (0, 0)),
            _mod_spec(d, 3, row_fn, layer),
            _mod_spec(d, 4, row_fn, layer),
            pl.BlockSpec((d, ne), lambda i: (0, 0)),
            pl.BlockSpec((1, ne), lambda i: (0, 0)),
        ],
        out_specs=[spec, pl.BlockSpec((tm, LANES), lambda i: (i, 0))],
        out_shape=[jax.ShapeDtypeStruct((t, d), F32), jax.ShapeDtypeStruct((t, LANES), F32)],
        compiler_params=_cparams("arbitrary"),
        name="ffn_norm_route",
    )(x2d, g, modv, modv, w_router, b_router)


def _gather_rows(tok_ref, f_hbm, xbuf, sem, slot):
    for r in range(MOE_ROWS):
        pltpu.make_async_copy(f_hbm.at[pl.ds(tok_ref[0, 0, r], 1)],
                              xbuf.at[slot, pl.ds(r, 1)], sem.at[slot]).start()


def _wait_rows(xbuf, sem, slot):
    pltpu.make_async_copy(xbuf.at[slot], xbuf.at[slot], sem.at[slot]).wait()


def _new_expert(be_ref, b):
    return (b == 0) | (be_ref[b] != be_ref[jnp.maximum(b - 1, 0)])


def _moe_up_kernel(be_ref, nu_ref, *refs, gather):
    if gather:
        (tok0_ref, tokn_ref, f_hbm, w_ref, bg_ref, bl_ref, perm_ref,
         o_ref, xs_ref, xbuf, wg_ref, wl_ref, sem) = refs
    else:
        x_ref, w_ref, bg_ref, bl_ref, perm_ref, o_ref, wg_ref, wl_ref = refs
    b = pl.program_id(1)
    slot = b % 2

    if gather:
        @pl.when(b == 0)
        def _():
            _gather_rows(tok0_ref, f_hbm, xbuf, sem, 0)

        _wait_rows(xbuf, sem, slot)
        _gather_rows(tokn_ref, f_hbm, xbuf, sem, 1 - slot)
        xs_ref[...] = xbuf[slot].astype(BF16)
        x_ref = xs_ref

    @pl.when(_new_expert(be_ref, b))
    def _():
        for q in range(w_ref.shape[1] // (2 * LANES)):
            wq = w_ref[:, q * 2 * LANES:(q + 1) * 2 * LANES].astype(BF16)
            r = jnp.dot(wq, perm_ref[...], preferred_element_type=F32)
            wg_ref[:, q * LANES:(q + 1) * LANES] = r[:, :LANES].astype(BF16)
            wl_ref[:, q * LANES:(q + 1) * LANES] = r[:, LANES:].astype(BF16)

    @pl.when(b < nu_ref[0])
    def _():
        x = x_ref[...]
        glu = jnp.dot(x, wg_ref[...], preferred_element_type=F32) + bg_ref[...]
        lin = jnp.dot(x, wl_ref[...], preferred_element_type=F32) + bl_ref[...]
        glu = jnp.minimum(glu, SWIGLU_LIMIT)
        lin = jnp.clip(lin, -SWIGLU_LIMIT, SWIGLU_LIMIT)
        o_ref[...] = (glu * jax.nn.sigmoid(SWIGLU_ALPHA * glu) * (lin + 1.0)).astype(o_ref.dtype)

    @pl.when(b >= nu_ref[0])
    def _():
        o_ref[...] = jnp.zeros_like(o_ref)

    if gather:
        @pl.when(b == pl.num_programs(1) - 1)
        def _():
            _wait_rows(xbuf, sem, 1 - slot)


def _moe_up(f2d, slot_tok, block_expert, n_used, w_up, b_glu, b_lin, perm, layer):
    d = f2d.shape[1]
    nb = slot_tok.shape[0] // MOE_ROWS
    s = nb * MOE_ROWS
    de = w_up.shape[3] // 2
    tn = min(1024, de)
    nj = de // tn
    tok3 = slot_tok.reshape(nb, 1, MOE_ROWS)
    tok_spec = lambda fn: pl.BlockSpec((1, 1, MOE_ROWS), fn, memory_space=pltpu.SMEM)

    def common_specs(j0):
        bspec = pl.BlockSpec((None, None, 1, tn), lambda j, b, be, nu: (layer, be[b], 0, j + j0))
        return [pl.BlockSpec((None, None, d, 2 * tn),
                             lambda j, b, be, nu: (layer, be[b], 0, j + j0)),
                bspec, bspec, pl.BlockSpec(perm.shape, lambda j, b, be, nu: (0, 0))]

    row_blk = lambda j, b, be, nu: (b, 0)
    w_scratch = [pltpu.VMEM((d, tn), BF16), pltpu.VMEM((d, tn), BF16)]
    act0, xs = pl.pallas_call(
        functools.partial(_moe_up_kernel, gather=True),
        grid_spec=pltpu.PrefetchScalarGridSpec(
            num_scalar_prefetch=2,
            grid=(1, nb),
            in_specs=[tok_spec(lambda j, b, be, nu: (0, 0, 0)),
                      tok_spec(lambda j, b, be, nu: ((b + 1) % nb, 0, 0)),
                      pl.BlockSpec(memory_space=pl.ANY)] + common_specs(0),
            out_specs=[pl.BlockSpec((MOE_ROWS, tn), row_blk), pl.BlockSpec((MOE_ROWS, d), row_blk)],
            scratch_shapes=[pltpu.VMEM((2, MOE_ROWS, d), F32)] + w_scratch
            + [pltpu.SemaphoreType.DMA((2,))],
        ),
        out_shape=[jax.ShapeDtypeStruct((s, tn), BF16), jax.ShapeDtypeStruct((s, d), BF16)],
        compiler_params=_cparams("arbitrary", "arbitrary"),
        name="moe_up_gather",
    )(block_expert, n_used, tok3, tok3, f2d, w_up, b_glu, b_lin, perm)
    if nj == 1:
        return [act0]
    rest = pl.pallas_call(
        functools.partial(_moe_up_kernel, gather=False),
        grid_spec=pltpu.PrefetchScalarGridSpec(
            num_scalar_prefetch=2,
            grid=(nj - 1, nb),
            in_specs=[pl.BlockSpec((MOE_ROWS, d), row_blk)] + common_specs(1),
            out_specs=pl.BlockSpec((MOE_ROWS, tn), lambda j, b, be, nu: (b, j)),
            scratch_shapes=w_scratch,
        ),
        out_shape=jax.ShapeDtypeStruct((s, (nj - 1) * tn), BF16),
        compiler_params=_cparams("arbitrary", "arbitrary"),
        name="moe_up_dense",
    )(block_expert, n_used, xs, w_up, b_glu, b_lin, perm)
    return [act0, rest]


def _moe_down_kernel(be_ref, nu_ref, *refs):
    *a_refs, w_ref, b_ref, o_ref, wbf_ref = refs
    b = pl.program_id(1)

    @pl.when(_new_expert(be_ref, b))
    def _():
        wbf_ref[...] = w_ref[...].astype(BF16)

    @pl.when(b < nu_ref[0])
    def _():
        acc = b_ref[...]
        k0 = 0
        for a_ref in a_refs:
            kw = a_ref.shape[1]
            acc = acc + jnp.dot(a_ref[...], wbf_ref[k0:k0 + kw, :], preferred_element_type=F32)
            k0 += kw
        o_ref[...] = acc

    @pl.when(b >= nu_ref[0])
    def _():
        o_ref[...] = jnp.zeros_like(o_ref)


def _moe_down(acts, block_expert, n_used, w_down, b_down, layer):
    s = acts[0].shape[0]
    de, d = w_down.shape[2:]
    nb = s // MOE_ROWS
    tn = min(1024, d)
    return pl.pallas_call(
        _moe_down_kernel,
        grid_spec=pltpu.PrefetchScalarGridSpec(
            num_scalar_prefetch=2,
            grid=(d // tn, nb),
            in_specs=[pl.BlockSpec((MOE_ROWS, a.shape[1]), lambda j, b, be, nu: (b, 0))
                      for a in acts]
            + [pl.BlockSpec((None, None, de, tn), lambda j, b, be, nu: (layer, be[b], 0, j)),
               pl.BlockSpec((None, None, 1, tn), lambda j, b, be, nu: (layer, be[b], 0, j))],
            out_specs=pl.BlockSpec((MOE_ROWS, tn), lambda j, b, be, nu: (b, j)),
            scratch_shapes=[pltpu.VMEM((de, tn), BF16)],
        ),
        out_shape=jax.ShapeDtypeStruct((s, d), F32),
        compiler_params=_cparams("arbitrary", "arbitrary"),
        name="moe_down",
    )(block_expert, n_used, *acts, w_down, b_down)


def _combine_kernel(dest0_ref, destn_ref, ys_hbm, r_ref, x_ref, g_ref, o_ref, buf, sem, *, tm):
    i = pl.program_id(0)
    slot = i % 2

    def fetch(dest_ref, slot):
        for r in range(tm):
            for k in range(TOP_K):
                pltpu.make_async_copy(ys_hbm.at[pl.ds(dest_ref[0, 0, r * TOP_K + k], 1)],
                                      buf.at[slot, k, pl.ds(r, 1)], sem.at[slot]).start()

    def drain(slot):
        pltpu.make_async_copy(buf.at[slot], buf.at[slot], sem.at[slot]).wait()

    @pl.when(i == 0)
    def _():
        fetch(dest0_ref, 0)

    drain(slot)
    fetch(destn_ref, 1 - slot)
    route = r_ref[...]
    acc = route[:, TOP_K:TOP_K + 1] * buf[slot, 0]
    for k in range(1, TOP_K):
        acc = acc + route[:, TOP_K + k:TOP_K + k + 1] * buf[slot, k]
    o_ref[...] = x_ref[...] + g_ref[...] * acc

    @pl.when(i == pl.num_programs(0) - 1)
    def _():
        drain(1 - slot)


def _moe_combine(ys, dest, route, x2d, modv, layer, row_fn, tm):
    t, d = x2d.shape
    nt = t // tm
    dest3 = dest.reshape(nt, 1, tm * TOP_K)
    dspec = lambda fn: pl.BlockSpec((1, 1, tm * TOP_K), fn, memory_space=pltpu.SMEM)
    return pl.pallas_call(
        functools.partial(_combine_kernel, tm=tm),
        grid=(nt,),
        in_specs=[
            dspec(lambda i: (0, 0, 0)),
            dspec(lambda i: ((i + 1) % nt, 0, 0)),
            pl.BlockSpec(memory_space=pl.ANY),
            pl.BlockSpec((tm, LANES), lambda i: (i, 0)),
            pl.BlockSpec((tm, d), lambda i: (i, 0)),
            _mod_spec(d, 5, row_fn, layer),
        ],
        out_specs=pl.BlockSpec((tm, d), lambda i: (i, 0)),
        out_shape=jax.ShapeDtypeStruct((t, d), F32),
        scratch_shapes=[pltpu.VMEM((2, TOP_K, tm, d), F32), pltpu.SemaphoreType.DMA((2,))],
        compiler_params=_cparams("arbitrary"),
        name="moe_combine",
    )(dest3, dest3, ys, route, x2d, modv)


def _routing_tables(route, ne):
    t = route.shape[0]
    flat_e = route[:, :TOP_K].astype(jnp.int32).reshape(-1)
    n_rows = t * TOP_K
    onehot = (flat_e[:, None] == jnp.arange(ne, dtype=jnp.int32)[None, :]).astype(jnp.int32)
    csum = jnp.cumsum(onehot, axis=0)
    rank = jnp.take_along_axis(csum, flat_e[:, None], axis=1)[:, 0] - 1
    counts = csum[-1]
    padded = (counts + MOE_ROWS - 1) // MOE_ROWS * MOE_ROWS
    pad_end = jnp.cumsum(padded)
    pad_start = pad_end - padded
    dest = pad_start[flat_e] + rank
    n_blocks = -(-n_rows // MOE_ROWS) + ne
    slot_tok = jnp.zeros((n_blocks * MOE_ROWS,), jnp.int32).at[dest].set(
        jnp.arange(n_rows, dtype=jnp.int32) // TOP_K)
    block_expert = jnp.minimum(
        jnp.searchsorted(pad_end, jnp.arange(n_blocks, dtype=jnp.int32) * MOE_ROWS, side="right"),
        ne - 1).astype(jnp.int32)
    n_used = (pad_end[-1:] // MOE_ROWS).astype(jnp.int32)
    return dest.astype(jnp.int32), slot_tok, block_expert, n_used


def _moe_experts(f2d, route, w_up, b_up, w_down, b_down, layer):
    depth, ne = w_up.shape[:2]
    dest, slot_tok, block_expert, n_used = _routing_tables(route, ne)
    b_pairs = b_up.reshape(depth, ne, 1, -1, 2)
    col = jnp.arange(2 * LANES)
    src = jnp.where(col < LANES, 2 * col, 2 * (col - LANES) + 1)
    perm = (jnp.arange(2 * LANES)[:, None] == src[None, :]).astype(BF16)
    acts = _moe_up(f2d, slot_tok, block_expert, n_used, w_up, b_pairs[..., 0], b_pairs[..., 1],
                   perm, layer)
    ys = _moe_down(acts, block_expert, n_used, w_down, b_down.reshape(depth, ne, 1, -1), layer)
    return ys, dest


def _mla_proj_kernel(x_ref, g_ref, sh_ref, sc_ref, win_ref, gq_ref, gkv_ref, wq_ref, wkv_ref,
                     q_ref, kv_ref, kpe_ref, *, qp, kvp):
    h = _norm_mod(x_ref[...], g_ref[...], sh_ref[...], sc_ref[...]).astype(BF16)
    a = jnp.dot(h, win_ref[...], preferred_element_type=F32)
    cq = a[:, :qp]
    ckv = a[:, qp:qp + kvp]
    kpe_ref[...] = a[:, qp + kvp:]
    cqn = cq * lax.rsqrt(jnp.sum(cq * cq, axis=-1, keepdims=True) * (1.0 / Q_LORA_RANK) + NORM_EPS)
    ckvn = ckv * lax.rsqrt(jnp.sum(ckv * ckv, axis=-1, keepdims=True) * (1.0 / KV_LORA_RANK)
                           + NORM_EPS)
    q_ref[...] = jnp.dot((cqn * gq_ref[...]).astype(BF16), wq_ref[...], preferred_element_type=F32)
    kv_ref[...] = jnp.dot((ckvn * gkv_ref[...]).astype(BF16), wkv_ref[...],
                          preferred_element_type=F32)


def _mla_proj(x2d, g, modv, layer, row_fn, win_p, gq_p, gkv_p, wq_p, wkv_p, tm):
    t, d = x2d.shape
    qp, kvp = gq_p.shape[1], gkv_p.shape[1]
    na = win_p.shape[1]
    nq, nkv = wq_p.shape[1], wkv_p.shape[1]
    full = lambda arr: pl.BlockSpec(arr.shape, lambda i: (0, 0))
    return pl.pallas_call(
        functools.partial(_mla_proj_kernel, qp=qp, kvp=kvp),
        grid=(t // tm,),
        in_specs=[
            pl.BlockSpec((tm, d), lambda i: (i, 0)),
            pl.BlockSpec((1, d), lambda i: (0, 0)),
            _mod_spec(d, 0, row_fn, layer),
            _mod_spec(d, 1, row_fn, layer),
            full(win_p), full(gq_p), full(gkv_p), full(wq_p), full(wkv_p),
        ],
        out_specs=[pl.BlockSpec((tm, nq), lambda i: (i, 0)),
                   pl.BlockSpec((tm, nkv), lambda i: (i, 0)),
                   pl.BlockSpec((tm, na - qp - kvp), lambda i: (i, 0))],
        out_shape=[jax.ShapeDtypeStruct((t, nq), F32), jax.ShapeDtypeStruct((t, nkv), F32),
                   jax.ShapeDtypeStruct((t, na - qp - kvp), F32)],
        compiler_params=_cparams("arbitrary"),
        name="mla_projections",
    )(x2d, g, modv, modv, win_p, gq_p, gkv_p, wq_p, wkv_p)


def _rope(x, cos, sin_lo, sin_hi):
    quarter = QK_ROPE_DIM // 4
    return (x * cos + pltpu.roll(x, LANES - quarter, 1) * sin_lo
            + pltpu.roll(x, quarter, 1) * sin_hi)


def _headnorm_kernel(q_ref, kv_ref, kpe_ref, gq_ref, gk_ref, cos_ref, slo_ref, shi_ref,
                     qo_ref, ko_ref, vo_ref, *, heads, q_scale):
    inv = 1.0 / (QK_NOPE_DIM + QK_ROPE_DIM)
    cos, slo, shi = cos_ref[...], slo_ref[...], shi_ref[...]
    kpe = kpe_ref[...]
    kpe_ss = jnp.sum(kpe * kpe, axis=-1, keepdims=True)
    gq = gq_ref[...]
    gk = gk_ref[...]
    for h in range(heads):
        qh = q_ref[:, h * HEAD_SLOT:(h + 1) * HEAD_SLOT]
        rs = lax.rsqrt(jnp.sum(qh * qh, axis=-1, keepdims=True) * inv + NORM_EPS) * q_scale
        qn = qh * rs * gq
        qo_ref[h, :, :LANES] = qn[:, :LANES].astype(BF16)
        qo_ref[h, :, LANES:] = _rope(qn[:, LANES:], cos, slo, shi).astype(BF16)
        kn = kv_ref[:, h * HEAD_SLOT:h * HEAD_SLOT + QK_NOPE_DIM]
        rk = lax.rsqrt((jnp.sum(kn * kn, axis=-1, keepdims=True) + kpe_ss) * inv + NORM_EPS)
        ko_ref[h, :, :LANES] = (kn * rk * gk[:, :LANES]).astype(BF16)
        ko_ref[h, :, LANES:] = _rope(kpe * rk * gk[:, LANES:], cos, slo, shi).astype(BF16)
        vo_ref[h] = kv_ref[:, h * HEAD_SLOT + QK_NOPE_DIM:(h + 1) * HEAD_SLOT].astype(BF16)


def _head_norm_rope(q_raw, kv_raw, kpe, gq_slot, gk_slot, cos, slo, shi, bsz, n_ctx, n_lat, tm):
    heads = MLA_HEADS
    n_keys = n_ctx + n_lat
    ctx_tiles = bsz * n_ctx // tm
    per_ctx = n_ctx // tm
    per_lat = n_lat // tm

    def batch_of(i):
        return jnp.where(i < ctx_tiles, i // per_ctx, (i - ctx_tiles) // per_lat)

    def key_blk(i):
        return jnp.where(i < ctx_tiles, i % per_ctx, per_ctx + (i - ctx_tiles) % per_lat)

    tab = pl.BlockSpec((tm, LANES), lambda i: (key_blk(i), 0))
    t = q_raw.shape[0]
    out_map = lambda i: (batch_of(i), 0, key_blk(i), 0)

    def q_map(i):
        blk = jnp.where(i < ctx_tiles, per_lat + i % per_ctx, (i - ctx_tiles) % per_lat)
        return (batch_of(i), 0, blk, 0)

    return pl.pallas_call(
        functools.partial(_headnorm_kernel, heads=heads,
                          q_scale=float(QK_NOPE_DIM + QK_ROPE_DIM) ** -0.5),
        grid=(t // tm,),
        in_specs=[
            pl.BlockSpec((tm, heads * HEAD_SLOT), lambda i: (i, 0)),
            pl.BlockSpec((tm, heads * HEAD_SLOT), lambda i: (i, 0)),
            pl.BlockSpec((tm, LANES), lambda i: (i, 0)),
            pl.BlockSpec((1, HEAD_SLOT), lambda i: (0, 0)),
            pl.BlockSpec((1, HEAD_SLOT), lambda i: (0, 0)),
            tab, tab, tab,
        ],
        out_specs=[pl.BlockSpec((None, heads, tm, HEAD_SLOT), q_map),
                   pl.BlockSpec((None, heads, tm, HEAD_SLOT), out_map),
                   pl.BlockSpec((None, heads, tm, V_HEAD_DIM), out_map)],
        out_shape=[jax.ShapeDtypeStruct((bsz, heads, n_keys, HEAD_SLOT), BF16),
                   jax.ShapeDtypeStruct((bsz, heads, n_keys, HEAD_SLOT), BF16),
                   jax.ShapeDtypeStruct((bsz, heads, n_keys, V_HEAD_DIM), BF16)],
        compiler_params=_cparams("arbitrary"),
        name="mla_headnorm_rope",
    )(q_raw, kv_raw, kpe, gq_slot, gk_slot, cos, slo, shi)


def _attn_kernel(q_ref, k_ref, v_ref, o_ref, *, sub):
    for r in range(q_ref.shape[0] // sub):
        rows = pl.ds(r * sub, sub)
        s = lax.dot_general(q_ref[rows, :], k_ref[...], (((1,), (1,)), ((), ())),
                            preferred_element_type=F32)
        m = jnp.max(s, axis=-1, keepdims=True)
        p = jnp.exp(s - m)
        l = jnp.sum(p, axis=-1, keepdims=True)
        o = jnp.dot(p.astype(BF16), v_ref[...], preferred_element_type=F32)
        o_ref[rows, :] = (o / l).astype(o_ref.dtype)


def _attention(q, k, v, n_ctx, tq):
    bsz, heads, n_keys, _ = k.shape
    n_lat = n_keys - n_ctx
    return pl.pallas_call(
        functools.partial(_attn_kernel, sub=min(256, tq)),
        grid=(bsz, heads, n_lat // tq),
        in_specs=[
            pl.BlockSpec((None, None, tq, HEAD_SLOT), lambda b, h, i: (b, h, i, 0)),
            pl.BlockSpec((None, None, n_keys, HEAD_SLOT), lambda b, h, i: (b, h, 0, 0)),
            pl.BlockSpec((None, None, n_keys, V_HEAD_DIM), lambda b, h, i: (b, h, 0, 0)),
        ],
        out_specs=pl.BlockSpec((None, tq, V_HEAD_DIM), lambda b, h, i: (b, i, h)),
        out_shape=jax.ShapeDtypeStruct((bsz, n_lat, heads * V_HEAD_DIM), BF16),
        compiler_params=_cparams("arbitrary", "arbitrary", "arbitrary"),
        name="mla_attention",
    )(q, k, v)


def _dft_tables(n):
    k = jnp.arange(n, dtype=jnp.int32)
    ang = ((k[:, None] * k[None, :]) % n).astype(F32) * (2.0 * math.pi / n)
    return jnp.cos(ang), jnp.sin(ang)


def _rope_tables(n_ctx, n_lat):
    rows = n_lat // GRID_W
    pairs = QK_ROPE_DIM // 4
    row = jnp.repeat(jnp.arange(rows, dtype=F32), GRID_W)
    col = jnp.tile(jnp.arange(GRID_W, dtype=F32), rows)
    inv = ROPE_THETA ** (-jnp.arange(pairs, dtype=F32) / pairs)
    ang = jnp.stack([row[:, None] * inv, col[:, None] * inv], axis=1)
    cos, sin = jnp.cos(ang), jnp.sin(ang)
    zero = jnp.zeros_like(sin)
    cos_t = jnp.stack([cos, cos], axis=2).reshape(n_lat, QK_ROPE_DIM)
    slo_t = jnp.stack([-sin, zero], axis=2).reshape(n_lat, QK_ROPE_DIM)
    shi_t = jnp.stack([zero, sin], axis=2).reshape(n_lat, QK_ROPE_DIM)

    def full(tab, ctx_val):
        tab = jnp.pad(tab, ((0, 0), (0, LANES - QK_ROPE_DIM)))
        return jnp.concatenate([jnp.full((n_ctx, LANES), ctx_val, F32), tab], axis=0)

    return full(cos_t, 1.0), full(slo_t, 0.0), full(shi_t, 0.0)


def _pad_cols(w, n):
    return jnp.pad(w, ((0, 0), (0, n - w.shape[1])))


def _mla_weights(w_in, g_q_lora, w_q_up, g_kv_lora, w_kv_up, g_q_head, g_k_head):
    qp = -(-Q_LORA_RANK // LANES) * LANES
    kvp = -(-KV_LORA_RANK // LANES) * LANES
    qr, kvr = Q_LORA_RANK, KV_LORA_RANK
    win_p = jnp.concatenate([_pad_cols(w_in[:, :qr], qp), _pad_cols(w_in[:, qr:qr + kvr], kvp),
                             _pad_cols(w_in[:, qr + kvr:], LANES)], axis=1).astype(BF16)
    gq_p = _pad_cols(g_q_lora[None, :], qp)
    gkv_p = _pad_cols(g_kv_lora[None, :], kvp)
    hd = QK_NOPE_DIM + QK_ROPE_DIM
    wq = w_q_up.reshape(qr, MLA_HEADS, hd)
    wq = jnp.pad(wq, ((0, qp - qr), (0, 0), (0, HEAD_SLOT - hd)))
    wq_p = wq.reshape(qp, MLA_HEADS * HEAD_SLOT).astype(BF16)
    wkv_p = jnp.pad(w_kv_up, ((0, kvp - kvr), (0, 0))).astype(BF16)
    gq_slot = _pad_cols(g_q_head[None, :], HEAD_SLOT)
    gk_slot = _pad_cols(g_k_head[None, :], HEAD_SLOT)
    return win_p, gq_p, gkv_p, wq_p, wkv_p, gq_slot, gk_slot


def kernel(x, c, ctx, c_ctx, w_mod, b_mod, g_mix, g_ffn, fourier_w_out, fourier_b_out, mla_w_in, mla_g_q_lora, mla_w_q_up, mla_g_kv_lora, mla_w_kv_up, mla_g_q_head, mla_g_k_head, mla_w_out, router_w, router_b, expert_w_up, expert_b_up, expert_w_down, expert_b_down):
    bsz, n, d = x.shape
    l = ctx.shape[1]
    assert bsz < 8 and n % GRID_W == 0
    tl = min(256, l)
    assert l % tl == 0 and n % tl == 0
    ctx_row = bsz

    c8 = jnp.zeros((8, d), F32).at[:bsz].set(c).at[ctx_row].set(c_ctx)
    modv = _mod_vectors(c8, w_mod, b_mod).reshape(w_mod.shape[0], 8, 1, 6 * d)

    def lat_row(tm):
        return lambda i: i // (n // tm)

    def all_row(tm):
        nct = bsz * l // tm
        return lambda i: jnp.where(i < nct, ctx_row, (i - nct) // (n // tm))

    def ctx_only_row(i):
        return ctx_row

    dg = d // FOURIER_GROUPS
    cc, sc_ = _dft_tables(dg)
    cs = jnp.concatenate([cc, sc_], axis=1).astype(BF16)
    wf = fourier_w_out[0].astype(BF16)
    bf = fourier_b_out[0][None, :]
    g0 = g_mix[0][None, :]

    def fourier(x3, row_fn_of, tm):
        bb, nn, _ = x3.shape
        x2 = x3.reshape(bb * nn, d)
        a, b = _channel_dft(x2, g0, modv, 0, row_fn_of, cs, tm)
        cn, sn = _dft_tables(nn)
        y = _seq_dft(a.reshape(bb, nn, d), b.reshape(bb, nn, d), cn.astype(BF16),
                     (-sn).astype(BF16), 1.0 / math.sqrt(nn * dg))
        return _mm_residual(y.reshape(bb * nn, d), wf, bf, x2, modv, 0, 2, row_fn_of, tm)

    tm_lat = min(512, n)
    x_lat1 = fourier(x, lat_row(tm_lat), tm_lat)
    x_ctx1 = fourier(ctx, ctx_only_row, tl)
    x_all = jnp.concatenate([x_ctx1, x_lat1], axis=0)
    n_ctx_tok = bsz * l

    f0, route0 = _ffn_route(x_all, g_ffn[0][None, :], modv, 0, all_row(tl), router_w[0],
                            router_b[0][None, :], tl)
    ys0, dest0 = _moe_experts(f0, route0, expert_w_up, expert_b_up, expert_w_down,
                              expert_b_down, 0)
    tc = min(128, tl)
    x_all2 = _moe_combine(ys0, dest0, route0, x_all, modv, 0, all_row(tc), tc)

    win_p, gq_p, gkv_p, wq_p, wkv_p, gq_slot, gk_slot = _mla_weights(
        mla_w_in[0], mla_g_q_lora[0], mla_w_q_up[0], mla_g_kv_lora[0], mla_w_kv_up[0],
        mla_g_q_head[0], mla_g_k_head[0])
    q_raw, kv_raw, kpe = _mla_proj(x_all2, g_mix[1][None, :], modv, 1, all_row(tl),
                                   win_p, gq_p, gkv_p, wq_p, wkv_p, tl)
    cos, slo, shi = _rope_tables(l, n)
    qh, kh, vh = _head_norm_rope(q_raw, kv_raw, kpe, gq_slot, gk_slot, cos, slo, shi,
                                 bsz, l, n, tl)
    attn = _attention(qh, kh, vh, l, min(512, n))
    x_lat2 = x_all2[n_ctx_tok:]
    zero_bias = jnp.zeros((1, d), F32)
    x_lat3 = _mm_residual(attn.reshape(bsz * n, -1), mla_w_out[0].astype(BF16), zero_bias,
                          x_lat2, modv, 1, 2, lat_row(tm_lat), tm_lat)

    f1, route1 = _ffn_route(x_lat3, g_ffn[1][None, :], modv, 1, lat_row(tl), router_w[1],
                            router_b[1][None, :], tl)
    ys1, dest1 = _moe_experts(f1, route1, expert_w_up, expert_b_up, expert_w_down,
                              expert_b_down, 1)
    out = _moe_combine(ys1, dest1, route1, x_lat3, modv, 1, lat_row(tc), tc)
    return out.reshape(bsz, n, d)
```

```python
import functools
import math

import jax
import jax.numpy as jnp
from jax import lax
from jax.experimental import pallas as pl
from jax.experimental.pallas import tpu as pltpu

GRID_W = 64
FOURIER_GROUPS = 4
MLA_HEADS = 16
Q_LORA_RANK = 448
KV_LORA_RANK = 512
QK_NOPE_DIM = 128
QK_ROPE_DIM = 64
V_HEAD_DIM = 128
ROPE_THETA = 10000.0
TOP_K = 4
SWIGLU_ALPHA = 1.702
SWIGLU_LIMIT = 7.0
NORM_EPS = 1e-6

LANES = 128
MOE_ROWS = 256
HEAD_SLOT = 256
VMEM_LIMIT = 56 << 20

F32 = jnp.float32
BF16 = jnp.bfloat16
HIGHEST = lax.Precision.HIGHEST


def _cparams(*sem, vmem=VMEM_LIMIT):
    return pltpu.CompilerParams(dimension_semantics=sem, vmem_limit_bytes=vmem)


def _norm_mod(x, g, sh, sc):
    ms = jnp.mean(x * x, axis=-1, keepdims=True)
    return (x * lax.rsqrt(ms + NORM_EPS)) * g * (1.0 + sc) + sh


def _mod_spec(d, piece, row_fn, layer):
    return pl.BlockSpec((None, None, 1, d), lambda *ids: (layer, row_fn(*ids), 0, piece))


def _mod_kernel(c_ref, w_ref, b_ref, o_ref):
    c = c_ref[...]
    a = c * jax.nn.sigmoid(c)
    o_ref[...] = jnp.dot(a, w_ref[...], preferred_element_type=F32, precision=HIGHEST) + b_ref[...]


def _mod_vectors(c8, w_mod, b_mod):
    depth, d, n6 = w_mod.shape
    tn = next(t for t in (1024, 512, 256, 128) if n6 % t == 0)
    return pl.pallas_call(
        _mod_kernel,
        grid=(depth, n6 // tn),
        in_specs=[
            pl.BlockSpec((8, d), lambda l, j: (0, 0)),
            pl.BlockSpec((None, d, tn), lambda l, j: (l, 0, j)),
            pl.BlockSpec((None, 1, tn), lambda l, j: (l, 0, j)),
        ],
        out_specs=pl.BlockSpec((None, 8, tn), lambda l, j: (l, 0, j)),
        out_shape=jax.ShapeDtypeStruct((depth, 8, n6), F32),
        compiler_params=_cparams("arbitrary", "arbitrary"),
        name="mod_vectors",
    )(c8, w_mod, b_mod.reshape(depth, 1, n6))


def _chdft_kernel(x_ref, g_ref, sh_ref, sc_ref, cs_ref, a_ref, b_ref, *, groups):
    h = _norm_mod(x_ref[...], g_ref[...], sh_ref[...], sc_ref[...]).astype(BF16)
    dg = h.shape[-1] // groups
    for gi in range(groups):
        r = jnp.dot(h[:, gi * dg:(gi + 1) * dg], cs_ref[...], preferred_element_type=F32)
        a_ref[:, gi * dg:(gi + 1) * dg] = r[:, :dg].astype(BF16)
        b_ref[:, gi * dg:(gi + 1) * dg] = r[:, dg:].astype(BF16)


def _channel_dft(x2d, g, modv, layer, row_fn, cs, tm):
    t, d = x2d.shape
    spec = pl.BlockSpec((tm, d), lambda i: (i, 0))
    return pl.pallas_call(
        functools.partial(_chdft_kernel, groups=FOURIER_GROUPS),
        grid=(t // tm,),
        in_specs=[
            spec,
            pl.BlockSpec((1, d), lambda i: (0, 0)),
            _mod_spec(d, 0, row_fn, layer),
            _mod_spec(d, 1, row_fn, layer),
            pl.BlockSpec(cs.shape, lambda i: (0, 0)),
        ],
        out_specs=[spec, spec],
        out_shape=[jax.ShapeDtypeStruct((t, d), BF16)] * 2,
        compiler_params=_cparams("arbitrary"),
        name="channel_dft",
    )(x2d, g, modv, modv, cs)


def _seqdft_kernel(cn_ref, sn_ref, a_ref, b_ref, y_ref, acc_ref, *, scale):
    k = pl.program_id(2)

    @pl.when(k == 0)
    def _():
        acc_ref[...] = jnp.zeros_like(acc_ref)

    acc_ref[...] += (jnp.dot(cn_ref[...], a_ref[...], preferred_element_type=F32)
                     + jnp.dot(sn_ref[...], b_ref[...], preferred_element_type=F32))

    @pl.when(k == pl.num_programs(2) - 1)
    def _():
        y_ref[...] = (acc_ref[...] * scale).astype(y_ref.dtype)


def _seq_dft(a, b, cn, sn_neg, scale):
    bsz, n, d = a.shape
    tm = min(1024, n)
    tk = min(512, n)
    mat = pl.BlockSpec((tm, tk), lambda bi, i, k: (i, k))
    rhs = pl.BlockSpec((None, tk, d), lambda bi, i, k: (bi, k, 0))
    return pl.pallas_call(
        functools.partial(_seqdft_kernel, scale=scale),
        grid=(bsz, n // tm, n // tk),
        in_specs=[mat, mat, rhs, rhs],
        out_specs=pl.BlockSpec((None, tm, d), lambda bi, i, k: (bi, i, 0)),
        out_shape=jax.ShapeDtypeStruct((bsz, n, d), BF16),
        scratch_shapes=[pltpu.VMEM((tm, d), F32)],
        compiler_params=_cparams("arbitrary", "arbitrary", "arbitrary"),
        name="sequence_dft",
    )(cn, sn_neg, a, b)


def _mm_res_kernel(y_ref, w_ref, b_ref, r_ref, g_ref, o_ref):
    acc = jnp.dot(y_ref[...], w_ref[...], preferred_element_type=F32)
    o_ref[...] = r_ref[...] + g_ref[...] * (acc + b_ref[...])


def _mm_residual(y2d, w_bf, bias, res2d, modv, layer, piece, row_fn, tm):
    t, k = y2d.shape
    d = w_bf.shape[1]
    tn = min(1024, d)
    nj = d // tn
    return pl.pallas_call(
        _mm_res_kernel,
        grid=(nj, t // tm),
        in_specs=[
            pl.BlockSpec((tm, k), lambda j, i: (i, 0)),
            pl.BlockSpec((k, tn), lambda j, i: (0, j)),
            pl.BlockSpec((1, tn), lambda j, i: (0, j)),
            pl.BlockSpec((tm, tn), lambda j, i: (i, j)),
            pl.BlockSpec((None, None, 1, tn), lambda j, i: (layer, row_fn(i), 0, piece * nj + j)),
        ],
        out_specs=pl.BlockSpec((tm, tn), lambda j, i: (i, j)),
        out_shape=jax.ShapeDtypeStruct((t, d), F32),
        compiler_params=_cparams("arbitrary", "arbitrary"),
        name="mixer_out_residual",
    )(y2d, w_bf, bias, res2d, modv)


def _route_kernel(x_ref, g_ref, sh_ref, sc_ref, wr_ref, br_ref, f_ref, r_ref, cnt_ref):
    @pl.when(pl.program_id(0) == 0)
    def _():
        cnt_ref[...] = jnp.zeros_like(cnt_ref)

    f = _norm_mod(x_ref[...], g_ref[...], sh_ref[...], sc_ref[...])
    f_ref[...] = f
    logits = jnp.dot(f, wr_ref[...], preferred_element_type=F32, precision=HIGHEST) + br_ref[...]
    tm, ne = logits.shape
    col = lax.broadcasted_iota(jnp.int32, (tm, ne), 1).astype(F32)
    lane = lax.broadcasted_iota(jnp.int32, (tm, LANES), 1)
    out = jnp.zeros((tm, LANES), F32)
    vals, idxs = [], []
    hot = jnp.zeros((tm, ne), F32)
    for k in range(TOP_K):
        m = jnp.max(logits, axis=-1, keepdims=True)
        idx = jnp.min(jnp.where(logits == m, col, float(ne)), axis=-1, keepdims=True)
        logits = jnp.where(col == idx, -jnp.inf, logits)
        hot = jnp.where(col == idx, 1.0, hot)
        out = jnp.where(lane == k, idx, out)
        vals.append(m)
        idxs.append(idx)
    es = [jnp.exp(v - vals[0]) for v in vals]
    den = es[0]
    for e in es[1:]:
        den = den + e
    for k in range(TOP_K):
        out = jnp.where(lane == TOP_K + k, es[k] / den, out)
    earlier = (lax.broadcasted_iota(jnp.int32, (tm, tm), 1)
               < lax.broadcasted_iota(jnp.int32, (tm, tm), 0))
    before = jnp.dot(jnp.where(earlier, 1.0, 0.0).astype(BF16), hot.astype(BF16),
                     preferred_element_type=F32) + cnt_ref[:, :ne]
    for k in range(TOP_K):
        rank = jnp.sum(jnp.where(col == idxs[k], before, 0.0), axis=-1, keepdims=True)
        out = jnp.where(lane == 2 * TOP_K + k, rank, out)
    cnt_ref[:, :ne] = cnt_ref[:, :ne] + jnp.sum(hot, axis=0, keepdims=True)
    r_ref[...] = out


def _ffn_route(x2d, g, modv, layer, row_fn, w_router, b_router, tm):
    t, d = x2d.shape
    ne = w_router.shape[1]
    spec = pl.BlockSpec((tm, d), lambda i: (i, 0))
    return pl.pallas_call(
        _route_kernel,
        grid=(t // tm,),
        in_specs=[
            spec,
            pl.BlockSpec((1, d), lambda i: (0, 0)),
            _mod_spec(d, 3, row_fn, layer),
            _mod_spec(d, 4, row_fn, layer),
            pl.BlockSpec((d, ne), lambda i: (0, 0)),
            pl.BlockSpec((1, ne), lambda i: (0, 0)),
        ],
        out_specs=[spec, pl.BlockSpec((tm, LANES), lambda i: (i, 0)),
                   pl.BlockSpec((1, LANES), lambda i: (0, 0))],
        out_shape=[jax.ShapeDtypeStruct((t, d), F32), jax.ShapeDtypeStruct((t, LANES), F32),
                   jax.ShapeDtypeStruct((1, LANES), F32)],
        compiler_params=_cparams("arbitrary"),
        name="ffn_norm_route",
    )(x2d, g, modv, modv, w_router, b_router)


def _gather_rows(tok_ref, f_hbm, xbuf, sem, slot):
    for r in range(MOE_ROWS):
        pltpu.make_async_copy(f_hbm.at[pl.ds(tok_ref[0, 0, r], 1)],
                              xbuf.at[slot, pl.ds(r, 1)], sem.at[slot]).start(priority=r % 2)


def _wait_rows(xbuf, sem, slot):
    pltpu.make_async_copy(xbuf.at[slot], xbuf.at[slot], sem.at[slot]).wait()


def _new_expert(be_ref, b):
    return (b == 0) | (be_ref[b] != be_ref[jnp.maximum(b - 1, 0)])


def _moe_up_kernel(be_ref, nu_ref, *refs, gather):
    if gather:
        (tok0_ref, tokn_ref, f_hbm, w_ref, bg_ref, bl_ref, perm_ref,
         o_ref, xs_ref, xbuf, wg_ref, wl_ref, sem) = refs
    else:
        x_ref, w_ref, bg_ref, bl_ref, perm_ref, o_ref, wg_ref, wl_ref = refs
    b = pl.program_id(1)
    slot = b % 2

    if gather:
        @pl.when(b == 0)
        def _():
            _gather_rows(tok0_ref, f_hbm, xbuf, sem, 0)

        _wait_rows(xbuf, sem, slot)
        _gather_rows(tokn_ref, f_hbm, xbuf, sem, 1 - slot)
        xs_ref[...] = xbuf[slot].astype(BF16)
        x_ref = xs_ref

    @pl.when(_new_expert(be_ref, b))
    def _():
        for q in range(w_ref.shape[1] // (2 * LANES)):
            wq = w_ref[:, q * 2 * LANES:(q + 1) * 2 * LANES].astype(BF16)
            r = jnp.dot(wq, perm_ref[...], preferred_element_type=F32)
            wg_ref[:, q * LANES:(q + 1) * LANES] = r[:, :LANES].astype(BF16)
            wl_ref[:, q * LANES:(q + 1) * LANES] = r[:, LANES:].astype(BF16)

    @pl.when(b < nu_ref[0])
    def _():
        x = x_ref[...]
        glu = jnp.dot(x, wg_ref[...], preferred_element_type=F32) + bg_ref[...]
        lin = jnp.dot(x, wl_ref[...], preferred_element_type=F32) + bl_ref[...]
        glu = jnp.minimum(glu, SWIGLU_LIMIT)
        lin = jnp.clip(lin, -SWIGLU_LIMIT, SWIGLU_LIMIT)
        o_ref[...] = (glu * jax.nn.sigmoid(SWIGLU_ALPHA * glu) * (lin + 1.0)).astype(o_ref.dtype)

    @pl.when(b >= nu_ref[0])
    def _():
        o_ref[...] = jnp.zeros_like(o_ref)

    if gather:
        @pl.when(b == pl.num_programs(1) - 1)
        def _():
            _wait_rows(xbuf, sem, 1 - slot)


def _moe_up(f2d, slot_tok, block_expert, n_used, w_up, b_glu, b_lin, perm, layer):
    d = f2d.shape[1]
    nb = slot_tok.shape[0] // MOE_ROWS
    s = nb * MOE_ROWS
    de = w_up.shape[3] // 2
    tn = min(1024, de)
    nj = de // tn
    tok3 = slot_tok.reshape(nb, 1, MOE_ROWS)
    tok_spec = lambda fn: pl.BlockSpec((1, 1, MOE_ROWS), fn, memory_space=pltpu.SMEM)

    def common_specs(j0):
        bspec = pl.BlockSpec((None, None, 1, tn), lambda j, b, be, nu: (layer, be[b], 0, j + j0))
        return [pl.BlockSpec((None, None, d, 2 * tn),
                             lambda j, b, be, nu: (layer, be[b], 0, j + j0)),
                bspec, bspec, pl.BlockSpec(perm.shape, lambda j, b, be, nu: (0, 0))]

    row_blk = lambda j, b, be, nu: (b, 0)
    w_scratch = [pltpu.VMEM((d, tn), BF16), pltpu.VMEM((d, tn), BF16)]
    act0, xs = pl.pallas_call(
        functools.partial(_moe_up_kernel, gather=True),
        grid_spec=pltpu.PrefetchScalarGridSpec(
            num_scalar_prefetch=2,
            grid=(1, nb),
            in_specs=[tok_spec(lambda j, b, be, nu: (0, 0, 0)),
                      tok_spec(lambda j, b, be, nu: ((b + 1) % nb, 0, 0)),
                      pl.BlockSpec(memory_space=pl.ANY)] + common_specs(0),
            out_specs=[pl.BlockSpec((MOE_ROWS, tn), row_blk), pl.BlockSpec((MOE_ROWS, d), row_blk)],
            scratch_shapes=[pltpu.VMEM((2, MOE_ROWS, d), F32)] + w_scratch
            + [pltpu.SemaphoreType.DMA((2,))],
        ),
        out_shape=[jax.ShapeDtypeStruct((s, tn), BF16), jax.ShapeDtypeStruct((s, d), BF16)],
        compiler_params=_cparams("arbitrary", "arbitrary"),
        name="moe_up_gather",
    )(block_expert, n_used, tok3, tok3, f2d, w_up, b_glu, b_lin, perm)
    if nj == 1:
        return [act0]
    rest = pl.pallas_call(
        functools.partial(_moe_up_kernel, gather=False),
        grid_spec=pltpu.PrefetchScalarGridSpec(
            num_scalar_prefetch=2,
            grid=(nj - 1, nb),
            in_specs=[pl.BlockSpec((MOE_ROWS, d), row_blk)] + common_specs(1),
            out_specs=pl.BlockSpec((MOE_ROWS, tn), lambda j, b, be, nu: (b, j)),
            scratch_shapes=w_scratch,
        ),
        out_shape=jax.ShapeDtypeStruct((s, (nj - 1) * tn), BF16),
        compiler_params=_cparams("arbitrary", "arbitrary"),
        name="moe_up_dense",
    )(block_expert, n_used, xs, w_up, b_glu, b_lin, perm)
    return [act0, rest]


def _moe_down_kernel(be_ref, nu_ref, *refs):
    *a_refs, w_ref, b_ref, o_ref, wbf_ref = refs
    b = pl.program_id(1)

    @pl.when(_new_expert(be_ref, b))
    def _():
        wbf_ref[...] = w_ref[...].astype(BF16)

    @pl.when(b < nu_ref[0])
    def _():
        acc = b_ref[...]
        k0 = 0
        for a_ref in a_refs:
            kw = a_ref.shape[1]
            acc = acc + jnp.dot(a_ref[...], wbf_ref[k0:k0 + kw, :], preferred_element_type=F32)
            k0 += kw
        o_ref[...] = acc

    @pl.when(b >= nu_ref[0])
    def _():
        o_ref[...] = jnp.zeros_like(o_ref)


def _moe_down(acts, block_expert, n_used, w_down, b_down, layer):
    s = acts[0].shape[0]
    de, d = w_down.shape[2:]
    nb = s // MOE_ROWS
    tn = min(2048, d)
    return pl.pallas_call(
        _moe_down_kernel,
        grid_spec=pltpu.PrefetchScalarGridSpec(
            num_scalar_prefetch=2,
            grid=(d // tn, nb),
            in_specs=[pl.BlockSpec((MOE_ROWS, a.shape[1]), lambda j, b, be, nu: (b, 0))
                      for a in acts]
            + [pl.BlockSpec((None, None, de, tn), lambda j, b, be, nu: (layer, be[b], 0, j)),
               pl.BlockSpec((None, None, 1, tn), lambda j, b, be, nu: (layer, be[b], 0, j))],
            out_specs=pl.BlockSpec((MOE_ROWS, tn), lambda j, b, be, nu: (b, j)),
            scratch_shapes=[pltpu.VMEM((de, tn), BF16)],
        ),
        out_shape=jax.ShapeDtypeStruct((s, d), F32),
        compiler_params=_cparams("arbitrary", "arbitrary"),
        name="moe_down",
    )(block_expert, n_used, *acts, w_down, b_down)


def _combine_kernel(dest0_ref, destn_ref, ys_hbm, r_ref, x_ref, g_ref, o_ref, buf, sem, *, tm):
    i = pl.program_id(0)
    slot = i % 2

    def fetch(dest_ref, slot):
        for r in range(tm):
            for k in range(TOP_K):
                pltpu.make_async_copy(ys_hbm.at[pl.ds(dest_ref[0, 0, r * TOP_K + k], 1)],
                                      buf.at[slot, k, pl.ds(r, 1)], sem.at[slot]).start()

    def drain(slot):
        pltpu.make_async_copy(buf.at[slot], buf.at[slot], sem.at[slot]).wait()

    @pl.when(i == 0)
    def _():
        fetch(dest0_ref, 0)

    drain(slot)
    fetch(destn_ref, 1 - slot)
    route = r_ref[...]
    acc = route[:, TOP_K:TOP_K + 1] * buf[slot, 0]
    for k in range(1, TOP_K):
        acc = acc + route[:, TOP_K + k:TOP_K + k + 1] * buf[slot, k]
    o_ref[...] = x_ref[...] + g_ref[...] * acc

    @pl.when(i == pl.num_programs(0) - 1)
    def _():
        drain(1 - slot)


def _moe_combine(ys, dest, route, x2d, modv, layer, row_fn, tm):
    t, d = x2d.shape
    nt = t // tm
    dest3 = dest.reshape(nt, 1, tm * TOP_K)
    dspec = lambda fn: pl.BlockSpec((1, 1, tm * TOP_K), fn, memory_space=pltpu.SMEM)
    return pl.pallas_call(
        functools.partial(_combine_kernel, tm=tm),
        grid=(nt,),
        in_specs=[
            dspec(lambda i: (0, 0, 0)),
            dspec(lambda i: ((i + 1) % nt, 0, 0)),
            pl.BlockSpec(memory_space=pl.ANY),
            pl.BlockSpec((tm, LANES), lambda i: (i, 0)),
            pl.BlockSpec((tm, d), lambda i: (i, 0)),
            _mod_spec(d, 5, row_fn, layer),
        ],
        out_specs=pl.BlockSpec((tm, d), lambda i: (i, 0)),
        out_shape=jax.ShapeDtypeStruct((t, d), F32),
        scratch_shapes=[pltpu.VMEM((2, TOP_K, tm, d), F32), pltpu.SemaphoreType.DMA((2,))],
        compiler_params=_cparams("arbitrary"),
        name="moe_combine",
    )(dest3, dest3, ys, route, x2d, modv)


def _routing_tables(route, counts_f, ne):
    t = route.shape[0]
    i32 = jnp.int32
    flat_e = route[:, :TOP_K].astype(i32).reshape(-1)
    rank = route[:, 2 * TOP_K:3 * TOP_K].astype(i32).reshape(-1)
    counts = counts_f[0, :ne].astype(i32)
    n_rows = t * TOP_K
    padded = (counts + MOE_ROWS - 1) // MOE_ROWS * MOE_ROWS
    pad_end = jnp.cumsum(padded)
    pad_start = pad_end - padded
    dest = pad_start[flat_e] + rank
    n_blocks = -(-n_rows // MOE_ROWS) + ne
    blk_row0 = jnp.arange(n_blocks, dtype=i32) * MOE_ROWS
    block_expert = jnp.minimum(
        jnp.sum((pad_end[None, :] <= blk_row0[:, None]).astype(i32), axis=1), ne - 1)
    n_used = (pad_end[-1:] // MOE_ROWS).astype(i32)
    shift = max(n_rows - 1, 1).bit_length()
    assert ne << shift < 2 ** 31
    order = jnp.sort(flat_e * (1 << shift) + jnp.arange(n_rows, dtype=i32)) & ((1 << shift) - 1)
    first = jnp.cumsum(counts) - counts
    local = (blk_row0 - pad_start[block_expert])[:, None] + jnp.arange(MOE_ROWS, dtype=i32)[None, :]
    src = jnp.clip(first[block_expert][:, None] + local, 0, n_rows - 1)
    slot_tok = jnp.where(local < counts[block_expert][:, None], order[src] // TOP_K, 0)
    return dest.astype(i32), slot_tok.reshape(-1).astype(i32), block_expert.astype(i32), n_used


def _moe_experts(f2d, route, counts_f, w_up, b_up, w_down, b_down, layer):
    depth, ne = w_up.shape[:2]
    dest, slot_tok, block_expert, n_used = _routing_tables(route, counts_f, ne)
    b_pairs = b_up.reshape(depth, ne, 1, -1, 2)
    col = jnp.arange(2 * LANES)
    src = jnp.where(col < LANES, 2 * col, 2 * (col - LANES) + 1)
    perm = (jnp.arange(2 * LANES)[:, None] == src[None, :]).astype(BF16)
    acts = _moe_up(f2d, slot_tok, block_expert, n_used, w_up, b_pairs[..., 0], b_pairs[..., 1],
                   perm, layer)
    ys = _moe_down(acts, block_expert, n_used, w_down, b_down.reshape(depth, ne, 1, -1), layer)
    return ys, dest


def _mla_proj_kernel(x_ref, g_ref, sh_ref, sc_ref, win_ref, gq_ref, gkv_ref, wq_ref, wkv_ref,
                     q_ref, kv_ref, kpe_ref, *, qp, kvp):
    h = _norm_mod(x_ref[...], g_ref[...], sh_ref[...], sc_ref[...]).astype(BF16)
    a = jnp.dot(h, win_ref[...], preferred_element_type=F32)
    cq = a[:, :qp]
    ckv = a[:, qp:qp + kvp]
    kpe_ref[...] = a[:, qp + kvp:]
    cqn = cq * lax.rsqrt(jnp.sum(cq * cq, axis=-1, keepdims=True) * (1.0 / Q_LORA_RANK) + NORM_EPS)
    ckvn = ckv * lax.rsqrt(jnp.sum(ckv * ckv, axis=-1, keepdims=True) * (1.0 / KV_LORA_RANK)
                           + NORM_EPS)
    q_ref[...] = jnp.dot((cqn * gq_ref[...]).astype(BF16), wq_ref[...], preferred_element_type=F32)
    kv_ref[...] = jnp.dot((ckvn * gkv_ref[...]).astype(BF16), wkv_ref[...],
                          preferred_element_type=F32)


def _mla_proj(x2d, g, modv, layer, row_fn, win_p, gq_p, gkv_p, wq_p, wkv_p, tm):
    t, d = x2d.shape
    qp, kvp = gq_p.shape[1], gkv_p.shape[1]
    na = win_p.shape[1]
    nq, nkv = wq_p.shape[1], wkv_p.shape[1]
    full = lambda arr: pl.BlockSpec(arr.shape, lambda i: (0, 0))
    return pl.pallas_call(
        functools.partial(_mla_proj_kernel, qp=qp, kvp=kvp),
        grid=(t // tm,),
        in_specs=[
            pl.BlockSpec((tm, d), lambda i: (i, 0)),
            pl.BlockSpec((1, d), lambda i: (0, 0)),
            _mod_spec(d, 0, row_fn, layer),
            _mod_spec(d, 1, row_fn, layer),
            full(win_p), full(gq_p), full(gkv_p), full(wq_p), full(wkv_p),
        ],
        out_specs=[pl.BlockSpec((tm, nq), lambda i: (i, 0)),
                   pl.BlockSpec((tm, nkv), lambda i: (i, 0)),
                   pl.BlockSpec((tm, na - qp - kvp), lambda i: (i, 0))],
        out_shape=[jax.ShapeDtypeStruct((t, nq), F32), jax.ShapeDtypeStruct((t, nkv), F32),
                   jax.ShapeDtypeStruct((t, na - qp - kvp), F32)],
        compiler_params=_cparams("arbitrary"),
        name="mla_projections",
    )(x2d, g, modv, modv, win_p, gq_p, gkv_p, wq_p, wkv_p)


def _rope(x, cos, sin_lo, sin_hi):
    quarter = QK_ROPE_DIM // 4
    return (x * cos + pltpu.roll(x, LANES - quarter, 1) * sin_lo
            + pltpu.roll(x, quarter, 1) * sin_hi)


def _headnorm_kernel(q_ref, kv_ref, kpe_ref, gq_ref, gk_ref, cos_ref, slo_ref, shi_ref,
                     qo_ref, ko_ref, vo_ref, *, heads, q_scale):
    inv = 1.0 / (QK_NOPE_DIM + QK_ROPE_DIM)
    cos, slo, shi = cos_ref[...], slo_ref[...], shi_ref[...]
    kpe = kpe_ref[...]
    kpe_ss = jnp.sum(kpe * kpe, axis=-1, keepdims=True)
    gq = gq_ref[...]
    gk = gk_ref[...]
    for h in range(heads):
        qh = q_ref[:, h * HEAD_SLOT:(h + 1) * HEAD_SLOT]
        rs = lax.rsqrt(jnp.sum(qh * qh, axis=-1, keepdims=True) * inv + NORM_EPS) * q_scale
        qn = qh * rs * gq
        qo_ref[h, :, :LANES] = qn[:, :LANES].astype(BF16)
        qo_ref[h, :, LANES:] = _rope(qn[:, LANES:], cos, slo, shi).astype(BF16)
        kn = kv_ref[:, h * HEAD_SLOT:h * HEAD_SLOT + QK_NOPE_DIM]
        rk = lax.rsqrt((jnp.sum(kn * kn, axis=-1, keepdims=True) + kpe_ss) * inv + NORM_EPS)
        ko_ref[h, :, :LANES] = (kn * rk * gk[:, :LANES]).astype(BF16)
        ko_ref[h, :, LANES:] = _rope(kpe * rk * gk[:, LANES:], cos, slo, shi).astype(BF16)
        vo_ref[h] = kv_ref[:, h * HEAD_SLOT + QK_NOPE_DIM:(h + 1) * HEAD_SLOT].astype(BF16)


def _head_norm_rope(q_raw, kv_raw, kpe, gq_slot, gk_slot, cos, slo, shi, bsz, n_ctx, n_lat, tm):
    heads = MLA_HEADS
    n_keys = n_ctx + n_lat
    ctx_tiles = bsz * n_ctx // tm
    per_ctx = n_ctx // tm
    per_lat = n_lat // tm

    def batch_of(i):
        return jnp.where(i < ctx_tiles, i // per_ctx, (i - ctx_tiles) // per_lat)

    def key_blk(i):
        return jnp.where(i < ctx_tiles, i % per_ctx, per_ctx + (i - ctx_tiles) % per_lat)

    tab = pl.BlockSpec((tm, LANES), lambda i: (key_blk(i), 0))
    t = q_raw.shape[0]
    out_map = lambda i: (batch_of(i), 0, key_blk(i), 0)

    def q_map(i):
        blk = jnp.where(i < ctx_tiles, per_lat + i % per_ctx, (i - ctx_tiles) % per_lat)
        return (batch_of(i), 0, blk, 0)

    return pl.pallas_call(
        functools.partial(_headnorm_kernel, heads=heads,
                          q_scale=float(QK_NOPE_DIM + QK_ROPE_DIM) ** -0.5),
        grid=(t // tm,),
        in_specs=[
            pl.BlockSpec((tm, heads * HEAD_SLOT), lambda i: (i, 0)),
            pl.BlockSpec((tm, heads * HEAD_SLOT), lambda i: (i, 0)),
            pl.BlockSpec((tm, LANES), lambda i: (i, 0)),
            pl.BlockSpec((1, HEAD_SLOT), lambda i: (0, 0)),
            pl.BlockSpec((1, HEAD_SLOT), lambda i: (0, 0)),
            tab, tab, tab,
        ],
        out_specs=[pl.BlockSpec((None, heads, tm, HEAD_SLOT), q_map),
                   pl.BlockSpec((None, heads, tm, HEAD_SLOT), out_map),
                   pl.BlockSpec((None, heads, tm, V_HEAD_DIM), out_map)],
        out_shape=[jax.ShapeDtypeStruct((bsz, heads, n_keys, HEAD_SLOT), BF16),
                   jax.ShapeDtypeStruct((bsz, heads, n_keys, HEAD_SLOT), BF16),
                   jax.ShapeDtypeStruct((bsz, heads, n_keys, V_HEAD_DIM), BF16)],
        compiler_params=_cparams("arbitrary"),
        name="mla_headnorm_rope",
    )(q_raw, kv_raw, kpe, gq_slot, gk_slot, cos, slo, shi)


def _attn_kernel(q_ref, k_ref, v_ref, o_ref, *, sub):
    for r in range(q_ref.shape[0] // sub):
        rows = pl.ds(r * sub, sub)
        s = lax.dot_general(q_ref[rows, :], k_ref[...], (((1,), (1,)), ((), ())),
                            preferred_element_type=F32)
        m = jnp.max(s, axis=-1, keepdims=True)
        p = jnp.exp(s - m)
        l = jnp.sum(p, axis=-1, keepdims=True)
        o = jnp.dot(p.astype(BF16), v_ref[...], preferred_element_type=F32)
        o_ref[rows, :] = (o / l).astype(o_ref.dtype)


def _attention(q, k, v, n_ctx, tq):
    bsz, heads, n_keys, _ = k.shape
    n_lat = n_keys - n_ctx
    return pl.pallas_call(
        functools.partial(_attn_kernel, sub=min(256, tq)),
        grid=(bsz, heads, n_lat // tq),
        in_specs=[
            pl.BlockSpec((None, None, tq, HEAD_SLOT), lambda b, h, i: (b, h, i, 0)),
            pl.BlockSpec((None, None, n_keys, HEAD_SLOT), lambda b, h, i: (b, h, 0, 0)),
            pl.BlockSpec((None, None, n_keys, V_HEAD_DIM), lambda b, h, i: (b, h, 0, 0)),
        ],
        out_specs=pl.BlockSpec((None, tq, V_HEAD_DIM), lambda b, h, i: (b, i, h)),
        out_shape=jax.ShapeDtypeStruct((bsz, n_lat, heads * V_HEAD_DIM), BF16),
        compiler_params=_cparams("arbitrary", "arbitrary", "arbitrary"),
        name="mla_attention",
    )(q, k, v)


def _dft_tables(n):
    k = jnp.arange(n, dtype=jnp.int32)
    ang = ((k[:, None] * k[None, :]) % n).astype(F32) * (2.0 * math.pi / n)
    return jnp.cos(ang), jnp.sin(ang)


def _rope_tables(n_ctx, n_lat):
    rows = n_lat // GRID_W
    pairs = QK_ROPE_DIM // 4
    row = jnp.repeat(jnp.arange(rows, dtype=F32), GRID_W)
    col = jnp.tile(jnp.arange(GRID_W, dtype=F32), rows)
    inv = ROPE_THETA ** (-jnp.arange(pairs, dtype=F32) / pairs)
    ang = jnp.stack([row[:, None] * inv, col[:, None] * inv], axis=1)
    cos, sin = jnp.cos(ang), jnp.sin(ang)
    zero = jnp.zeros_like(sin)
    cos_t = jnp.stack([cos, cos], axis=2).reshape(n_lat, QK_ROPE_DIM)
    slo_t = jnp.stack([-sin, zero], axis=2).reshape(n_lat, QK_ROPE_DIM)
    shi_t = jnp.stack([zero, sin], axis=2).reshape(n_lat, QK_ROPE_DIM)

    def full(tab, ctx_val):
        tab = jnp.pad(tab, ((0, 0), (0, LANES - QK_ROPE_DIM)))
        return jnp.concatenate([jnp.full((n_ctx, LANES), ctx_val, F32), tab], axis=0)

    return full(cos_t, 1.0), full(slo_t, 0.0), full(shi_t, 0.0)


def _pad_cols(w, n):
    return jnp.pad(w, ((0, 0), (0, n - w.shape[1])))


def _mla_weights(w_in, g_q_lora, w_q_up, g_kv_lora, w_kv_up, g_q_head, g_k_head):
    qp = -(-Q_LORA_RANK // LANES) * LANES
    kvp = -(-KV_LORA_RANK // LANES) * LANES
    qr, kvr = Q_LORA_RANK, KV_LORA_RANK
    win_p = jnp.concatenate([_pad_cols(w_in[:, :qr], qp), _pad_cols(w_in[:, qr:qr + kvr], kvp),
                             _pad_cols(w_in[:, qr + kvr:], LANES)], axis=1).astype(BF16)
    gq_p = _pad_cols(g_q_lora[None, :], qp)
    gkv_p = _pad_cols(g_kv_lora[None, :], kvp)
    hd = QK_NOPE_DIM + QK_ROPE_DIM
    wq = w_q_up.reshape(qr, MLA_HEADS, hd)
    wq = jnp.pad(wq, ((0, qp - qr), (0, 0), (0, HEAD_SLOT - hd)))
    wq_p = wq.reshape(qp, MLA_HEADS * HEAD_SLOT).astype(BF16)
    wkv_p = jnp.pad(w_kv_up, ((0, kvp - kvr), (0, 0))).astype(BF16)
    gq_slot = _pad_cols(g_q_head[None, :], HEAD_SLOT)
    gk_slot = _pad_cols(g_k_head[None, :], HEAD_SLOT)
    return win_p, gq_p, gkv_p, wq_p, wkv_p, gq_slot, gk_slot


def kernel(x, c, ctx, c_ctx, w_mod, b_mod, g_mix, g_ffn, fourier_w_out, fourier_b_out, mla_w_in, mla_g_q_lora, mla_w_q_up, mla_g_kv_lora, mla_w_kv_up, mla_g_q_head, mla_g_k_head, mla_w_out, router_w, router_b, expert_w_up, expert_b_up, expert_w_down, expert_b_down):
    bsz, n, d = x.shape
    l = ctx.shape[1]
    assert bsz < 8 and n % GRID_W == 0
    tl = min(256, l)
    assert l % tl == 0 and n % tl == 0
    ctx_row = bsz

    c8 = jnp.zeros((8, d), F32).at[:bsz].set(c).at[ctx_row].set(c_ctx)
    modv = _mod_vectors(c8, w_mod, b_mod).reshape(w_mod.shape[0], 8, 1, 6 * d)

    def lat_row(tm):
        return lambda i: i // (n // tm)

    def all_row(tm):
        nct = bsz * l // tm
        return lambda i: jnp.where(i < nct, ctx_row, (i - nct) // (n // tm))

    def ctx_only_row(i):
        return ctx_row

    dg = d // FOURIER_GROUPS
    cc, sc_ = _dft_tables(dg)
    cs = jnp.concatenate([cc, sc_], axis=1).astype(BF16)
    wf = fourier_w_out[0].astype(BF16)
    bf = fourier_b_out[0][None, :]
    g0 = g_mix[0][None, :]

    def fourier(x3, row_fn_of, tm):
        bb, nn, _ = x3.shape
        x2 = x3.reshape(bb * nn, d)
        a, b = _channel_dft(x2, g0, modv, 0, row_fn_of, cs, tm)
        cn, sn = _dft_tables(nn)
        y = _seq_dft(a.reshape(bb, nn, d), b.reshape(bb, nn, d), cn.astype(BF16),
                     (-sn).astype(BF16), 1.0 / math.sqrt(nn * dg))
        return _mm_residual(y.reshape(bb * nn, d), wf, bf, x2, modv, 0, 2, row_fn_of, tm)

    tm_lat = min(512, n)
    x_lat1 = fourier(x, lat_row(tm_lat), tm_lat)
    x_ctx1 = fourier(ctx, ctx_only_row, tl)
    x_all = jnp.concatenate([x_ctx1, x_lat1], axis=0)
    n_ctx_tok = bsz * l

    f0, route0, cnt0 = _ffn_route(x_all, g_ffn[0][None, :], modv, 0, all_row(tl), router_w[0],
                                  router_b[0][None, :], tl)
    ys0, dest0 = _moe_experts(f0, route0, cnt0, expert_w_up, expert_b_up, expert_w_down,
                              expert_b_down, 0)
    tc = min(128, tl)
    x_all2 = _moe_combine(ys0, dest0, route0, x_all, modv, 0, all_row(tc), tc)

    win_p, gq_p, gkv_p, wq_p, wkv_p, gq_slot, gk_slot = _mla_weights(
        mla_w_in[0], mla_g_q_lora[0], mla_w_q_up[0], mla_g_kv_lora[0], mla_w_kv_up[0],
        mla_g_q_head[0], mla_g_k_head[0])
    q_raw, kv_raw, kpe = _mla_proj(x_all2, g_mix[1][None, :], modv, 1, all_row(tl),
                                   win_p, gq_p, gkv_p, wq_p, wkv_p, tl)
    cos, slo, shi = _rope_tables(l, n)
    qh, kh, vh = _head_norm_rope(q_raw, kv_raw, kpe, gq_slot, gk_slot, cos, slo, shi,
                                 bsz, l, n, tl)
    attn = _attention(qh, kh, vh, l, min(512, n))
    x_lat2 = x_all2[n_ctx_tok:]
    zero_bias = jnp.zeros((1, d), F32)
    x_lat3 = _mm_residual(attn.reshape(bsz * n, -1), mla_w_out[0].astype(BF16), zero_bias,
                          x_lat2, modv, 1, 2, lat_row(tm_lat), tm_lat)

    f1, route1, cnt1 = _ffn_route(x_lat3, g_ffn[1][None, :], modv, 1, lat_row(tl), router_w[1],
                                  router_b[1][None, :], tl)
    ys1, dest1 = _moe_experts(f1, route1, cnt1, expert_w_up, expert_b_up, expert_w_down,
                              expert_b_down, 1)
    out = _moe_combine(ys1, dest1, route1, x_lat3, modv, 1, lat_row(tc), tc)
    return out.reshape(bsz, n, d)
```

```python
import functools
import math

import jax
import jax.numpy as jnp
from jax import lax
from jax.experimental import pallas as pl
from jax.experimental.pallas import tpu as pltpu

GRID_W = 64
FOURIER_GROUPS = 4
MLA_HEADS = 16
Q_LORA_RANK = 448
KV_LORA_RANK = 512
QK_NOPE_DIM = 128
QK_ROPE_DIM = 64
V_HEAD_DIM = 128
ROPE_THETA = 10000.0
TOP_K = 4
SWIGLU_ALPHA = 1.702
SWIGLU_LIMIT = 7.0
NORM_EPS = 1e-6

LANES = 128
MOE_ROWS = 256
HEAD_SLOT = 256
VMEM_LIMIT = 56 << 20

F32 = jnp.float32
BF16 = jnp.bfloat16
HIGHEST = lax.Precision.HIGHEST


def _cparams(*sem, vmem=VMEM_LIMIT):
    return pltpu.CompilerParams(dimension_semantics=sem, vmem_limit_bytes=vmem)


def _norm_mod(x, g, sh, sc):
    ms = jnp.mean(x * x, axis=-1, keepdims=True)
    return (x * lax.rsqrt(ms + NORM_EPS)) * g * (1.0 + sc) + sh


def _mod_spec(d, piece, row_fn, layer):
    return pl.BlockSpec((None, None, 1, d), lambda *ids: (layer, row_fn(*ids), 0, piece))


def _mod_kernel(c_ref, w_ref, b_ref, o_ref):
    c = c_ref[...]
    a = c * jax.nn.sigmoid(c)
    o_ref[...] = jnp.dot(a, w_ref[...], preferred_element_type=F32, precision=HIGHEST) + b_ref[...]


def _mod_vectors(c8, w_mod, b_mod):
    depth, d, n6 = w_mod.shape
    tn = next(t for t in (1024, 512, 256, 128) if n6 % t == 0)
    return pl.pallas_call(
        _mod_kernel,
        grid=(depth, n6 // tn),
        in_specs=[
            pl.BlockSpec((8, d), lambda l, j: (0, 0)),
            pl.BlockSpec((None, d, tn), lambda l, j: (l, 0, j)),
            pl.BlockSpec((None, 1, tn), lambda l, j: (l, 0, j)),
        ],
        out_specs=pl.BlockSpec((None, 8, tn), lambda l, j: (l, 0, j)),
        out_shape=jax.ShapeDtypeStruct((depth, 8, n6), F32),
        compiler_params=_cparams("arbitrary", "arbitrary"),
        name="mod_vectors",
    )(c8, w_mod, b_mod.reshape(depth, 1, n6))


ROW_CHUNK = 8


def _fft1_kernel(x_ref, g_ref, sh_ref, sc_ref, m_ref, tr_ref, ti_ref):
    n1 = x_ref.shape[0]
    g, sh, sc = g_ref[...], sh_ref[...], sc_ref[...]
    for c in range(x_ref.shape[1]):
        h = _norm_mod(x_ref[:, c, :], g, sh, sc).astype(BF16)
        t = jnp.dot(m_ref[c], h, preferred_element_type=F32)
        tr_ref[:, c, :] = t[:n1]
        ti_ref[:, c, :] = t[n1:]


def _fft_stage1(x3, g, modv, layer, row_fn, m_tab, n1, n2):
    bsz, n, d = x3.shape
    x4 = x3.reshape(bsz, n1, n2, d)
    blk = pl.BlockSpec((None, n1, ROW_CHUNK, d), lambda b, j: (b, 0, j, 0))
    return pl.pallas_call(
        _fft1_kernel,
        grid=(bsz, n2 // ROW_CHUNK),
        in_specs=[
            blk,
            pl.BlockSpec((1, d), lambda b, j: (0, 0)),
            _mod_spec(d, 0, row_fn, layer),
            _mod_spec(d, 1, row_fn, layer),
            pl.BlockSpec((ROW_CHUNK, 2 * n1, n1), lambda b, j: (j, 0, 0)),
        ],
        out_specs=[blk, blk],
        out_shape=[jax.ShapeDtypeStruct((bsz, n1, n2, d), F32)] * 2,
        compiler_params=_cparams("arbitrary", "arbitrary"),
        name="fourier_stage1",
    )(x4, g, modv, modv, m_tab)


def _fft2_kernel(tr_ref, ti_ref, w2_ref, cs_ref, y_ref, *, groups, scale):
    ck, n2, d = tr_ref.shape
    dg = d // groups
    xr, xi = [], []
    for kk in range(ck):
        t = jnp.concatenate([tr_ref[kk], ti_ref[kk]], axis=0).astype(BF16)
        xx = jnp.dot(w2_ref[...], t, preferred_element_type=F32)
        xr.append(xx[:n2])
        xi.append(xx[n2:])
    xr = jnp.concatenate(xr, axis=0).astype(BF16)
    xi = jnp.concatenate(xi, axis=0).astype(BF16)
    for gi in range(groups):
        cols = slice(gi * dg, (gi + 1) * dg)
        y = (jnp.dot(xr[:, cols], cs_ref[0], preferred_element_type=F32)
             + jnp.dot(xi[:, cols], cs_ref[1], preferred_element_type=F32))
        y_ref[:, cols] = (y * scale).astype(y_ref.dtype)


def _fft_stage2(tr, ti, w2, cs, scale):
    bsz, n1, n2, d = tr.shape
    ck = ROW_CHUNK
    blk = pl.BlockSpec((None, ck, n2, d), lambda b, j: (b, j, 0, 0))
    return pl.pallas_call(
        functools.partial(_fft2_kernel, groups=FOURIER_GROUPS, scale=scale),
        grid=(bsz, n1 // ck),
        in_specs=[blk, blk,
                  pl.BlockSpec(w2.shape, lambda b, j: (0, 0)),
                  pl.BlockSpec(cs.shape, lambda b, j: (0, 0, 0))],
        out_specs=pl.BlockSpec((None, ck * n2, d), lambda b, j: (b, j, 0)),
        out_shape=jax.ShapeDtypeStruct((bsz, n1 * n2, d), BF16),
        compiler_params=_cparams("arbitrary", "arbitrary"),
        name="fourier_stage2",
    )(tr, ti, w2, cs)


def _fft_out_kernel(y_ref, w_ref, b_ref, x_ref, g_ref, o_ref):
    n2, ck, _ = x_ref.shape
    out = jnp.dot(y_ref[...], w_ref[...], preferred_element_type=F32) + b_ref[...]
    gate = g_ref[...]
    for kk in range(ck):
        o_ref[:, kk, :] = x_ref[:, kk, :] + gate * out[kk * n2:(kk + 1) * n2]


def _fft_out(y, w_bf, bias, x3, modv, layer, row_fn, n1, n2):
    bsz, n, d = x3.shape
    ck = ROW_CHUNK
    x4 = x3.reshape(bsz, n2, n1, d)
    blk = pl.BlockSpec((None, n2, ck, d), lambda b, j: (b, 0, j, 0))
    out = pl.pallas_call(
        _fft_out_kernel,
        grid=(bsz, n1 // ck),
        in_specs=[
            pl.BlockSpec((None, ck * n2, d), lambda b, j: (b, j, 0)),
            pl.BlockSpec(w_bf.shape, lambda b, j: (0, 0)),
            pl.BlockSpec((1, d), lambda b, j: (0, 0)),
            blk,
            _mod_spec(d, 2, row_fn, layer),
        ],
        out_specs=blk,
        out_shape=jax.ShapeDtypeStruct((bsz, n2, n1, d), F32),
        compiler_params=_cparams("arbitrary", "arbitrary"),
        name="fourier_out_residual",
    )(y, w_bf, bias, x4, modv)
    return out.reshape(bsz * n, d)


def _fourier_tables(n, dg):
    assert n & (n - 1) == 0
    n2 = 1 << ((n.bit_length() - 1) // 2)
    n1 = n // n2
    assert n1 % ROW_CHUNK == 0 and n2 % ROW_CHUNK == 0
    i32 = jnp.int32
    k1 = jnp.arange(n1, dtype=i32)[None, :, None]
    pos = n2 * jnp.arange(n1, dtype=i32)[None, None, :] + jnp.arange(n2, dtype=i32)[:, None, None]
    ang1 = ((k1 * pos) % n).astype(F32) * (2.0 * math.pi / n)
    m_tab = jnp.concatenate([jnp.cos(ang1), -jnp.sin(ang1)], axis=1).astype(BF16)
    c2, s2 = _dft_tables(n2)
    w2 = jnp.concatenate([jnp.concatenate([c2, s2], axis=1),
                          jnp.concatenate([-s2, c2], axis=1)], axis=0).astype(BF16)
    cc, sc = _dft_tables(dg)
    cs = jnp.stack([cc, sc]).astype(BF16)
    return n1, n2, m_tab, w2, cs


def _mm_res_kernel(y_ref, w_ref, b_ref, r_ref, g_ref, o_ref):
    acc = jnp.dot(y_ref[...], w_ref[...], preferred_element_type=F32)
    o_ref[...] = r_ref[...] + g_ref[...] * (acc + b_ref[...])


def _mm_residual(y2d, w_bf, bias, res2d, modv, layer, piece, row_fn, tm):
    t, k = y2d.shape
    d = w_bf.shape[1]
    tn = min(1024, d)
    nj = d // tn
    return pl.pallas_call(
        _mm_res_kernel,
        grid=(nj, t // tm),
        in_specs=[
            pl.BlockSpec((tm, k), lambda j, i: (i, 0)),
            pl.BlockSpec((k, tn), lambda j, i: (0, j)),
            pl.BlockSpec((1, tn), lambda j, i: (0, j)),
            pl.BlockSpec((tm, tn), lambda j, i: (i, j)),
            pl.BlockSpec((None, None, 1, tn), lambda j, i: (layer, row_fn(i), 0, piece * nj + j)),
        ],
        out_specs=pl.BlockSpec((tm, tn), lambda j, i: (i, j)),
        out_shape=jax.ShapeDtypeStruct((t, d), F32),
        compiler_params=_cparams("arbitrary", "arbitrary"),
        name="mixer_out_residual",
    )(y2d, w_bf, bias, res2d, modv)


def _route_kernel(x_ref, g_ref, sh_ref, sc_ref, wr_ref, br_ref, f_ref, r_ref, cnt_ref):
    @pl.when(pl.program_id(0) == 0)
    def _():
        cnt_ref[...] = jnp.zeros_like(cnt_ref)

    f = _norm_mod(x_ref[...], g_ref[...], sh_ref[...], sc_ref[...])
    f_ref[...] = f
    logits = jnp.dot(f, wr_ref[...], preferred_element_type=F32, precision=HIGHEST) + br_ref[...]
    tm, ne = logits.shape
    col = lax.broadcasted_iota(jnp.int32, (tm, ne), 1).astype(F32)
    lane = lax.broadcasted_iota(jnp.int32, (tm, LANES), 1)
    out = jnp.zeros((tm, LANES), F32)
    vals, idxs = [], []
    hot = jnp.zeros((tm, ne), F32)
    for k in range(TOP_K):
        m = jnp.max(logits, axis=-1, keepdims=True)
        idx = jnp.min(jnp.where(logits == m, col, float(ne)), axis=-1, keepdims=True)
        logits = jnp.where(col == idx, -jnp.inf, logits)
        hot = jnp.where(col == idx, 1.0, hot)
        out = jnp.where(lane == k, idx, out)
        vals.append(m)
        idxs.append(idx)
    es = [jnp.exp(v - vals[0]) for v in vals]
    den = es[0]
    for e in es[1:]:
        den = den + e
    for k in range(TOP_K):
        out = jnp.where(lane == TOP_K + k, es[k] / den, out)
    earlier = (lax.broadcasted_iota(jnp.int32, (tm, tm), 1)
               < lax.broadcasted_iota(jnp.int32, (tm, tm), 0))
    before = jnp.dot(jnp.where(earlier, 1.0, 0.0).astype(BF16), hot.astype(BF16),
                     preferred_element_type=F32) + cnt_ref[:, :ne]
    for k in range(TOP_K):
        rank = jnp.sum(jnp.where(col == idxs[k], before, 0.0), axis=-1, keepdims=True)
        out = jnp.where(lane == 2 * TOP_K + k, rank, out)
    cnt_ref[:, :ne] = cnt_ref[:, :ne] + jnp.sum(hot, axis=0, keepdims=True)
    r_ref[...] = out


def _ffn_route(x2d, g, modv, layer, row_fn, w_router, b_router, tm):
    t, d = x2d.shape
    ne = w_router.shape[1]
    spec = pl.BlockSpec((tm, d), lambda i: (i, 0))
    return pl.pallas_call(
        _route_kernel,
        grid=(t // tm,),
        in_specs=[
            spec,
            pl.BlockSpec((1, d), lambda i: (0, 0)),
            _mod_spec(d, 3, row_fn, layer),
            _mod_spec(d, 4, row_fn, layer),
            pl.BlockSpec((d, ne), lambda i: (0, 0)),
            pl.BlockSpec((1, ne), lambda i: (0, 0)),
        ],
        out_specs=[spec, pl.BlockSpec((tm, LANES), lambda i: (i, 0)),
                   pl.BlockSpec((1, LANES), lambda i: (0, 0))],
        out_shape=[jax.ShapeDtypeStruct((t, d), F32), jax.ShapeDtypeStruct((t, LANES), F32),
                   jax.ShapeDtypeStruct((1, LANES), F32)],
        compiler_params=_cparams("arbitrary"),
        name="ffn_norm_route",
    )(x2d, g, modv, modv, w_router, b_router)


def _gather_rows(tok_ref, f_hbm, xbuf, sem, slot, lo=0, hi=None):
    for r in range(lo, MOE_ROWS if hi is None else hi):
        pltpu.make_async_copy(f_hbm.at[pl.ds(tok_ref[0, 0, r], 1)],
                              xbuf.at[slot, pl.ds(r, 1)], sem.at[slot]).start(priority=r % 2)


def _wait_rows(xbuf, sem, slot):
    pltpu.make_async_copy(xbuf.at[slot], xbuf.at[slot], sem.at[slot]).wait()


def _new_expert(be_ref, b):
    return (b == 0) | (be_ref[b] != be_ref[jnp.maximum(b - 1, 0)])


def _moe_up_kernel(be_ref, nu_ref, *refs, gather):
    if gather:
        (tok0_ref, tokn_ref, f_hbm, w_ref, bg_ref, bl_ref, perm_ref,
         o_ref, xs_ref, xbuf, wg_ref, wl_ref, sem) = refs
    else:
        x_ref, w_ref, bg_ref, bl_ref, perm_ref, o_ref, wg_ref, wl_ref = refs
    b = pl.program_id(1)
    slot = b % 2

    if gather:
        @pl.when(b == 0)
        def _():
            _gather_rows(tok0_ref, f_hbm, xbuf, sem, 0)

        _wait_rows(xbuf, sem, slot)
        xs_ref[...] = xbuf[slot].astype(BF16)
        x_ref = xs_ref

    def prefetch_rows(part, parts):
        if gather:
            _gather_rows(tokn_ref, f_hbm, xbuf, sem, 1 - slot,
                         part * MOE_ROWS // parts, (part + 1) * MOE_ROWS // parts)

    @pl.when(_new_expert(be_ref, b))
    def _():
        for q in range(w_ref.shape[1] // (2 * LANES)):
            wq = w_ref[:, q * 2 * LANES:(q + 1) * 2 * LANES].astype(BF16)
            r = jnp.dot(wq, perm_ref[...], preferred_element_type=F32)
            wg_ref[:, q * LANES:(q + 1) * LANES] = r[:, :LANES].astype(BF16)
            wl_ref[:, q * LANES:(q + 1) * LANES] = r[:, LANES:].astype(BF16)

    @pl.when(b < nu_ref[0])
    def _():
        x = x_ref[...]
        tn = o_ref.shape[1]
        cw = min(2 * LANES, tn)
        for c in range(tn // cw):
            prefetch_rows(c, tn // cw)
            cols = slice(c * cw, (c + 1) * cw)
            glu = jnp.dot(x, wg_ref[:, cols], preferred_element_type=F32) + bg_ref[:, cols]
            lin = jnp.dot(x, wl_ref[:, cols], preferred_element_type=F32) + bl_ref[:, cols]
            glu = jnp.minimum(glu, SWIGLU_LIMIT)
            lin = jnp.clip(lin, -SWIGLU_LIMIT, SWIGLU_LIMIT)
            o_ref[:, cols] = (glu * jax.nn.sigmoid(SWIGLU_ALPHA * glu)
                              * (lin + 1.0)).astype(o_ref.dtype)

    @pl.when(b >= nu_ref[0])
    def _():
        prefetch_rows(0, 1)
        o_ref[...] = jnp.zeros_like(o_ref)

    if gather:
        @pl.when(b == pl.num_programs(1) - 1)
        def _():
            _wait_rows(xbuf, sem, 1 - slot)


def _moe_up(f2d, slot_tok, block_expert, n_used, w_up, b_glu, b_lin, perm, layer):
    d = f2d.shape[1]
    nb = slot_tok.shape[0] // MOE_ROWS
    s = nb * MOE_ROWS
    de = w_up.shape[3] // 2
    tn = min(1024, de)
    nj = de // tn
    tok3 = slot_tok.reshape(nb, 1, MOE_ROWS)
    tok_spec = lambda fn: pl.BlockSpec((1, 1, MOE_ROWS), fn, memory_space=pltpu.SMEM)

    def common_specs(j0):
        bspec = pl.BlockSpec((None, None, 1, tn), lambda j, b, be, nu: (layer, be[b], 0, j + j0))
        return [pl.BlockSpec((None, None, d, 2 * tn),
                             lambda j, b, be, nu: (layer, be[b], 0, j + j0)),
                bspec, bspec, pl.BlockSpec(perm.shape, lambda j, b, be, nu: (0, 0))]

    row_blk = lambda j, b, be, nu: (b, 0)
    w_scratch = [pltpu.VMEM((d, tn), BF16), pltpu.VMEM((d, tn), BF16)]
    act0, xs = pl.pallas_call(
        functools.partial(_moe_up_kernel, gather=True),
        grid_spec=pltpu.PrefetchScalarGridSpec(
            num_scalar_prefetch=2,
            grid=(1, nb),
            in_specs=[tok_spec(lambda j, b, be, nu: (0, 0, 0)),
                      tok_spec(lambda j, b, be, nu: ((b + 1) % nb, 0, 0)),
                      pl.BlockSpec(memory_space=pl.ANY)] + common_specs(0),
            out_specs=[pl.BlockSpec((MOE_ROWS, tn), row_blk), pl.BlockSpec((MOE_ROWS, d), row_blk)],
            scratch_shapes=[pltpu.VMEM((2, MOE_ROWS, d), F32)] + w_scratch
            + [pltpu.SemaphoreType.DMA((2,))],
        ),
        out_shape=[jax.ShapeDtypeStruct((s, tn), BF16), jax.ShapeDtypeStruct((s, d), BF16)],
        compiler_params=_cparams("arbitrary", "arbitrary"),
        name="moe_up_gather",
    )(block_expert, n_used, tok3, tok3, f2d, w_up, b_glu, b_lin, perm)
    if nj == 1:
        return [act0]
    rest = pl.pallas_call(
        functools.partial(_moe_up_kernel, gather=False),
        grid_spec=pltpu.PrefetchScalarGridSpec(
            num_scalar_prefetch=2,
            grid=(nj - 1, nb),
            in_specs=[pl.BlockSpec((MOE_ROWS, d), row_blk)] + common_specs(1),
            out_specs=pl.BlockSpec((MOE_ROWS, tn), lambda j, b, be, nu: (b, j)),
            scratch_shapes=w_scratch,
        ),
        out_shape=jax.ShapeDtypeStruct((s, (nj - 1) * tn), BF16),
        compiler_params=_cparams("arbitrary", "arbitrary"),
        name="moe_up_dense",
    )(block_expert, n_used, xs, w_up, b_glu, b_lin, perm)
    return [act0, rest]


def _moe_down_kernel(be_ref, nu_ref, *refs):
    *a_refs, w_ref, b_ref, o_ref, wbf_ref = refs
    b = pl.program_id(1)

    @pl.when(_new_expert(be_ref, b))
    def _():
        wbf_ref[...] = w_ref[...].astype(BF16)

    @pl.when(b < nu_ref[0])
    def _():
        acc = b_ref[...]
        k0 = 0
        for a_ref in a_refs:
            kw = a_ref.shape[1]
            acc = acc + jnp.dot(a_ref[...], wbf_ref[k0:k0 + kw, :], preferred_element_type=F32)
            k0 += kw
        o_ref[...] = acc

    @pl.when(b >= nu_ref[0])
    def _():
        o_ref[...] = jnp.zeros_like(o_ref)


def _moe_down(acts, block_expert, n_used, w_down, b_down, layer):
    s = acts[0].shape[0]
    de, d = w_down.shape[2:]
    nb = s // MOE_ROWS
    tn = min(2048, d)
    return pl.pallas_call(
        _moe_down_kernel,
        grid_spec=pltpu.PrefetchScalarGridSpec(
            num_scalar_prefetch=2,
            grid=(d // tn, nb),
            in_specs=[pl.BlockSpec((MOE_ROWS, a.shape[1]), lambda j, b, be, nu: (b, 0))
                      for a in acts]
            + [pl.BlockSpec((None, None, de, tn), lambda j, b, be, nu: (layer, be[b], 0, j)),
               pl.BlockSpec((None, None, 1, tn), lambda j, b, be, nu: (layer, be[b], 0, j))],
            out_specs=pl.BlockSpec((MOE_ROWS, tn), lambda j, b, be, nu: (b, j)),
            scratch_shapes=[pltpu.VMEM((de, tn), BF16)],
        ),
        out_shape=jax.ShapeDtypeStruct((s, d), F32),
        compiler_params=_cparams("arbitrary", "arbitrary"),
        name="moe_down",
    )(block_expert, n_used, *acts, w_down, b_down)


def _combine_kernel(dest0_ref, destn_ref, ys_hbm, r_ref, x_ref, g_ref, o_ref, buf, sem, *, tm):
    i = pl.program_id(0)
    slot = i % 2

    def fetch(dest_ref, slot):
        for r in range(tm):
            for k in range(TOP_K):
                pltpu.make_async_copy(ys_hbm.at[pl.ds(dest_ref[0, 0, r * TOP_K + k], 1)],
                                      buf.at[slot, k, pl.ds(r, 1)], sem.at[slot]).start()

    def drain(slot):
        pltpu.make_async_copy(buf.at[slot], buf.at[slot], sem.at[slot]).wait()

    @pl.when(i == 0)
    def _():
        fetch(dest0_ref, 0)

    drain(slot)
    fetch(destn_ref, 1 - slot)
    route = r_ref[...]
    acc = route[:, TOP_K:TOP_K + 1] * buf[slot, 0]
    for k in range(1, TOP_K):
        acc = acc + route[:, TOP_K + k:TOP_K + k + 1] * buf[slot, k]
    o_ref[...] = x_ref[...] + g_ref[...] * acc

    @pl.when(i == pl.num_programs(0) - 1)
    def _():
        drain(1 - slot)


def _moe_combine(ys, dest, route, x2d, modv, layer, row_fn, tm):
    t, d = x2d.shape
    nt = t // tm
    dest3 = dest.reshape(nt, 1, tm * TOP_K)
    dspec = lambda fn: pl.BlockSpec((1, 1, tm * TOP_K), fn, memory_space=pltpu.SMEM)
    return pl.pallas_call(
        functools.partial(_combine_kernel, tm=tm),
        grid=(nt,),
        in_specs=[
            dspec(lambda i: (0, 0, 0)),
            dspec(lambda i: ((i + 1) % nt, 0, 0)),
            pl.BlockSpec(memory_space=pl.ANY),
            pl.BlockSpec((tm, LANES), lambda i: (i, 0)),
            pl.BlockSpec((tm, d), lambda i: (i, 0)),
            _mod_spec(d, 5, row_fn, layer),
        ],
        out_specs=pl.BlockSpec((tm, d), lambda i: (i, 0)),
        out_shape=jax.ShapeDtypeStruct((t, d), F32),
        scratch_shapes=[pltpu.VMEM((2, TOP_K, tm, d), F32), pltpu.SemaphoreType.DMA((2,))],
        compiler_params=_cparams("arbitrary"),
        name="moe_combine",
    )(dest3, dest3, ys, route, x2d, modv)


def _routing_tables(route, counts_f, ne):
    t = route.shape[0]
    i32 = jnp.int32
    flat_e = route[:, :TOP_K].astype(i32).reshape(-1)
    rank = route[:, 2 * TOP_K:3 * TOP_K].astype(i32).reshape(-1)
    counts = counts_f[0, :ne].astype(i32)
    n_rows = t * TOP_K
    padded = (counts + MOE_ROWS - 1) // MOE_ROWS * MOE_ROWS
    pad_end = jnp.cumsum(padded)
    pad_start = pad_end - padded
    dest = pad_start[flat_e] + rank
    n_blocks = -(-n_rows // MOE_ROWS) + ne
    blk_row0 = jnp.arange(n_blocks, dtype=i32) * MOE_ROWS
    block_expert = jnp.minimum(
        jnp.sum((pad_end[None, :] <= blk_row0[:, None]).astype(i32), axis=1), ne - 1)
    n_used = (pad_end[-1:] // MOE_ROWS).astype(i32)
    shift = max(n_rows - 1, 1).bit_length()
    assert ne << shift < 2 ** 31
    order = jnp.sort(flat_e * (1 << shift) + jnp.arange(n_rows, dtype=i32)) & ((1 << shift) - 1)
    first = jnp.cumsum(counts) - counts
    local = (blk_row0 - pad_start[block_expert])[:, None] + jnp.arange(MOE_ROWS, dtype=i32)[None, :]
    src = jnp.clip(first[block_expert][:, None] + local, 0, n_rows - 1)
    slot_tok = jnp.where(local < counts[block_expert][:, None], order[src] // TOP_K, 0)
    return dest.astype(i32), slot_tok.reshape(-1).astype(i32), block_expert.astype(i32), n_used


def _moe_experts(f2d, route, counts_f, w_up, b_up, w_down, b_down, layer):
    depth, ne = w_up.shape[:2]
    dest, slot_tok, block_expert, n_used = _routing_tables(route, counts_f, ne)
    b_pairs = b_up.reshape(depth, ne, 1, -1, 2)
    col = jnp.arange(2 * LANES)
    src = jnp.where(col < LANES, 2 * col, 2 * (col - LANES) + 1)
    perm = (jnp.arange(2 * LANES)[:, None] == src[None, :]).astype(BF16)
    acts = _moe_up(f2d, slot_tok, block_expert, n_used, w_up, b_pairs[..., 0], b_pairs[..., 1],
                   perm, layer)
    ys = _moe_down(acts, block_expert, n_used, w_down, b_down.reshape(depth, ne, 1, -1), layer)
    return ys, dest


def _mla_proj_kernel(x_ref, g_ref, sh_ref, sc_ref, win_ref, gq_ref, gkv_ref, wq_ref, wkv_ref,
                     q_ref, kv_ref, kpe_ref, *, qp, kvp):
    h = _norm_mod(x_ref[...], g_ref[...], sh_ref[...], sc_ref[...]).astype(BF16)
    a = jnp.dot(h, win_ref[...], preferred_element_type=F32)
    cq = a[:, :qp]
    ckv = a[:, qp:qp + kvp]
    kpe_ref[...] = a[:, qp + kvp:]
    cqn = cq * lax.rsqrt(jnp.sum(cq * cq, axis=-1, keepdims=True) * (1.0 / Q_LORA_RANK) + NORM_EPS)
    ckvn = ckv * lax.rsqrt(jnp.sum(ckv * ckv, axis=-1, keepdims=True) * (1.0 / KV_LORA_RANK)
                           + NORM_EPS)
    q_ref[...] = jnp.dot((cqn * gq_ref[...]).astype(BF16), wq_ref[...], preferred_element_type=F32)
    kv_ref[...] = jnp.dot((ckvn * gkv_ref[...]).astype(BF16), wkv_ref[...],
                          preferred_element_type=F32)


def _mla_proj(x2d, g, modv, layer, row_fn, win_p, gq_p, gkv_p, wq_p, wkv_p, tm):
    t, d = x2d.shape
    qp, kvp = gq_p.shape[1], gkv_p.shape[1]
    na = win_p.shape[1]
    nq, nkv = wq_p.shape[1], wkv_p.shape[1]
    full = lambda arr: pl.BlockSpec(arr.shape, lambda i: (0, 0))
    return pl.pallas_call(
        functools.partial(_mla_proj_kernel, qp=qp, kvp=kvp),
        grid=(t // tm,),
        in_specs=[
            pl.BlockSpec((tm, d), lambda i: (i, 0)),
            pl.BlockSpec((1, d), lambda i: (0, 0)),
            _mod_spec(d, 0, row_fn, layer),
            _mod_spec(d, 1, row_fn, layer),
            full(win_p), full(gq_p), full(gkv_p), full(wq_p), full(wkv_p),
        ],
        out_specs=[pl.BlockSpec((tm, nq), lambda i: (i, 0)),
                   pl.BlockSpec((tm, nkv), lambda i: (i, 0)),
                   pl.BlockSpec((tm, na - qp - kvp), lambda i: (i, 0))],
        out_shape=[jax.ShapeDtypeStruct((t, nq), F32), jax.ShapeDtypeStruct((t, nkv), F32),
                   jax.ShapeDtypeStruct((t, na - qp - kvp), F32)],
        compiler_params=_cparams("arbitrary"),
        name="mla_projections",
    )(x2d, g, modv, modv, win_p, gq_p, gkv_p, wq_p, wkv_p)


def _rope(x, cos, sin_lo, sin_hi):
    quarter = QK_ROPE_DIM // 4
    return (x * cos + pltpu.roll(x, LANES - quarter, 1) * sin_lo
            + pltpu.roll(x, quarter, 1) * sin_hi)


def _headnorm_kernel(q_ref, kv_ref, kpe_ref, gq_ref, gk_ref, cos_ref, slo_ref, shi_ref,
                     qo_ref, ko_ref, vo_ref, *, heads, q_scale):
    inv = 1.0 / (QK_NOPE_DIM + QK_ROPE_DIM)
    cos, slo, shi = cos_ref[...], slo_ref[...], shi_ref[...]
    kpe = kpe_ref[...]
    kpe_ss = jnp.sum(kpe * kpe, axis=-1, keepdims=True)
    gq = gq_ref[...]
    gk = gk_ref[...]
    for h in range(heads):
        qh = q_ref[:, h * HEAD_SLOT:(h + 1) * HEAD_SLOT]
        rs = lax.rsqrt(jnp.sum(qh * qh, axis=-1, keepdims=True) * inv + NORM_EPS) * q_scale
        qn = qh * rs * gq
        qo_ref[h, :, :LANES] = qn[:, :LANES].astype(BF16)
        qo_ref[h, :, LANES:] = _rope(qn[:, LANES:], cos, slo, shi).astype(BF16)
        kn = kv_ref[:, h * HEAD_SLOT:h * HEAD_SLOT + QK_NOPE_DIM]
        rk = lax.rsqrt((jnp.sum(kn * kn, axis=-1, keepdims=True) + kpe_ss) * inv + NORM_EPS)
        ko_ref[h, :, :LANES] = (kn * rk * gk[:, :LANES]).astype(BF16)
        ko_ref[h, :, LANES:] = _rope(kpe * rk * gk[:, LANES:], cos, slo, shi).astype(BF16)
        vo_ref[h] = kv_ref[:, h * HEAD_SLOT + QK_NOPE_DIM:(h + 1) * HEAD_SLOT].astype(BF16)


def _head_norm_rope(q_raw, kv_raw, kpe, gq_slot, gk_slot, cos, slo, shi, bsz, n_ctx, n_lat, tm):
    heads = MLA_HEADS
    n_keys = n_ctx + n_lat
    ctx_tiles = bsz * n_ctx // tm
    per_ctx = n_ctx // tm
    per_lat = n_lat // tm

    def batch_of(i):
        return jnp.where(i < ctx_tiles, i // per_ctx, (i - ctx_tiles) // per_lat)

    def key_blk(i):
        return jnp.where(i < ctx_tiles, i % per_ctx, per_ctx + (i - ctx_tiles) % per_lat)

    tab = pl.BlockSpec((tm, LANES), lambda i: (key_blk(i), 0))
    t = q_raw.shape[0]
    out_map = lambda i: (batch_of(i), 0, key_blk(i), 0)

    def q_map(i):
        blk = jnp.where(i < ctx_tiles, per_lat + i % per_ctx, (i - ctx_tiles) % per_lat)
        return (batch_of(i), 0, blk, 0)

    return pl.pallas_call(
        functools.partial(_headnorm_kernel, heads=heads,
                          q_scale=float(QK_NOPE_DIM + QK_ROPE_DIM) ** -0.5),
        grid=(t // tm,),
        in_specs=[
            pl.BlockSpec((tm, heads * HEAD_SLOT), lambda i: (i, 0)),
            pl.BlockSpec((tm, heads * HEAD_SLOT), lambda i: (i, 0)),
            pl.BlockSpec((tm, LANES), lambda i: (i, 0)),
            pl.BlockSpec((1, HEAD_SLOT), lambda i: (0, 0)),
            pl.BlockSpec((1, HEAD_SLOT), lambda i: (0, 0)),
            tab, tab, tab,
        ],
        out_specs=[pl.BlockSpec((None, heads, tm, HEAD_SLOT), q_map),
                   pl.BlockSpec((None, heads, tm, HEAD_SLOT), out_map),
                   pl.BlockSpec((None, heads, tm, V_HEAD_DIM), out_map)],
        out_shape=[jax.ShapeDtypeStruct((bsz, heads, n_keys, HEAD_SLOT), BF16),
                   jax.ShapeDtypeStruct((bsz, heads, n_keys, HEAD_SLOT), BF16),
                   jax.ShapeDtypeStruct((bsz, heads, n_keys, V_HEAD_DIM), BF16)],
        compiler_params=_cparams("arbitrary"),
        name="mla_headnorm_rope",
    )(q_raw, kv_raw, kpe, gq_slot, gk_slot, cos, slo, shi)


def _attn_kernel(q_ref, k_ref, v_ref, o_ref, *, sub):
    for r in range(q_ref.shape[0] // sub):
        rows = pl.ds(r * sub, sub)
        s = lax.dot_general(q_ref[rows, :], k_ref[...], (((1,), (1,)), ((), ())),
                            preferred_element_type=F32)
        m = jnp.max(s, axis=-1, keepdims=True)
        p = jnp.exp(s - m)
        l = jnp.sum(p, axis=-1, keepdims=True)
        o = jnp.dot(p.astype(BF16), v_ref[...], preferred_element_type=F32)
        o_ref[rows, :] = (o / l).astype(o_ref.dtype)


def _attention(q, k, v, n_ctx, tq):
    bsz, heads, n_keys, _ = k.shape
    n_lat = n_keys - n_ctx
    return pl.pallas_call(
        functools.partial(_attn_kernel, sub=min(256, tq)),
        grid=(bsz, heads, n_lat // tq),
        in_specs=[
            pl.BlockSpec((None, None, tq, HEAD_SLOT), lambda b, h, i: (b, h, i, 0)),
            pl.BlockSpec((None, None, n_keys, HEAD_SLOT), lambda b, h, i: (b, h, 0, 0)),
            pl.BlockSpec((None, None, n_keys, V_HEAD_DIM), lambda b, h, i: (b, h, 0, 0)),
        ],
        out_specs=pl.BlockSpec((None, tq, V_HEAD_DIM), lambda b, h, i: (b, i, h)),
        out_shape=jax.ShapeDtypeStruct((bsz, n_lat, heads * V_HEAD_DIM), BF16),
        compiler_params=_cparams("arbitrary", "arbitrary", "arbitrary"),
        name="mla_attention",
    )(q, k, v)


def _dft_tables(n):
    k = jnp.arange(n, dtype=jnp.int32)
    ang = ((k[:, None] * k[None, :]) % n).astype(F32) * (2.0 * math.pi / n)
    return jnp.cos(ang), jnp.sin(ang)


def _rope_tables(n_ctx, n_lat):
    rows = n_lat // GRID_W
    pairs = QK_ROPE_DIM // 4
    row = jnp.repeat(jnp.arange(rows, dtype=F32), GRID_W)
    col = jnp.tile(jnp.arange(GRID_W, dtype=F32), rows)
    inv = ROPE_THETA ** (-jnp.arange(pairs, dtype=F32) / pairs)
    ang = jnp.stack([row[:, None] * inv, col[:, None] * inv], axis=1)
    cos, sin = jnp.cos(ang), jnp.sin(ang)
    zero = jnp.zeros_like(sin)
    cos_t = jnp.stack([cos, cos], axis=2).reshape(n_lat, QK_ROPE_DIM)
    slo_t = jnp.stack([-sin, zero], axis=2).reshape(n_lat, QK_ROPE_DIM)
    shi_t = jnp.stack([zero, sin], axis=2).reshape(n_lat, QK_ROPE_DIM)

    def full(tab, ctx_val):
        tab = jnp.pad(tab, ((0, 0), (0, LANES - QK_ROPE_DIM)))
        return jnp.concatenate([jnp.full((n_ctx, LANES), ctx_val, F32), tab], axis=0)

    return full(cos_t, 1.0), full(slo_t, 0.0), full(shi_t, 0.0)


def _pad_cols(w, n):
    return jnp.pad(w, ((0, 0), (0, n - w.shape[1])))


def _mla_weights(w_in, g_q_lora, w_q_up, g_kv_lora, w_kv_up, g_q_head, g_k_head):
    qp = -(-Q_LORA_RANK // LANES) * LANES
    kvp = -(-KV_LORA_RANK // LANES) * LANES
    qr, kvr = Q_LORA_RANK, KV_LORA_RANK
    win_p = jnp.concatenate([_pad_cols(w_in[:, :qr], qp), _pad_cols(w_in[:, qr:qr + kvr], kvp),
                             _pad_cols(w_in[:, qr + kvr:], LANES)], axis=1).astype(BF16)
    gq_p = _pad_cols(g_q_lora[None, :], qp)
    gkv_p = _pad_cols(g_kv_lora[None, :], kvp)
    hd = QK_NOPE_DIM + QK_ROPE_DIM
    wq = w_q_up.reshape(qr, MLA_HEADS, hd)
    wq = jnp.pad(wq, ((0, qp - qr), (0, 0), (0, HEAD_SLOT - hd)))
    wq_p = wq.reshape(qp, MLA_HEADS * HEAD_SLOT).astype(BF16)
    wkv_p = jnp.pad(w_kv_up, ((0, kvp - kvr), (0, 0))).astype(BF16)
    gq_slot = _pad_cols(g_q_head[None, :], HEAD_SLOT)
    gk_slot = _pad_cols(g_k_head[None, :], HEAD_SLOT)
    return win_p, gq_p, gkv_p, wq_p, wkv_p, gq_slot, gk_slot


def kernel(x, c, ctx, c_ctx, w_mod, b_mod, g_mix, g_ffn, fourier_w_out, fourier_b_out, mla_w_in, mla_g_q_lora, mla_w_q_up, mla_g_kv_lora, mla_w_kv_up, mla_g_q_head, mla_g_k_head, mla_w_out, router_w, router_b, expert_w_up, expert_b_up, expert_w_down, expert_b_down):
    bsz, n, d = x.shape
    l = ctx.shape[1]
    assert bsz < 8 and n % GRID_W == 0
    tl = min(256, l)
    assert l % tl == 0 and n % tl == 0
    ctx_row = bsz

    c8 = jnp.zeros((8, d), F32).at[:bsz].set(c).at[ctx_row].set(c_ctx)
    modv = _mod_vectors(c8, w_mod, b_mod).reshape(w_mod.shape[0], 8, 1, 6 * d)

    def lat_row(tm):
        return lambda i: i // (n // tm)

    def all_row(tm):
        nct = bsz * l // tm
        return lambda i: jnp.where(i < nct, ctx_row, (i - nct) // (n // tm))

    dg = d // FOURIER_GROUPS
    wf = fourier_w_out[0].astype(BF16)
    bf = fourier_b_out[0][None, :]
    g0 = g_mix[0][None, :]

    def fourier(x3, row_fn):
        nn = x3.shape[1]
        n1, n2, m_tab, w2, cs = _fourier_tables(nn, dg)
        tr, ti = _fft_stage1(x3, g0, modv, 0, row_fn, m_tab, n1, n2)
        y = _fft_stage2(tr, ti, w2, cs, 1.0 / math.sqrt(nn * dg))
        return _fft_out(y, wf, bf, x3, modv, 0, row_fn, n1, n2)

    tm_lat = min(512, n)
    x_lat1 = fourier(x, lambda b, j: b)
    x_ctx1 = fourier(ctx, lambda b, j: ctx_row)
    x_all = jnp.concatenate([x_ctx1, x_lat1], axis=0)
    n_ctx_tok = bsz * l

    f0, route0, cnt0 = _ffn_route(x_all, g_ffn[0][None, :], modv, 0, all_row(tl), router_w[0],
                                  router_b[0][None, :], tl)
    ys0, dest0 = _moe_experts(f0, route0, cnt0, expert_w_up, expert_b_up, expert_w_down,
                              expert_b_down, 0)
    tc = min(128, tl)
    x_all2 = _moe_combine(ys0, dest0, route0, x_all, modv, 0, all_row(tc), tc)

    win_p, gq_p, gkv_p, wq_p, wkv_p, gq_slot, gk_slot = _mla_weights(
        mla_w_in[0], mla_g_q_lora[0], mla_w_q_up[0], mla_g_kv_lora[0], mla_w_kv_up[0],
        mla_g_q_head[0], mla_g_k_head[0])
    q_raw, kv_raw, kpe = _mla_proj(x_all2, g_mix[1][None, :], modv, 1, all_row(tl),
                                   win_p, gq_p, gkv_p, wq_p, wkv_p, tl)
    cos, slo, shi = _rope_tables(l, n)
    qh, kh, vh = _head_norm_rope(q_raw, kv_raw, kpe, gq_slot, gk_slot, cos, slo, shi,
                                 bsz, l, n, tl)
    attn = _attention(qh, kh, vh, l, min(1024, n))
    x_lat2 = x_all2[n_ctx_tok:]
    zero_bias = jnp.zeros((1, d), F32)
    x_lat3 = _mm_residual(attn.reshape(bsz * n, -1), mla_w_out[0].astype(BF16), zero_bias,
                          x_lat2, modv, 1, 2, lat_row(tm_lat), tm_lat)

    f1, route1, cnt1 = _ffn_route(x_lat3, g_ffn[1][None, :], modv, 1, lat_row(tl), router_w[1],
                                  router_b[1][None, :], tl)
    ys1, dest1 = _moe_experts(f1, route1, cnt1, expert_w_up, expert_b_up, expert_w_down,
                              expert_b_down, 1)
    out = _moe_combine(ys1, dest1, route1, x_lat3, modv, 1, lat_row(tc), tc)
    return out.reshape(bsz, n, d)
```

```python
import functools
import math

import jax
import jax.numpy as jnp
from jax import lax
from jax.experimental import pallas as pl
from jax.experimental.pallas import tpu as pltpu

GRID_W = 64
FOURIER_GROUPS = 4
MLA_HEADS = 16
Q_LORA_RANK = 448
KV_LORA_RANK = 512
QK_NOPE_DIM = 128
QK_ROPE_DIM = 64
V_HEAD_DIM = 128
ROPE_THETA = 10000.0
TOP_K = 4
SWIGLU_ALPHA = 1.702
SWIGLU_LIMIT = 7.0
NORM_EPS = 1e-6

LANES = 128
MOE_ROWS = 256
HEAD_SLOT = 256
VMEM_LIMIT = 56 << 20

F32 = jnp.float32
BF16 = jnp.bfloat16
HIGHEST = lax.Precision.HIGHEST


def _cparams(*sem, vmem=VMEM_LIMIT):
    return pltpu.CompilerParams(dimension_semantics=sem, vmem_limit_bytes=vmem)


def _norm_mod(x, g, sh, sc):
    ms = jnp.mean(x * x, axis=-1, keepdims=True)
    return (x * lax.rsqrt(ms + NORM_EPS)) * g * (1.0 + sc) + sh


def _pack_bf16_pairs(x):
    h = x.shape[1] // 2
    lo = pltpu.bitcast(x[:, :h].astype(BF16).astype(F32), jnp.uint32)
    hi = pltpu.bitcast(x[:, h:].astype(BF16).astype(F32), jnp.uint32)
    return hi | (lo >> 16)


def _unpack_bf16_pairs(w):
    lo = pltpu.bitcast(w << 16, F32).astype(BF16)
    hi = pltpu.bitcast(w & jnp.uint32(0xFFFF0000), F32).astype(BF16)
    return jnp.concatenate([lo, hi], axis=1)


def _mod_spec(d, piece, row_fn, layer):
    return pl.BlockSpec((None, None, 1, d), lambda *ids: (layer, row_fn(*ids), 0, piece))


def _mod_kernel(c_ref, w_ref, b_ref, o_ref):
    c = c_ref[...]
    a = c * jax.nn.sigmoid(c)
    o_ref[...] = jnp.dot(a, w_ref[...], preferred_element_type=F32, precision=HIGHEST) + b_ref[...]


def _mod_vectors(c8, w_mod, b_mod):
    depth, d, n6 = w_mod.shape
    tn = next(t for t in (1024, 512, 256, 128) if n6 % t == 0)
    return pl.pallas_call(
        _mod_kernel,
        grid=(depth, n6 // tn),
        in_specs=[
            pl.BlockSpec((8, d), lambda l, j: (0, 0)),
            pl.BlockSpec((None, d, tn), lambda l, j: (l, 0, j)),
            pl.BlockSpec((None, 1, tn), lambda l, j: (l, 0, j)),
        ],
        out_specs=pl.BlockSpec((None, 8, tn), lambda l, j: (l, 0, j)),
        out_shape=jax.ShapeDtypeStruct((depth, 8, n6), F32),
        compiler_params=_cparams("arbitrary", "arbitrary"),
        name="mod_vectors",
    )(c8, w_mod, b_mod.reshape(depth, 1, n6))


ROW_CHUNK = 8


def _fft1_kernel(x_ref, g_ref, sh_ref, sc_ref, m_ref, tr_ref, ti_ref):
    n1 = x_ref.shape[0]
    g, sh, sc = g_ref[...], sh_ref[...], sc_ref[...]
    for c in range(x_ref.shape[1]):
        h = _norm_mod(x_ref[:, c, :], g, sh, sc).astype(BF16)
        t = jnp.dot(m_ref[c], h, preferred_element_type=F32)
        tr_ref[:, c, :] = t[:n1]
        ti_ref[:, c, :] = t[n1:]


def _fft_stage1(x3, g, modv, layer, row_fn, m_tab, n1, n2):
    bsz, n, d = x3.shape
    x4 = x3.reshape(bsz, n1, n2, d)
    blk = pl.BlockSpec((None, n1, ROW_CHUNK, d), lambda b, j: (b, 0, j, 0))
    return pl.pallas_call(
        _fft1_kernel,
        grid=(bsz, n2 // ROW_CHUNK),
        in_specs=[
            blk,
            pl.BlockSpec((1, d), lambda b, j: (0, 0)),
            _mod_spec(d, 0, row_fn, layer),
            _mod_spec(d, 1, row_fn, layer),
            pl.BlockSpec((ROW_CHUNK, 2 * n1, n1), lambda b, j: (j, 0, 0)),
        ],
        out_specs=[blk, blk],
        out_shape=[jax.ShapeDtypeStruct((bsz, n1, n2, d), F32)] * 2,
        compiler_params=_cparams("arbitrary", "arbitrary"),
        name="fourier_stage1",
    )(x4, g, modv, modv, m_tab)


def _fft2_kernel(tr_ref, ti_ref, w2_ref, cs_ref, y_ref, *, groups, scale):
    ck, n2, d = tr_ref.shape
    dg = d // groups
    xr, xi = [], []
    for kk in range(ck):
        t = jnp.concatenate([tr_ref[kk], ti_ref[kk]], axis=0).astype(BF16)
        xx = jnp.dot(w2_ref[...], t, preferred_element_type=F32)
        xr.append(xx[:n2])
        xi.append(xx[n2:])
    xr = jnp.concatenate(xr, axis=0).astype(BF16)
    xi = jnp.concatenate(xi, axis=0).astype(BF16)
    for gi in range(groups):
        cols = slice(gi * dg, (gi + 1) * dg)
        y = (jnp.dot(xr[:, cols], cs_ref[0], preferred_element_type=F32)
             + jnp.dot(xi[:, cols], cs_ref[1], preferred_element_type=F32))
        y_ref[:, cols] = (y * scale).astype(y_ref.dtype)


def _fft_stage2(tr, ti, w2, cs, scale):
    bsz, n1, n2, d = tr.shape
    ck = ROW_CHUNK
    blk = pl.BlockSpec((None, ck, n2, d), lambda b, j: (b, j, 0, 0))
    return pl.pallas_call(
        functools.partial(_fft2_kernel, groups=FOURIER_GROUPS, scale=scale),
        grid=(bsz, n1 // ck),
        in_specs=[blk, blk,
                  pl.BlockSpec(w2.shape, lambda b, j: (0, 0)),
                  pl.BlockSpec(cs.shape, lambda b, j: (0, 0, 0))],
        out_specs=pl.BlockSpec((None, ck * n2, d), lambda b, j: (b, j, 0)),
        out_shape=jax.ShapeDtypeStruct((bsz, n1 * n2, d), BF16),
        compiler_params=_cparams("arbitrary", "arbitrary"),
        name="fourier_stage2",
    )(tr, ti, w2, cs)


def _fft_out_kernel(y_ref, w_ref, b_ref, x_ref, g_ref, o_ref):
    n2, ck, _ = x_ref.shape
    out = jnp.dot(y_ref[...], w_ref[...], preferred_element_type=F32) + b_ref[...]
    gate = g_ref[...]
    for kk in range(ck):
        o_ref[:, kk, :] = x_ref[:, kk, :] + gate * out[kk * n2:(kk + 1) * n2]


def _fft_out(y, w_bf, bias, x3, modv, layer, row_fn, n1, n2):
    bsz, n, d = x3.shape
    ck = ROW_CHUNK
    x4 = x3.reshape(bsz, n2, n1, d)
    blk = pl.BlockSpec((None, n2, ck, d), lambda b, j: (b, 0, j, 0))
    out = pl.pallas_call(
        _fft_out_kernel,
        grid=(bsz, n1 // ck),
        in_specs=[
            pl.BlockSpec((None, ck * n2, d), lambda b, j: (b, j, 0)),
            pl.BlockSpec(w_bf.shape, lambda b, j: (0, 0)),
            pl.BlockSpec((1, d), lambda b, j: (0, 0)),
            blk,
            _mod_spec(d, 2, row_fn, layer),
        ],
        out_specs=blk,
        out_shape=jax.ShapeDtypeStruct((bsz, n2, n1, d), F32),
        compiler_params=_cparams("arbitrary", "arbitrary"),
        name="fourier_out_residual",
    )(y, w_bf, bias, x4, modv)
    return out.reshape(bsz * n, d)


def _fourier_tables(n, dg):
    assert n & (n - 1) == 0
    n2 = 1 << ((n.bit_length() - 1) // 2)
    n1 = n // n2
    assert n1 % ROW_CHUNK == 0 and n2 % ROW_CHUNK == 0
    i32 = jnp.int32
    k1 = jnp.arange(n1, dtype=i32)[None, :, None]
    pos = n2 * jnp.arange(n1, dtype=i32)[None, None, :] + jnp.arange(n2, dtype=i32)[:, None, None]
    ang1 = ((k1 * pos) % n).astype(F32) * (2.0 * math.pi / n)
    m_tab = jnp.concatenate([jnp.cos(ang1), -jnp.sin(ang1)], axis=1).astype(BF16)
    c2, s2 = _dft_tables(n2)
    w2 = jnp.concatenate([jnp.concatenate([c2, s2], axis=1),
                          jnp.concatenate([-s2, c2], axis=1)], axis=0).astype(BF16)
    cc, sc = _dft_tables(dg)
    cs = jnp.stack([cc, sc]).astype(BF16)
    return n1, n2, m_tab, w2, cs


def _mm_res_kernel(y_ref, w_ref, b_ref, r_ref, g_ref, o_ref):
    acc = jnp.dot(y_ref[...], w_ref[...], preferred_element_type=F32)
    o_ref[...] = r_ref[...] + g_ref[...] * (acc + b_ref[...])


def _mm_residual(y2d, w_bf, bias, res2d, modv, layer, piece, row_fn, tm):
    t, k = y2d.shape
    d = w_bf.shape[1]
    tn = min(1024, d)
    nj = d // tn
    return pl.pallas_call(
        _mm_res_kernel,
        grid=(nj, t // tm),
        in_specs=[
            pl.BlockSpec((tm, k), lambda j, i: (i, 0)),
            pl.BlockSpec((k, tn), lambda j, i: (0, j)),
            pl.BlockSpec((1, tn), lambda j, i: (0, j)),
            pl.BlockSpec((tm, tn), lambda j, i: (i, j)),
            pl.BlockSpec((None, None, 1, tn), lambda j, i: (layer, row_fn(i), 0, piece * nj + j)),
        ],
        out_specs=pl.BlockSpec((tm, tn), lambda j, i: (i, j)),
        out_shape=jax.ShapeDtypeStruct((t, d), F32),
        compiler_params=_cparams("arbitrary", "arbitrary"),
        name="mixer_out_residual",
    )(y2d, w_bf, bias, res2d, modv)


def _route_kernel(x_ref, g_ref, sh_ref, sc_ref, wr_ref, br_ref, f_ref, r_ref, cnt_ref):
    @pl.when(pl.program_id(0) == 0)
    def _():
        cnt_ref[...] = jnp.zeros_like(cnt_ref)

    f = _norm_mod(x_ref[...], g_ref[...], sh_ref[...], sc_ref[...])
    f_ref[...] = _pack_bf16_pairs(f)
    logits = jnp.dot(f, wr_ref[...], preferred_element_type=F32, precision=HIGHEST) + br_ref[...]
    tm, ne = logits.shape
    col = lax.broadcasted_iota(jnp.int32, (tm, ne), 1).astype(F32)
    lane = lax.broadcasted_iota(jnp.int32, (tm, LANES), 1)
    out = jnp.zeros((tm, LANES), F32)
    vals, idxs = [], []
    hot = jnp.zeros((tm, ne), F32)
    for k in range(TOP_K):
        m = jnp.max(logits, axis=-1, keepdims=True)
        idx = jnp.min(jnp.where(logits == m, col, float(ne)), axis=-1, keepdims=True)
        logits = jnp.where(col == idx, -jnp.inf, logits)
        hot = jnp.where(col == idx, 1.0, hot)
        out = jnp.where(lane == k, idx, out)
        vals.append(m)
        idxs.append(idx)
    es = [jnp.exp(v - vals[0]) for v in vals]
    den = es[0]
    for e in es[1:]:
        den = den + e
    for k in range(TOP_K):
        out = jnp.where(lane == TOP_K + k, es[k] / den, out)
    earlier = (lax.broadcasted_iota(jnp.int32, (tm, tm), 1)
               < lax.broadcasted_iota(jnp.int32, (tm, tm), 0))
    before = jnp.dot(jnp.where(earlier, 1.0, 0.0).astype(BF16), hot.astype(BF16),
                     preferred_element_type=F32) + cnt_ref[:, :ne]
    for k in range(TOP_K):
        rank = jnp.sum(jnp.where(col == idxs[k], before, 0.0), axis=-1, keepdims=True)
        out = jnp.where(lane == 2 * TOP_K + k, rank, out)
    cnt_ref[:, :ne] = cnt_ref[:, :ne] + jnp.sum(hot, axis=0, keepdims=True)
    r_ref[...] = out


def _ffn_route(x2d, g, modv, layer, row_fn, w_router, b_router, tm):
    t, d = x2d.shape
    ne = w_router.shape[1]
    spec = pl.BlockSpec((tm, d), lambda i: (i, 0))
    return pl.pallas_call(
        _route_kernel,
        grid=(t // tm,),
        in_specs=[
            spec,
            pl.BlockSpec((1, d), lambda i: (0, 0)),
            _mod_spec(d, 3, row_fn, layer),
            _mod_spec(d, 4, row_fn, layer),
            pl.BlockSpec((d, ne), lambda i: (0, 0)),
            pl.BlockSpec((1, ne), lambda i: (0, 0)),
        ],
        out_specs=[pl.BlockSpec((tm, d // 2), lambda i: (i, 0)),
                   pl.BlockSpec((tm, LANES), lambda i: (i, 0)),
                   pl.BlockSpec((1, LANES), lambda i: (0, 0))],
        out_shape=[jax.ShapeDtypeStruct((t, d // 2), jnp.uint32),
                   jax.ShapeDtypeStruct((t, LANES), F32),
                   jax.ShapeDtypeStruct((1, LANES), F32)],
        compiler_params=_cparams("arbitrary"),
        name="ffn_norm_route",
    )(x2d, g, modv, modv, w_router, b_router)


def _gather_kernel(tok0_ref, tokn_ref, f_hbm, xs_ref, buf, sem, *, rows):
    i = pl.program_id(0)
    slot = i % 2

    def fetch(tok_ref, slot):
        for r in range(rows):
            pltpu.make_async_copy(f_hbm.at[pl.ds(tok_ref[0, 0, r], 1)],
                                  buf.at[slot, pl.ds(r, 1)], sem.at[slot]).start()

    def drain(slot):
        pltpu.make_async_copy(buf.at[slot], buf.at[slot], sem.at[slot]).wait()

    @pl.when(i == 0)
    def _():
        fetch(tok0_ref, 0)

    drain(slot)
    fetch(tokn_ref, 1 - slot)
    xs_ref[...] = _unpack_bf16_pairs(buf[slot])

    @pl.when(i == pl.num_programs(0) - 1)
    def _():
        drain(1 - slot)


def _moe_gather(f_packed, slot_tok):
    s = slot_tok.shape[0]
    nb = s // MOE_ROWS
    rows = 2 * MOE_ROWS if nb % 2 == 0 else MOE_ROWS
    ns = s // rows
    hw = f_packed.shape[1]
    tok3 = slot_tok.reshape(ns, 1, rows)
    tok_spec = lambda fn: pl.BlockSpec((1, 1, rows), fn, memory_space=pltpu.SMEM)
    return pl.pallas_call(
        functools.partial(_gather_kernel, rows=rows),
        grid=(ns,),
        in_specs=[tok_spec(lambda i: (0, 0, 0)), tok_spec(lambda i: ((i + 1) % ns, 0, 0)),
                  pl.BlockSpec(memory_space=pl.ANY)],
        out_specs=pl.BlockSpec((rows, 2 * hw), lambda i: (i, 0)),
        out_shape=jax.ShapeDtypeStruct((s, 2 * hw), BF16),
        scratch_shapes=[pltpu.VMEM((2, rows, hw), jnp.uint32), pltpu.SemaphoreType.DMA((2,))],
        compiler_params=_cparams("arbitrary"),
        name="moe_gather",
    )(tok3, tok3, f_packed)


def _new_expert(be_ref, b):
    return (b == 0) | (be_ref[b] != be_ref[jnp.maximum(b - 1, 0)])


def _moe_up_kernel(be_ref, nu_ref, x_ref, w_ref, bg_ref, bl_ref, perm_ref, o_ref, wg_ref, wl_ref):
    b = pl.program_id(1)

    @pl.when(_new_expert(be_ref, b))
    def _():
        for q in range(w_ref.shape[1] // (2 * LANES)):
            wq = w_ref[:, q * 2 * LANES:(q + 1) * 2 * LANES].astype(BF16)
            r = jnp.dot(wq, perm_ref[...], preferred_element_type=F32)
            wg_ref[:, q * LANES:(q + 1) * LANES] = r[:, :LANES].astype(BF16)
            wl_ref[:, q * LANES:(q + 1) * LANES] = r[:, LANES:].astype(BF16)

    @pl.when(b < nu_ref[0])
    def _():
        x = x_ref[...]
        glu = jnp.dot(x, wg_ref[...], preferred_element_type=F32) + bg_ref[...]
        lin = jnp.dot(x, wl_ref[...], preferred_element_type=F32) + bl_ref[...]
        glu = jnp.minimum(glu, SWIGLU_LIMIT)
        lin = jnp.clip(lin, -SWIGLU_LIMIT, SWIGLU_LIMIT)
        o_ref[...] = (glu * jax.nn.sigmoid(SWIGLU_ALPHA * glu) * (lin + 1.0)).astype(o_ref.dtype)

    @pl.when(b >= nu_ref[0])
    def _():
        o_ref[...] = jnp.zeros_like(o_ref)


def _moe_up(xs, block_expert, n_used, w_up, b_glu, b_lin, perm, layer):
    s, d = xs.shape
    nb = s // MOE_ROWS
    de = w_up.shape[3] // 2
    tn = min(1024, de)
    bspec = pl.BlockSpec((None, None, 1, tn), lambda j, b, be, nu: (layer, be[b], 0, j))
    return pl.pallas_call(
        _moe_up_kernel,
        grid_spec=pltpu.PrefetchScalarGridSpec(
            num_scalar_prefetch=2,
            grid=(de // tn, nb),
            in_specs=[pl.BlockSpec((MOE_ROWS, d), lambda j, b, be, nu: (b, 0)),
                      pl.BlockSpec((None, None, d, 2 * tn),
                                   lambda j, b, be, nu: (layer, be[b], 0, j)),
                      bspec, bspec, pl.BlockSpec(perm.shape, lambda j, b, be, nu: (0, 0))],
            out_specs=pl.BlockSpec((MOE_ROWS, tn), lambda j, b, be, nu: (b, j)),
            scratch_shapes=[pltpu.VMEM((d, tn), BF16), pltpu.VMEM((d, tn), BF16)],
        ),
        out_shape=jax.ShapeDtypeStruct((s, de), BF16),
        compiler_params=_cparams("arbitrary", "arbitrary"),
        name="moe_up_swiglu",
    )(block_expert, n_used, xs, w_up, b_glu, b_lin, perm)


def _moe_down_kernel(be_ref, nu_ref, a_ref, w_ref, b_ref, o_ref, wbf_ref):
    b = pl.program_id(1)

    @pl.when(_new_expert(be_ref, b))
    def _():
        wbf_ref[...] = w_ref[...].astype(BF16)

    @pl.when(b < nu_ref[0])
    def _():
        o_ref[...] = _pack_bf16_pairs(
            jnp.dot(a_ref[...], wbf_ref[...], preferred_element_type=F32) + b_ref[...])

    @pl.when(b >= nu_ref[0])
    def _():
        o_ref[...] = jnp.zeros_like(o_ref)


def _moe_down(act, block_expert, n_used, w_down, b_down, layer):
    s, de = act.shape
    d = w_down.shape[3]
    nb = s // MOE_ROWS
    tn = d
    return pl.pallas_call(
        _moe_down_kernel,
        grid_spec=pltpu.PrefetchScalarGridSpec(
            num_scalar_prefetch=2,
            grid=(d // tn, nb),
            in_specs=[pl.BlockSpec((MOE_ROWS, de), lambda j, b, be, nu: (b, 0)),
                      pl.BlockSpec((None, None, de, tn), lambda j, b, be, nu: (layer, be[b], 0, j)),
                      pl.BlockSpec((None, None, 1, tn), lambda j, b, be, nu: (layer, be[b], 0, j))],
            out_specs=pl.BlockSpec((MOE_ROWS, tn // 2), lambda j, b, be, nu: (b, j)),
            scratch_shapes=[pltpu.VMEM((de, tn), BF16)],
        ),
        out_shape=jax.ShapeDtypeStruct((s, d // 2), jnp.uint32),
        compiler_params=_cparams("arbitrary", "arbitrary"),
        name="moe_down",
    )(block_expert, n_used, act, w_down, b_down)


def _combine_kernel(dest0_ref, destn_ref, ys_hbm, r_ref, x_ref, g_ref, o_ref, buf, sem, *, tm):
    i = pl.program_id(0)
    slot = i % 2

    def fetch(dest_ref, slot):
        for r in range(tm):
            for k in range(TOP_K):
                pltpu.make_async_copy(ys_hbm.at[pl.ds(dest_ref[0, 0, r * TOP_K + k], 1)],
                                      buf.at[slot, k, pl.ds(r, 1)], sem.at[slot]).start()

    def drain(slot):
        pltpu.make_async_copy(buf.at[slot], buf.at[slot], sem.at[slot]).wait()

    @pl.when(i == 0)
    def _():
        fetch(dest0_ref, 0)

    drain(slot)
    fetch(destn_ref, 1 - slot)
    route = r_ref[...]
    h = buf.shape[-1]
    acc_lo = acc_hi = 0.0
    for k in range(TOP_K):
        w = buf[slot, k]
        p = route[:, TOP_K + k:TOP_K + k + 1]
        acc_lo = acc_lo + p * pltpu.bitcast(w << 16, F32)
        acc_hi = acc_hi + p * pltpu.bitcast(w & jnp.uint32(0xFFFF0000), F32)
    o_ref[:, :h] = x_ref[:, :h] + g_ref[:, :h] * acc_lo
    o_ref[:, h:] = x_ref[:, h:] + g_ref[:, h:] * acc_hi

    @pl.when(i == pl.num_programs(0) - 1)
    def _():
        drain(1 - slot)


def _moe_combine(ys, dest, route, x2d, modv, layer, row_fn, tm):
    t, d = x2d.shape
    nt = t // tm
    dest3 = dest.reshape(nt, 1, tm * TOP_K)
    dspec = lambda fn: pl.BlockSpec((1, 1, tm * TOP_K), fn, memory_space=pltpu.SMEM)
    return pl.pallas_call(
        functools.partial(_combine_kernel, tm=tm),
        grid=(nt,),
        in_specs=[
            dspec(lambda i: (0, 0, 0)),
            dspec(lambda i: ((i + 1) % nt, 0, 0)),
            pl.BlockSpec(memory_space=pl.ANY),
            pl.BlockSpec((tm, LANES), lambda i: (i, 0)),
            pl.BlockSpec((tm, d), lambda i: (i, 0)),
            _mod_spec(d, 5, row_fn, layer),
        ],
        out_specs=pl.BlockSpec((tm, d), lambda i: (i, 0)),
        out_shape=jax.ShapeDtypeStruct((t, d), F32),
        scratch_shapes=[pltpu.VMEM((2, TOP_K, tm, d // 2), jnp.uint32),
                        pltpu.SemaphoreType.DMA((2,))],
        compiler_params=_cparams("arbitrary"),
        name="moe_combine",
    )(dest3, dest3, ys, route, x2d, modv)


def _routing_tables(route, counts_f, ne):
    t = route.shape[0]
    i32 = jnp.int32
    flat_e = route[:, :TOP_K].astype(i32).reshape(-1)
    rank = route[:, 2 * TOP_K:3 * TOP_K].astype(i32).reshape(-1)
    counts = counts_f[0, :ne].astype(i32)
    n_rows = t * TOP_K
    padded = (counts + MOE_ROWS - 1) // MOE_ROWS * MOE_ROWS
    pad_end = jnp.cumsum(padded)
    pad_start = pad_end - padded
    dest = pad_start[flat_e] + rank
    n_blocks = -(-n_rows // MOE_ROWS) + ne
    blk_row0 = jnp.arange(n_blocks, dtype=i32) * MOE_ROWS
    block_expert = jnp.minimum(
        jnp.sum((pad_end[None, :] <= blk_row0[:, None]).astype(i32), axis=1), ne - 1)
    n_used = (pad_end[-1:] // MOE_ROWS).astype(i32)
    shift = max(n_rows - 1, 1).bit_length()
    assert ne << shift < 2 ** 31
    order = jnp.sort(flat_e * (1 << shift) + jnp.arange(n_rows, dtype=i32)) & ((1 << shift) - 1)
    first = jnp.cumsum(counts) - counts
    local = (blk_row0 - pad_start[block_expert])[:, None] + jnp.arange(MOE_ROWS, dtype=i32)[None, :]
    src = jnp.clip(first[block_expert][:, None] + local, 0, n_rows - 1)
    slot_tok = jnp.where(local < counts[block_expert][:, None], order[src] // TOP_K, 0)
    return dest.astype(i32), slot_tok.reshape(-1).astype(i32), block_expert.astype(i32), n_used


def _moe_experts(f_packed, route, counts_f, w_up, b_up, w_down, b_down, layer):
    depth, ne = w_up.shape[:2]
    dest, slot_tok, block_expert, n_used = _routing_tables(route, counts_f, ne)
    b_pairs = b_up.reshape(depth, ne, 1, -1, 2)
    col = jnp.arange(2 * LANES)
    src = jnp.where(col < LANES, 2 * col, 2 * (col - LANES) + 1)
    perm = (jnp.arange(2 * LANES)[:, None] == src[None, :]).astype(BF16)
    xs = _moe_gather(f_packed, slot_tok)
    act = _moe_up(xs, block_expert, n_used, w_up, b_pairs[..., 0], b_pairs[..., 1], perm, layer)
    ys = _moe_down(act, block_expert, n_used, w_down, b_down.reshape(depth, ne, 1, -1), layer)
    return ys, dest


def _mla_proj_kernel(x_ref, g_ref, sh_ref, sc_ref, win_ref, gq_ref, gkv_ref, wq_ref, wkv_ref,
                     q_ref, kv_ref, kpe_ref, *, qp, kvp):
    h = _norm_mod(x_ref[...], g_ref[...], sh_ref[...], sc_ref[...]).astype(BF16)
    a = jnp.dot(h, win_ref[...], preferred_element_type=F32)
    cq = a[:, :qp]
    ckv = a[:, qp:qp + kvp]
    kpe_ref[...] = a[:, qp + kvp:]
    cqn = cq * lax.rsqrt(jnp.sum(cq * cq, axis=-1, keepdims=True) * (1.0 / Q_LORA_RANK) + NORM_EPS)
    ckvn = ckv * lax.rsqrt(jnp.sum(ckv * ckv, axis=-1, keepdims=True) * (1.0 / KV_LORA_RANK)
                           + NORM_EPS)
    q_ref[...] = jnp.dot((cqn * gq_ref[...]).astype(BF16), wq_ref[...], preferred_element_type=F32)
    kv_ref[...] = jnp.dot((ckvn * gkv_ref[...]).astype(BF16), wkv_ref[...],
                          preferred_element_type=F32)


def _mla_proj(x2d, g, modv, layer, row_fn, win_p, gq_p, gkv_p, wq_p, wkv_p, tm):
    t, d = x2d.shape
    qp, kvp = gq_p.shape[1], gkv_p.shape[1]
    na = win_p.shape[1]
    nq, nkv = wq_p.shape[1], wkv_p.shape[1]
    full = lambda arr: pl.BlockSpec(arr.shape, lambda i: (0, 0))
    return pl.pallas_call(
        functools.partial(_mla_proj_kernel, qp=qp, kvp=kvp),
        grid=(t // tm,),
        in_specs=[
            pl.BlockSpec((tm, d), lambda i: (i, 0)),
            pl.BlockSpec((1, d), lambda i: (0, 0)),
            _mod_spec(d, 0, row_fn, layer),
            _mod_spec(d, 1, row_fn, layer),
            full(win_p), full(gq_p), full(gkv_p), full(wq_p), full(wkv_p),
        ],
        out_specs=[pl.BlockSpec((tm, nq), lambda i: (i, 0)),
                   pl.BlockSpec((tm, nkv), lambda i: (i, 0)),
                   pl.BlockSpec((tm, na - qp - kvp), lambda i: (i, 0))],
        out_shape=[jax.ShapeDtypeStruct((t, nq), F32), jax.ShapeDtypeStruct((t, nkv), F32),
                   jax.ShapeDtypeStruct((t, na - qp - kvp), F32)],
        compiler_params=_cparams("arbitrary"),
        name="mla_projections",
    )(x2d, g, modv, modv, win_p, gq_p, gkv_p, wq_p, wkv_p)


def _rope(x, cos, sin_lo, sin_hi):
    quarter = QK_ROPE_DIM // 4
    return (x * cos + pltpu.roll(x, LANES - quarter, 1) * sin_lo
            + pltpu.roll(x, quarter, 1) * sin_hi)


def _headnorm_kernel(q_ref, kv_ref, kpe_ref, gq_ref, gk_ref, cos_ref, slo_ref, shi_ref,
                     qo_ref, ko_ref, vo_ref, *, heads, q_scale):
    inv = 1.0 / (QK_NOPE_DIM + QK_ROPE_DIM)
    cos, slo, shi = cos_ref[...], slo_ref[...], shi_ref[...]
    kpe = kpe_ref[...]
    kpe_ss = jnp.sum(kpe * kpe, axis=-1, keepdims=True)
    gq = gq_ref[...]
    gk = gk_ref[...]
    for h in range(heads):
        qh = q_ref[:, h * HEAD_SLOT:(h + 1) * HEAD_SLOT]
        rs = lax.rsqrt(jnp.sum(qh * qh, axis=-1, keepdims=True) * inv + NORM_EPS) * q_scale
        qn = qh * rs * gq
        qo_ref[h, :, :LANES] = qn[:, :LANES].astype(BF16)
        qo_ref[h, :, LANES:] = _rope(qn[:, LANES:], cos, slo, shi).astype(BF16)
        kn = kv_ref[:, h * HEAD_SLOT:h * HEAD_SLOT + QK_NOPE_DIM]
        rk = lax.rsqrt((jnp.sum(kn * kn, axis=-1, keepdims=True) + kpe_ss) * inv + NORM_EPS)
        ko_ref[h, :, :LANES] = (kn * rk * gk[:, :LANES]).astype(BF16)
        ko_ref[h, :, LANES:] = _rope(kpe * rk * gk[:, LANES:], cos, slo, shi).astype(BF16)
        vo_ref[h] = kv_ref[:, h * HEAD_SLOT + QK_NOPE_DIM:(h + 1) * HEAD_SLOT].astype(BF16)


def _head_norm_rope(q_raw, kv_raw, kpe, gq_slot, gk_slot, cos, slo, shi, bsz, n_ctx, n_lat, tm):
    heads = MLA_HEADS
    n_keys = n_ctx + n_lat
    ctx_tiles = bsz * n_ctx // tm
    per_ctx = n_ctx // tm
    per_lat = n_lat // tm

    def batch_of(i):
        return jnp.where(i < ctx_tiles, i // per_ctx, (i - ctx_tiles) // per_lat)

    def key_blk(i):
        return jnp.where(i < ctx_tiles, i % per_ctx, per_ctx + (i - ctx_tiles) % per_lat)

    tab = pl.BlockSpec((tm, LANES), lambda i: (key_blk(i), 0))
    t = q_raw.shape[0]
    out_map = lambda i: (batch_of(i), 0, key_blk(i), 0)

    def q_map(i):
        blk = jnp.where(i < ctx_tiles, per_lat + i % per_ctx, (i - ctx_tiles) % per_lat)
        return (batch_of(i), 0, blk, 0)

    return pl.pallas_call(
        functools.partial(_headnorm_kernel, heads=heads,
                          q_scale=float(QK_NOPE_DIM + QK_ROPE_DIM) ** -0.5),
        grid=(t // tm,),
        in_specs=[
            pl.BlockSpec((tm, heads * HEAD_SLOT), lambda i: (i, 0)),
            pl.BlockSpec((tm, heads * HEAD_SLOT), lambda i: (i, 0)),
            pl.BlockSpec((tm, LANES), lambda i: (i, 0)),
            pl.BlockSpec((1, HEAD_SLOT), lambda i: (0, 0)),
            pl.BlockSpec((1, HEAD_SLOT), lambda i: (0, 0)),
            tab, tab, tab,
        ],
        out_specs=[pl.BlockSpec((None, heads, tm, HEAD_SLOT), q_map),
                   pl.BlockSpec((None, heads, tm, HEAD_SLOT), out_map),
                   pl.BlockSpec((None, heads, tm, V_HEAD_DIM), out_map)],
        out_shape=[jax.ShapeDtypeStruct((bsz, heads, n_keys, HEAD_SLOT), BF16),
                   jax.ShapeDtypeStruct((bsz, heads, n_keys, HEAD_SLOT), BF16),
                   jax.ShapeDtypeStruct((bsz, heads, n_keys, V_HEAD_DIM), BF16)],
        compiler_params=_cparams("arbitrary"),
        name="mla_headnorm_rope",
    )(q_raw, kv_raw, kpe, gq_slot, gk_slot, cos, slo, shi)


def _attn_kernel(q_ref, k_ref, v_ref, o_ref, *, sub):
    for r in range(q_ref.shape[0] // sub):
        rows = pl.ds(r * sub, sub)
        s = lax.dot_general(q_ref[rows, :], k_ref[...], (((1,), (1,)), ((), ())),
                            preferred_element_type=F32)
        m = jnp.max(s, axis=-1, keepdims=True)
        p = jnp.exp(s - m)
        l = jnp.sum(p, axis=-1, keepdims=True)
        o = jnp.dot(p.astype(BF16), v_ref[...], preferred_element_type=F32)
        o_ref[rows, :] = (o / l).astype(o_ref.dtype)


def _attention(q, k, v, n_ctx, tq):
    bsz, heads, n_keys, _ = k.shape
    n_lat = n_keys - n_ctx
    return pl.pallas_call(
        functools.partial(_attn_kernel, sub=min(256, tq)),
        grid=(bsz, heads, n_lat // tq),
        in_specs=[
            pl.BlockSpec((None, None, tq, HEAD_SLOT), lambda b, h, i: (b, h, i, 0)),
            pl.BlockSpec((None, None, n_keys, HEAD_SLOT), lambda b, h, i: (b, h, 0, 0)),
            pl.BlockSpec((None, None, n_keys, V_HEAD_DIM), lambda b, h, i: (b, h, 0, 0)),
        ],
        out_specs=pl.BlockSpec((None, tq, V_HEAD_DIM), lambda b, h, i: (b, i, h)),
        out_shape=jax.ShapeDtypeStruct((bsz, n_lat, heads * V_HEAD_DIM), BF16),
        compiler_params=_cparams("arbitrary", "arbitrary", "arbitrary"),
        name="mla_attention",
    )(q, k, v)


def _dft_tables(n):
    k = jnp.arange(n, dtype=jnp.int32)
    ang = ((k[:, None] * k[None, :]) % n).astype(F32) * (2.0 * math.pi / n)
    return jnp.cos(ang), jnp.sin(ang)


def _rope_tables(n_ctx, n_lat):
    rows = n_lat // GRID_W
    pairs = QK_ROPE_DIM // 4
    row = jnp.repeat(jnp.arange(rows, dtype=F32), GRID_W)
    col = jnp.tile(jnp.arange(GRID_W, dtype=F32), rows)
    inv = ROPE_THETA ** (-jnp.arange(pairs, dtype=F32) / pairs)
    ang = jnp.stack([row[:, None] * inv, col[:, None] * inv], axis=1)
    cos, sin = jnp.cos(ang), jnp.sin(ang)
    zero = jnp.zeros_like(sin)
    cos_t = jnp.stack([cos, cos], axis=2).reshape(n_lat, QK_ROPE_DIM)
    slo_t = jnp.stack([-sin, zero], axis=2).reshape(n_lat, QK_ROPE_DIM)
    shi_t = jnp.stack([zero, sin], axis=2).reshape(n_lat, QK_ROPE_DIM)

    def full(tab, ctx_val):
        tab = jnp.pad(tab, ((0, 0), (0, LANES - QK_ROPE_DIM)))
        return jnp.concatenate([jnp.full((n_ctx, LANES), ctx_val, F32), tab], axis=0)

    return full(cos_t, 1.0), full(slo_t, 0.0), full(shi_t, 0.0)


def _pad_cols(w, n):
    return jnp.pad(w, ((0, 0), (0, n - w.shape[1])))


def _mla_weights(w_in, g_q_lora, w_q_up, g_kv_lora, w_kv_up, g_q_head, g_k_head):
    qp = -(-Q_LORA_RANK // LANES) * LANES
    kvp = -(-KV_LORA_RANK // LANES) * LANES
    qr, kvr = Q_LORA_RANK, KV_LORA_RANK
    win_p = jnp.concatenate([_pad_cols(w_in[:, :qr], qp), _pad_cols(w_in[:, qr:qr + kvr], kvp),
                             _pad_cols(w_in[:, qr + kvr:], LANES)], axis=1).astype(BF16)
    gq_p = _pad_cols(g_q_lora[None, :], qp)
    gkv_p = _pad_cols(g_kv_lora[None, :], kvp)
    hd = QK_NOPE_DIM + QK_ROPE_DIM
    wq = w_q_up.reshape(qr, MLA_HEADS, hd)
    wq = jnp.pad(wq, ((0, qp - qr), (0, 0), (0, HEAD_SLOT - hd)))
    wq_p = wq.reshape(qp, MLA_HEADS * HEAD_SLOT).astype(BF16)
    wkv_p = jnp.pad(w_kv_up, ((0, kvp - kvr), (0, 0))).astype(BF16)
    gq_slot = _pad_cols(g_q_head[None, :], HEAD_SLOT)
    gk_slot = _pad_cols(g_k_head[None, :], HEAD_SLOT)
    return win_p, gq_p, gkv_p, wq_p, wkv_p, gq_slot, gk_slot


def kernel(x, c, ctx, c_ctx, w_mod, b_mod, g_mix, g_ffn, fourier_w_out, fourier_b_out, mla_w_in, mla_g_q_lora, mla_w_q_up, mla_g_kv_lora, mla_w_kv_up, mla_g_q_head, mla_g_k_head, mla_w_out, router_w, router_b, expert_w_up, expert_b_up, expert_w_down, expert_b_down):
    bsz, n, d = x.shape
    l = ctx.shape[1]
    assert bsz < 8 and n % GRID_W == 0
    tl = min(256, l)
    assert l % tl == 0 and n % tl == 0
    ctx_row = bsz

    c8 = jnp.zeros((8, d), F32).at[:bsz].set(c).at[ctx_row].set(c_ctx)
    modv = _mod_vectors(c8, w_mod, b_mod).reshape(w_mod.shape[0], 8, 1, 6 * d)

    def lat_row(tm):
        return lambda i: i // (n // tm)

    def all_row(tm):
        nct = bsz * l // tm
        return lambda i: jnp.where(i < nct, ctx_row, (i - nct) // (n // tm))

    dg = d // FOURIER_GROUPS
    wf = fourier_w_out[0].astype(BF16)
    bf = fourier_b_out[0][None, :]
    g0 = g_mix[0][None, :]

    def fourier(x3, row_fn):
        nn = x3.shape[1]
        n1, n2, m_tab, w2, cs = _fourier_tables(nn, dg)
        tr, ti = _fft_stage1(x3, g0, modv, 0, row_fn, m_tab, n1, n2)
        y = _fft_stage2(tr, ti, w2, cs, 1.0 / math.sqrt(nn * dg))
        return _fft_out(y, wf, bf, x3, modv, 0, row_fn, n1, n2)

    tm_lat = min(512, n)
    x_lat1 = fourier(x, lambda b, j: b)
    x_ctx1 = fourier(ctx, lambda b, j: ctx_row)
    x_all = jnp.concatenate([x_ctx1, x_lat1], axis=0)
    n_ctx_tok = bsz * l

    f0, route0, cnt0 = _ffn_route(x_all, g_ffn[0][None, :], modv, 0, all_row(tl), router_w[0],
                                  router_b[0][None, :], tl)
    ys0, dest0 = _moe_experts(f0, route0, cnt0, expert_w_up, expert_b_up, expert_w_down,
                              expert_b_down, 0)
    tc = min(128, tl)
    x_all2 = _moe_combine(ys0, dest0, route0, x_all, modv, 0, all_row(tc), tc)

    win_p, gq_p, gkv_p, wq_p, wkv_p, gq_slot, gk_slot = _mla_weights(
        mla_w_in[0], mla_g_q_lora[0], mla_w_q_up[0], mla_g_kv_lora[0], mla_w_kv_up[0],
        mla_g_q_head[0], mla_g_k_head[0])
    q_raw, kv_raw, kpe = _mla_proj(x_all2, g_mix[1][None, :], modv, 1, all_row(tl),
                                   win_p, gq_p, gkv_p, wq_p, wkv_p, tl)
    cos, slo, shi = _rope_tables(l, n)
    qh, kh, vh = _head_norm_rope(q_raw, kv_raw, kpe, gq_slot, gk_slot, cos, slo, shi,
                                 bsz, l, n, tl)
    attn = _attention(qh, kh, vh, l, min(1024, n))
    x_lat2 = x_all2[n_ctx_tok:]
    zero_bias = jnp.zeros((1, d), F32)
    x_lat3 = _mm_residual(attn.reshape(bsz * n, -1), mla_w_out[0].astype(BF16), zero_bias,
                          x_lat2, modv, 1, 2, lat_row(tm_lat), tm_lat)

    f1, route1, cnt1 = _ffn_route(x_lat3, g_ffn[1][None, :], modv, 1, lat_row(tl), router_w[1],
                                  router_b[1][None, :], tl)
    ys1, dest1 = _moe_experts(f1, route1, cnt1, expert_w_up, expert_b_up, expert_w_down,
                              expert_b_down, 1)
    out = _moe_combine(ys1, dest1, route1, x_lat3, modv, 1, lat_row(tc), tc)
    return out.reshape(bsz, n, d)
```

```python
import functools
import math

import jax
import jax.numpy as jnp
from jax import lax
from jax.experimental import pallas as pl
from jax.experimental.pallas import tpu as pltpu
from jax.experimental.pallas import tpu_sc as plsc

GRID_W = 64
FOURIER_GROUPS = 4
MLA_HEADS = 16
Q_LORA_RANK = 448
KV_LORA_RANK = 512
QK_NOPE_DIM = 128
QK_ROPE_DIM = 64
V_HEAD_DIM = 128
ROPE_THETA = 10000.0
TOP_K = 4
SWIGLU_ALPHA = 1.702
SWIGLU_LIMIT = 7.0
NORM_EPS = 1e-6

LANES = 128
MOE_ROWS = 256
HEAD_SLOT = 256
SC_PIECE_WORDS = 256
SC_WINDOW = 128
VMEM_LIMIT = 56 << 20

F32 = jnp.float32
BF16 = jnp.bfloat16
HIGHEST = lax.Precision.HIGHEST


def _cparams(*sem, vmem=VMEM_LIMIT):
    return pltpu.CompilerParams(dimension_semantics=sem, vmem_limit_bytes=vmem)


def _norm_mod(x, g, sh, sc):
    ms = jnp.mean(x * x, axis=-1, keepdims=True)
    return (x * lax.rsqrt(ms + NORM_EPS)) * g * (1.0 + sc) + sh


def _pack_bf16_pairs(x):
    h = x.shape[1] // 2
    lo = pltpu.bitcast(x[:, :h].astype(BF16).astype(F32), jnp.uint32)
    hi = pltpu.bitcast(x[:, h:].astype(BF16).astype(F32), jnp.uint32)
    return hi | (lo >> 16)


def _unpack_bf16_pairs(w):
    lo = pltpu.bitcast(w << 16, F32).astype(BF16)
    hi = pltpu.bitcast(w & jnp.uint32(0xFFFF0000), F32).astype(BF16)
    return jnp.concatenate([lo, hi], axis=1)


def _piece_width(words):
    return min(SC_PIECE_WORDS, words)


def _store_pieces(ref, words):
    for c in range(ref.shape[0]):
        ref[c] = words[:, c * ref.shape[2]:(c + 1) * ref.shape[2]]


def _load_pieces(ref):
    return jnp.concatenate([ref[c] for c in range(ref.shape[0])], axis=1)


def _sc_row_gather(table, idx):
    n = idx.shape[0]
    pw = table.shape[1]
    assert n % SC_WINDOW == 0
    mesh = plsc.VectorSubcoreMesh(core_axis_name="c", subcore_axis_name="s")

    @pl.kernel(out_type=jax.ShapeDtypeStruct((n, pw), table.dtype), mesh=mesh)
    def gather(t_hbm, i_hbm, o_hbm):
        def body(i_vmem, o_vmem):
            pltpu.sync_copy(t_hbm.at[i_vmem.at[0]], o_vmem)

        pltpu.emit_pipeline(
            body,
            grid=(n // SC_WINDOW,),
            in_specs=[pl.BlockSpec((1, SC_WINDOW), index_map=lambda i: (0, i))],
            out_specs=[pl.BlockSpec((SC_WINDOW, pw), index_map=lambda i: (i, 0))],
            core_axis_name=("c", "s"),
            dimension_semantics=(pltpu.PARALLEL,),
        )(i_hbm, o_hbm)

    return gather(table, idx.reshape(1, n))


def _mod_spec(d, piece, row_fn, layer):
    return pl.BlockSpec((None, None, 1, d), lambda *ids: (layer, row_fn(*ids), 0, piece))


def _mod_kernel(c_ref, w_ref, b_ref, o_ref):
    c = c_ref[...]
    a = c * jax.nn.sigmoid(c)
    o_ref[...] = jnp.dot(a, w_ref[...], preferred_element_type=F32, precision=HIGHEST) + b_ref[...]


def _mod_vectors(c8, w_mod, b_mod):
    depth, d, n6 = w_mod.shape
    tn = next(t for t in (1024, 512, 256, 128) if n6 % t == 0)
    return pl.pallas_call(
        _mod_kernel,
        grid=(depth, n6 // tn),
        in_specs=[
            pl.BlockSpec((8, d), lambda l, j: (0, 0)),
            pl.BlockSpec((None, d, tn), lambda l, j: (l, 0, j)),
            pl.BlockSpec((None, 1, tn), lambda l, j: (l, 0, j)),
        ],
        out_specs=pl.BlockSpec((None, 8, tn), lambda l, j: (l, 0, j)),
        out_shape=jax.ShapeDtypeStruct((depth, 8, n6), F32),
        compiler_params=_cparams("arbitrary", "arbitrary"),
        name="mod_vectors",
    )(c8, w_mod, b_mod.reshape(depth, 1, n6))


ROW_CHUNK = 8


def _fft1_kernel(x_ref, g_ref, sh_ref, sc_ref, m_ref, tr_ref, ti_ref):
    n1 = x_ref.shape[0]
    g, sh, sc = g_ref[...], sh_ref[...], sc_ref[...]
    for c in range(x_ref.shape[1]):
        h = _norm_mod(x_ref[:, c, :], g, sh, sc).astype(BF16)
        t = jnp.dot(m_ref[c], h, preferred_element_type=F32)
        tr_ref[:, c, :] = t[:n1]
        ti_ref[:, c, :] = t[n1:]


def _fft_stage1(x3, g, modv, layer, row_fn, m_tab, n1, n2):
    bsz, n, d = x3.shape
    x4 = x3.reshape(bsz, n1, n2, d)
    blk = pl.BlockSpec((None, n1, ROW_CHUNK, d), lambda b, j: (b, 0, j, 0))
    return pl.pallas_call(
        _fft1_kernel,
        grid=(bsz, n2 // ROW_CHUNK),
        in_specs=[
            blk,
            pl.BlockSpec((1, d), lambda b, j: (0, 0)),
            _mod_spec(d, 0, row_fn, layer),
            _mod_spec(d, 1, row_fn, layer),
            pl.BlockSpec((ROW_CHUNK, 2 * n1, n1), lambda b, j: (j, 0, 0)),
        ],
        out_specs=[blk, blk],
        out_shape=[jax.ShapeDtypeStruct((bsz, n1, n2, d), F32)] * 2,
        compiler_params=_cparams("arbitrary", "arbitrary"),
        name="fourier_stage1",
    )(x4, g, modv, modv, m_tab)


def _fft2_kernel(tr_ref, ti_ref, w2_ref, cs_ref, y_ref, *, groups, scale):
    ck, n2, d = tr_ref.shape
    dg = d // groups
    xr, xi = [], []
    for kk in range(ck):
        t = jnp.concatenate([tr_ref[kk], ti_ref[kk]], axis=0).astype(BF16)
        xx = jnp.dot(w2_ref[...], t, preferred_element_type=F32)
        xr.append(xx[:n2])
        xi.append(xx[n2:])
    xr = jnp.concatenate(xr, axis=0).astype(BF16)
    xi = jnp.concatenate(xi, axis=0).astype(BF16)
    for gi in range(groups):
        cols = slice(gi * dg, (gi + 1) * dg)
        y = (jnp.dot(xr[:, cols], cs_ref[0], preferred_element_type=F32)
             + jnp.dot(xi[:, cols], cs_ref[1], preferred_element_type=F32))
        y_ref[:, cols] = (y * scale).astype(y_ref.dtype)


def _fft_stage2(tr, ti, w2, cs, scale):
    bsz, n1, n2, d = tr.shape
    ck = ROW_CHUNK
    blk = pl.BlockSpec((None, ck, n2, d), lambda b, j: (b, j, 0, 0))
    return pl.pallas_call(
        functools.partial(_fft2_kernel, groups=FOURIER_GROUPS, scale=scale),
        grid=(bsz, n1 // ck),
        in_specs=[blk, blk,
                  pl.BlockSpec(w2.shape, lambda b, j: (0, 0)),
                  pl.BlockSpec(cs.shape, lambda b, j: (0, 0, 0))],
        out_specs=pl.BlockSpec((None, ck * n2, d), lambda b, j: (b, j, 0)),
        out_shape=jax.ShapeDtypeStruct((bsz, n1 * n2, d), BF16),
        compiler_params=_cparams("arbitrary", "arbitrary"),
        name="fourier_stage2",
    )(tr, ti, w2, cs)


def _fft_out_kernel(y_ref, w_ref, b_ref, x_ref, g_ref, o_ref):
    n2, ck, _ = x_ref.shape
    out = jnp.dot(y_ref[...], w_ref[...], preferred_element_type=F32) + b_ref[...]
    gate = g_ref[...]
    for kk in range(ck):
        o_ref[:, kk, :] = x_ref[:, kk, :] + gate * out[kk * n2:(kk + 1) * n2]


def _fft_out(y, w_bf, bias, x3, modv, layer, row_fn, n1, n2):
    bsz, n, d = x3.shape
    ck = ROW_CHUNK
    x4 = x3.reshape(bsz, n2, n1, d)
    blk = pl.BlockSpec((None, n2, ck, d), lambda b, j: (b, 0, j, 0))
    out = pl.pallas_call(
        _fft_out_kernel,
        grid=(bsz, n1 // ck),
        in_specs=[
            pl.BlockSpec((None, ck * n2, d), lambda b, j: (b, j, 0)),
            pl.BlockSpec(w_bf.shape, lambda b, j: (0, 0)),
            pl.BlockSpec((1, d), lambda b, j: (0, 0)),
            blk,
            _mod_spec(d, 2, row_fn, layer),
        ],
        out_specs=blk,
        out_shape=jax.ShapeDtypeStruct((bsz, n2, n1, d), F32),
        compiler_params=_cparams("arbitrary", "arbitrary"),
        name="fourier_out_residual",
    )(y, w_bf, bias, x4, modv)
    return out.reshape(bsz * n, d)


def _fourier_tables(n, dg):
    assert n & (n - 1) == 0
    n2 = 1 << ((n.bit_length() - 1) // 2)
    n1 = n // n2
    assert n1 % ROW_CHUNK == 0 and n2 % ROW_CHUNK == 0
    i32 = jnp.int32
    k1 = jnp.arange(n1, dtype=i32)[None, :, None]
    pos = n2 * jnp.arange(n1, dtype=i32)[None, None, :] + jnp.arange(n2, dtype=i32)[:, None, None]
    ang1 = ((k1 * pos) % n).astype(F32) * (2.0 * math.pi / n)
    m_tab = jnp.concatenate([jnp.cos(ang1), -jnp.sin(ang1)], axis=1).astype(BF16)
    c2, s2 = _dft_tables(n2)
    w2 = jnp.concatenate([jnp.concatenate([c2, s2], axis=1),
                          jnp.concatenate([-s2, c2], axis=1)], axis=0).astype(BF16)
    cc, sc = _dft_tables(dg)
    cs = jnp.stack([cc, sc]).astype(BF16)
    return n1, n2, m_tab, w2, cs


def _mm_res_kernel(y_ref, w_ref, b_ref, r_ref, g_ref, o_ref):
    acc = jnp.dot(y_ref[...], w_ref[...], preferred_element_type=F32)
    o_ref[...] = r_ref[...] + g_ref[...] * (acc + b_ref[...])


def _mm_residual(y2d, w_bf, bias, res2d, modv, layer, piece, row_fn, tm):
    t, k = y2d.shape
    d = w_bf.shape[1]
    tn = min(1024, d)
    nj = d // tn
    return pl.pallas_call(
        _mm_res_kernel,
        grid=(nj, t // tm),
        in_specs=[
            pl.BlockSpec((tm, k), lambda j, i: (i, 0)),
            pl.BlockSpec((k, tn), lambda j, i: (0, j)),
            pl.BlockSpec((1, tn), lambda j, i: (0, j)),
            pl.BlockSpec((tm, tn), lambda j, i: (i, j)),
            pl.BlockSpec((None, None, 1, tn), lambda j, i: (layer, row_fn(i), 0, piece * nj + j)),
        ],
        out_specs=pl.BlockSpec((tm, tn), lambda j, i: (i, j)),
        out_shape=jax.ShapeDtypeStruct((t, d), F32),
        compiler_params=_cparams("arbitrary", "arbitrary"),
        name="mixer_out_residual",
    )(y2d, w_bf, bias, res2d, modv)


def _route_kernel(x_ref, g_ref, sh_ref, sc_ref, wr_ref, br_ref, f_ref, r_ref, cnt_ref):
    @pl.when(pl.program_id(0) == 0)
    def _():
        cnt_ref[...] = jnp.zeros_like(cnt_ref)

    f = _norm_mod(x_ref[...], g_ref[...], sh_ref[...], sc_ref[...])
    _store_pieces(f_ref, _pack_bf16_pairs(f))
    logits = jnp.dot(f, wr_ref[...], preferred_element_type=F32, precision=HIGHEST) + br_ref[...]
    tm, ne = logits.shape
    col = lax.broadcasted_iota(jnp.int32, (tm, ne), 1).astype(F32)
    lane = lax.broadcasted_iota(jnp.int32, (tm, LANES), 1)
    out = jnp.zeros((tm, LANES), F32)
    vals, idxs = [], []
    hot = jnp.zeros((tm, ne), F32)
    for k in range(TOP_K):
        m = jnp.max(logits, axis=-1, keepdims=True)
        idx = jnp.min(jnp.where(logits == m, col, float(ne)), axis=-1, keepdims=True)
        logits = jnp.where(col == idx, -jnp.inf, logits)
        hot = jnp.where(col == idx, 1.0, hot)
        out = jnp.where(lane == k, idx, out)
        vals.append(m)
        idxs.append(idx)
    es = [jnp.exp(v - vals[0]) for v in vals]
    den = es[0]
    for e in es[1:]:
        den = den + e
    for k in range(TOP_K):
        out = jnp.where(lane == TOP_K + k, es[k] / den, out)
    earlier = (lax.broadcasted_iota(jnp.int32, (tm, tm), 1)
               < lax.broadcasted_iota(jnp.int32, (tm, tm), 0))
    before = jnp.dot(jnp.where(earlier, 1.0, 0.0).astype(BF16), hot.astype(BF16),
                     preferred_element_type=F32) + cnt_ref[:, :ne]
    for k in range(TOP_K):
        rank = jnp.sum(jnp.where(col == idxs[k], before, 0.0), axis=-1, keepdims=True)
        out = jnp.where(lane == 2 * TOP_K + k, rank, out)
    cnt_ref[:, :ne] = cnt_ref[:, :ne] + jnp.sum(hot, axis=0, keepdims=True)
    r_ref[...] = out


def _ffn_route(x2d, g, modv, layer, row_fn, w_router, b_router, tm):
    t, d = x2d.shape
    ne = w_router.shape[1]
    pw = _piece_width(d // 2)
    spec = pl.BlockSpec((tm, d), lambda i: (i, 0))
    return pl.pallas_call(
        _route_kernel,
        grid=(t // tm,),
        in_specs=[
            spec,
            pl.BlockSpec((1, d), lambda i: (0, 0)),
            _mod_spec(d, 3, row_fn, layer),
            _mod_spec(d, 4, row_fn, layer),
            pl.BlockSpec((d, ne), lambda i: (0, 0)),
            pl.BlockSpec((1, ne), lambda i: (0, 0)),
        ],
        out_specs=[pl.BlockSpec((d // 2 // pw, tm, pw), lambda i: (0, i, 0)),
                   pl.BlockSpec((tm, LANES), lambda i: (i, 0)),
                   pl.BlockSpec((1, LANES), lambda i: (0, 0))],
        out_shape=[jax.ShapeDtypeStruct((d // 2 // pw, t, pw), jnp.uint32),
                   jax.ShapeDtypeStruct((t, LANES), F32),
                   jax.ShapeDtypeStruct((1, LANES), F32)],
        compiler_params=_cparams("arbitrary"),
        name="ffn_norm_route",
    )(x2d, g, modv, modv, w_router, b_router)


def _moe_gather(f_pieces, slot_tok):
    npc, t, pw = f_pieces.shape
    s = slot_tok.shape[0]
    idx = (jnp.arange(npc, dtype=jnp.int32)[:, None] * t + slot_tok[None, :]).reshape(-1)
    return _sc_row_gather(f_pieces.reshape(npc * t, pw), idx).reshape(npc, s, pw)


def _new_expert(be_ref, b):
    return (b == 0) | (be_ref[b] != be_ref[jnp.maximum(b - 1, 0)])


def _moe_up_kernel(be_ref, nu_ref, x_ref, w_ref, bg_ref, bl_ref, perm_ref, o_ref, wg_ref, wl_ref):
    b = pl.program_id(1)

    @pl.when(_new_expert(be_ref, b))
    def _():
        for q in range(w_ref.shape[1] // (2 * LANES)):
            wq = w_ref[:, q * 2 * LANES:(q + 1) * 2 * LANES].astype(BF16)
            r = jnp.dot(wq, perm_ref[...], preferred_element_type=F32)
            wg_ref[:, q * LANES:(q + 1) * LANES] = r[:, :LANES].astype(BF16)
            wl_ref[:, q * LANES:(q + 1) * LANES] = r[:, LANES:].astype(BF16)

    @pl.when(b < nu_ref[0])
    def _():
        x = _unpack_bf16_pairs(_load_pieces(x_ref))
        glu = jnp.dot(x, wg_ref[...], preferred_element_type=F32) + bg_ref[...]
        lin = jnp.dot(x, wl_ref[...], preferred_element_type=F32) + bl_ref[...]
        glu = jnp.minimum(glu, SWIGLU_LIMIT)
        lin = jnp.clip(lin, -SWIGLU_LIMIT, SWIGLU_LIMIT)
        o_ref[...] = (glu * jax.nn.sigmoid(SWIGLU_ALPHA * glu) * (lin + 1.0)).astype(o_ref.dtype)

    @pl.when(b >= nu_ref[0])
    def _():
        o_ref[...] = jnp.zeros_like(o_ref)


def _moe_up(xs, block_expert, n_used, w_up, b_glu, b_lin, perm, layer):
    npc, s, pw = xs.shape
    d = 2 * npc * pw
    nb = s // MOE_ROWS
    de = w_up.shape[3] // 2
    tn = min(1024, de)
    bspec = pl.BlockSpec((None, None, 1, tn), lambda j, b, be, nu: (layer, be[b], 0, j))
    return pl.pallas_call(
        _moe_up_kernel,
        grid_spec=pltpu.PrefetchScalarGridSpec(
            num_scalar_prefetch=2,
            grid=(de // tn, nb),
            in_specs=[pl.BlockSpec((npc, MOE_ROWS, pw), lambda j, b, be, nu: (0, b, 0)),
                      pl.BlockSpec((None, None, d, 2 * tn),
                                   lambda j, b, be, nu: (layer, be[b], 0, j)),
                      bspec, bspec, pl.BlockSpec(perm.shape, lambda j, b, be, nu: (0, 0))],
            out_specs=pl.BlockSpec((MOE_ROWS, tn), lambda j, b, be, nu: (b, j)),
            scratch_shapes=[pltpu.VMEM((d, tn), BF16), pltpu.VMEM((d, tn), BF16)],
        ),
        out_shape=jax.ShapeDtypeStruct((s, de), BF16),
        compiler_params=_cparams("arbitrary", "arbitrary"),
        name="moe_up_swiglu",
    )(block_expert, n_used, xs, w_up, b_glu, b_lin, perm)


def _moe_down_kernel(be_ref, nu_ref, a_ref, w_ref, b_ref, o_ref, wbf_ref):
    b = pl.program_id(1)

    @pl.when(_new_expert(be_ref, b))
    def _():
        wbf_ref[...] = w_ref[...].astype(BF16)

    @pl.when(b < nu_ref[0])
    def _():
        _store_pieces(o_ref, _pack_bf16_pairs(
            jnp.dot(a_ref[...], wbf_ref[...], preferred_element_type=F32) + b_ref[...]))

    @pl.when(b >= nu_ref[0])
    def _():
        o_ref[...] = jnp.zeros_like(o_ref)


def _moe_down(act, block_expert, n_used, w_down, b_down, layer):
    s, de = act.shape
    d = w_down.shape[3]
    nb = s // MOE_ROWS
    tn = d
    pw = _piece_width(d // 2)
    npc = d // 2 // pw
    return pl.pallas_call(
        _moe_down_kernel,
        grid_spec=pltpu.PrefetchScalarGridSpec(
            num_scalar_prefetch=2,
            grid=(d // tn, nb),
            in_specs=[pl.BlockSpec((MOE_ROWS, de), lambda j, b, be, nu: (b, 0)),
                      pl.BlockSpec((None, None, de, tn), lambda j, b, be, nu: (layer, be[b], 0, j)),
                      pl.BlockSpec((None, None, 1, tn), lambda j, b, be, nu: (layer, be[b], 0, j))],
            out_specs=pl.BlockSpec((npc, MOE_ROWS, pw), lambda j, b, be, nu: (0, b, 0)),
            scratch_shapes=[pltpu.VMEM((de, tn), BF16)],
        ),
        out_shape=jax.ShapeDtypeStruct((npc, s, pw), jnp.uint32),
        compiler_params=_cparams("arbitrary", "arbitrary"),
        name="moe_down",
    )(block_expert, n_used, act, w_down, b_down)


def _combine_kernel(y_ref, r_ref, x_ref, g_ref, o_ref):
    route = r_ref[...]
    h = x_ref.shape[1] // 2
    acc_lo = acc_hi = 0.0
    for k in range(TOP_K):
        w = jnp.concatenate([y_ref[c, k] for c in range(y_ref.shape[0])], axis=1)
        p = route[:, TOP_K + k:TOP_K + k + 1]
        acc_lo = acc_lo + p * pltpu.bitcast(w << 16, F32)
        acc_hi = acc_hi + p * pltpu.bitcast(w & jnp.uint32(0xFFFF0000), F32)
    o_ref[:, :h] = x_ref[:, :h] + g_ref[:, :h] * acc_lo
    o_ref[:, h:] = x_ref[:, h:] + g_ref[:, h:] * acc_hi


def _moe_combine(ys, dest, route, x2d, modv, layer, row_fn, tm):
    t, d = x2d.shape
    npc, s, pw = ys.shape
    idx = (jnp.arange(npc, dtype=jnp.int32)[:, None, None] * s
           + dest.reshape(t, TOP_K).T[None, :, :]).reshape(-1)
    rows = _sc_row_gather(ys.reshape(npc * s, pw), idx).reshape(npc, TOP_K, t, pw)
    return pl.pallas_call(
        _combine_kernel,
        grid=(t // tm,),
        in_specs=[
            pl.BlockSpec((npc, TOP_K, tm, pw), lambda i: (0, 0, i, 0)),
            pl.BlockSpec((tm, LANES), lambda i: (i, 0)),
            pl.BlockSpec((tm, d), lambda i: (i, 0)),
            _mod_spec(d, 5, row_fn, layer),
        ],
        out_specs=pl.BlockSpec((tm, d), lambda i: (i, 0)),
        out_shape=jax.ShapeDtypeStruct((t, d), F32),
        compiler_params=_cparams("arbitrary"),
        name="moe_combine",
    )(rows, route, x2d, modv)


def _routing_tables(route, counts_f, ne):
    t = route.shape[0]
    i32 = jnp.int32
    flat_e = route[:, :TOP_K].astype(i32).reshape(-1)
    rank = route[:, 2 * TOP_K:3 * TOP_K].astype(i32).reshape(-1)
    counts = counts_f[0, :ne].astype(i32)
    n_rows = t * TOP_K
    padded = (counts + MOE_ROWS - 1) // MOE_ROWS * MOE_ROWS
    pad_end = jnp.cumsum(padded)
    pad_start = pad_end - padded
    dest = pad_start[flat_e] + rank
    n_blocks = -(-n_rows // MOE_ROWS) + ne
    blk_row0 = jnp.arange(n_blocks, dtype=i32) * MOE_ROWS
    block_expert = jnp.minimum(
        jnp.sum((pad_end[None, :] <= blk_row0[:, None]).astype(i32), axis=1), ne - 1)
    n_used = (pad_end[-1:] // MOE_ROWS).astype(i32)
    shift = max(n_rows - 1, 1).bit_length()
    assert ne << shift < 2 ** 31
    order = jnp.sort(flat_e * (1 << shift) + jnp.arange(n_rows, dtype=i32)) & ((1 << shift) - 1)
    first = jnp.cumsum(counts) - counts
    local = (blk_row0 - pad_start[block_expert])[:, None] + jnp.arange(MOE_ROWS, dtype=i32)[None, :]
    src = jnp.clip(first[block_expert][:, None] + local, 0, n_rows - 1)
    slot_tok = jnp.where(local < counts[block_expert][:, None], order[src] // TOP_K, 0)
    return dest.astype(i32), slot_tok.reshape(-1).astype(i32), block_expert.astype(i32), n_used


def _moe_experts(f_packed, route, counts_f, w_up, b_up, w_down, b_down, layer):
    depth, ne = w_up.shape[:2]
    dest, slot_tok, block_expert, n_used = _routing_tables(route, counts_f, ne)
    b_pairs = b_up.reshape(depth, ne, 1, -1, 2)
    col = jnp.arange(2 * LANES)
    src = jnp.where(col < LANES, 2 * col, 2 * (col - LANES) + 1)
    perm = (jnp.arange(2 * LANES)[:, None] == src[None, :]).astype(BF16)
    xs = _moe_gather(f_packed, slot_tok)
    act = _moe_up(xs, block_expert, n_used, w_up, b_pairs[..., 0], b_pairs[..., 1], perm, layer)
    ys = _moe_down(act, block_expert, n_used, w_down, b_down.reshape(depth, ne, 1, -1), layer)
    return ys, dest


def _mla_proj_kernel(x_ref, g_ref, sh_ref, sc_ref, win_ref, gq_ref, gkv_ref, wq_ref, wkv_ref,
                     q_ref, kv_ref, kpe_ref, *, qp, kvp):
    h = _norm_mod(x_ref[...], g_ref[...], sh_ref[...], sc_ref[...]).astype(BF16)
    a = jnp.dot(h, win_ref[...], preferred_element_type=F32)
    cq = a[:, :qp]
    ckv = a[:, qp:qp + kvp]
    kpe_ref[...] = a[:, qp + kvp:]
    cqn = cq * lax.rsqrt(jnp.sum(cq * cq, axis=-1, keepdims=True) * (1.0 / Q_LORA_RANK) + NORM_EPS)
    ckvn = ckv * lax.rsqrt(jnp.sum(ckv * ckv, axis=-1, keepdims=True) * (1.0 / KV_LORA_RANK)
                           + NORM_EPS)
    q_ref[...] = jnp.dot((cqn * gq_ref[...]).astype(BF16), wq_ref[...], preferred_element_type=F32)
    kv_ref[...] = jnp.dot((ckvn * gkv_ref[...]).astype(BF16), wkv_ref[...],
                          preferred_element_type=F32)


def _mla_proj(x2d, g, modv, layer, row_fn, win_p, gq_p, gkv_p, wq_p, wkv_p, tm):
    t, d = x2d.shape
    qp, kvp = gq_p.shape[1], gkv_p.shape[1]
    na = win_p.shape[1]
    nq, nkv = wq_p.shape[1], wkv_p.shape[1]
    full = lambda arr: pl.BlockSpec(arr.shape, lambda i: (0, 0))
    return pl.pallas_call(
        functools.partial(_mla_proj_kernel, qp=qp, kvp=kvp),
        grid=(t // tm,),
        in_specs=[
            pl.BlockSpec((tm, d), lambda i: (i, 0)),
            pl.BlockSpec((1, d), lambda i: (0, 0)),
            _mod_spec(d, 0, row_fn, layer),
            _mod_spec(d, 1, row_fn, layer),
            full(win_p), full(gq_p), full(gkv_p), full(wq_p), full(wkv_p),
        ],
        out_specs=[pl.BlockSpec((tm, nq), lambda i: (i, 0)),
                   pl.BlockSpec((tm, nkv), lambda i: (i, 0)),
                   pl.BlockSpec((tm, na - qp - kvp), lambda i: (i, 0))],
        out_shape=[jax.ShapeDtypeStruct((t, nq), F32), jax.ShapeDtypeStruct((t, nkv), F32),
                   jax.ShapeDtypeStruct((t, na - qp - kvp), F32)],
        compiler_params=_cparams("arbitrary"),
        name="mla_projections",
    )(x2d, g, modv, modv, win_p, gq_p, gkv_p, wq_p, wkv_p)


def _rope(x, cos, sin_lo, sin_hi):
    quarter = QK_ROPE_DIM // 4
    return (x * cos + pltpu.roll(x, LANES - quarter, 1) * sin_lo
            + pltpu.roll(x, quarter, 1) * sin_hi)


def _headnorm_kernel(q_ref, kv_ref, kpe_ref, gq_ref, gk_ref, cos_ref, slo_ref, shi_ref,
                     qo_ref, ko_ref, vo_ref, *, heads, q_scale):
    inv = 1.0 / (QK_NOPE_DIM + QK_ROPE_DIM)
    cos, slo, shi = cos_ref[...], slo_ref[...], shi_ref[...]
    kpe = kpe_ref[...]
    kpe_ss = jnp.sum(kpe * kpe, axis=-1, keepdims=True)
    gq = gq_ref[...]
    gk = gk_ref[...]
    for h in range(heads):
        qh = q_ref[:, h * HEAD_SLOT:(h + 1) * HEAD_SLOT]
        rs = lax.rsqrt(jnp.sum(qh * qh, axis=-1, keepdims=True) * inv + NORM_EPS) * q_scale
        qn = qh * rs * gq
        qo_ref[h, :, :LANES] = qn[:, :LANES].astype(BF16)
        qo_ref[h, :, LANES:] = _rope(qn[:, LANES:], cos, slo, shi).astype(BF16)
        kn = kv_ref[:, h * HEAD_SLOT:h * HEAD_SLOT + QK_NOPE_DIM]
        rk = lax.rsqrt((jnp.sum(kn * kn, axis=-1, keepdims=True) + kpe_ss) * inv + NORM_EPS)
        ko_ref[h, :, :LANES] = (kn * rk * gk[:, :LANES]).astype(BF16)
        ko_ref[h, :, LANES:] = _rope(kpe * rk * gk[:, LANES:], cos, slo, shi).astype(BF16)
        vo_ref[h] = kv_ref[:, h * HEAD_SLOT + QK_NOPE_DIM:(h + 1) * HEAD_SLOT].astype(BF16)


def _head_norm_rope(q_raw, kv_raw, kpe, gq_slot, gk_slot, cos, slo, shi, bsz, n_ctx, n_lat, tm):
    heads = MLA_HEADS
    n_keys = n_ctx + n_lat
    ctx_tiles = bsz * n_ctx // tm
    per_ctx = n_ctx // tm
    per_lat = n_lat // tm

    def batch_of(i):
        return jnp.where(i < ctx_tiles, i // per_ctx, (i - ctx_tiles) // per_lat)

    def key_blk(i):
        return jnp.where(i < ctx_tiles, i % per_ctx, per_ctx + (i - ctx_tiles) % per_lat)

    tab = pl.BlockSpec((tm, LANES), lambda i: (key_blk(i), 0))
    t = q_raw.shape[0]
    out_map = lambda i: (batch_of(i), 0, key_blk(i), 0)

    def q_map(i):
        blk = jnp.where(i < ctx_tiles, per_lat + i % per_ctx, (i - ctx_tiles) % per_lat)
        return (batch_of(i), 0, blk, 0)

    return pl.pallas_call(
        functools.partial(_headnorm_kernel, heads=heads,
                          q_scale=float(QK_NOPE_DIM + QK_ROPE_DIM) ** -0.5),
        grid=(t // tm,),
        in_specs=[
            pl.BlockSpec((tm, heads * HEAD_SLOT), lambda i: (i, 0)),
            pl.BlockSpec((tm, heads * HEAD_SLOT), lambda i: (i, 0)),
            pl.BlockSpec((tm, LANES), lambda i: (i, 0)),
            pl.BlockSpec((1, HEAD_SLOT), lambda i: (0, 0)),
            pl.BlockSpec((1, HEAD_SLOT), lambda i: (0, 0)),
            tab, tab, tab,
        ],
        out_specs=[pl.BlockSpec((None, heads, tm, HEAD_SLOT), q_map),
                   pl.BlockSpec((None, heads, tm, HEAD_SLOT), out_map),
                   pl.BlockSpec((None, heads, tm, V_HEAD_DIM), out_map)],
        out_shape=[jax.ShapeDtypeStruct((bsz, heads, n_keys, HEAD_SLOT), BF16),
                   jax.ShapeDtypeStruct((bsz, heads, n_keys, HEAD_SLOT), BF16),
                   jax.ShapeDtypeStruct((bsz, heads, n_keys, V_HEAD_DIM), BF16)],
        compiler_params=_cparams("arbitrary"),
        name="mla_headnorm_rope",
    )(q_raw, kv_raw, kpe, gq_slot, gk_slot, cos, slo, shi)


def _attn_kernel(q_ref, k_ref, v_ref, o_ref, *, sub):
    for r in range(q_ref.shape[0] // sub):
        rows = pl.ds(r * sub, sub)
        s = lax.dot_general(q_ref[rows, :], k_ref[...], (((1,), (1,)), ((), ())),
                            preferred_element_type=F32)
        m = jnp.max(s, axis=-1, keepdims=True)
        p = jnp.exp(s - m)
        l = jnp.sum(p, axis=-1, keepdims=True)
        o = jnp.dot(p.astype(BF16), v_ref[...], preferred_element_type=F32)
        o_ref[rows, :] = (o / l).astype(o_ref.dtype)


def _attention(q, k, v, n_ctx, tq):
    bsz, heads, n_keys, _ = k.shape
    n_lat = n_keys - n_ctx
    return pl.pallas_call(
        functools.partial(_attn_kernel, sub=min(256, tq)),
        grid=(bsz, heads, n_lat // tq),
        in_specs=[
            pl.BlockSpec((None, None, tq, HEAD_SLOT), lambda b, h, i: (b, h, i, 0)),
            pl.BlockSpec((None, None, n_keys, HEAD_SLOT), lambda b, h, i: (b, h, 0, 0)),
            pl.BlockSpec((None, None, n_keys, V_HEAD_DIM), lambda b, h, i: (b, h, 0, 0)),
        ],
        out_specs=pl.BlockSpec((None, tq, V_HEAD_DIM), lambda b, h, i: (b, i, h)),
        out_shape=jax.ShapeDtypeStruct((bsz, n_lat, heads * V_HEAD_DIM), BF16),
        compiler_params=_cparams("arbitrary", "arbitrary", "arbitrary"),
        name="mla_attention",
    )(q, k, v)


def _dft_tables(n):
    k = jnp.arange(n, dtype=jnp.int32)
    ang = ((k[:, None] * k[None, :]) % n).astype(F32) * (2.0 * math.pi / n)
    return jnp.cos(ang), jnp.sin(ang)


def _rope_tables(n_ctx, n_lat):
    rows = n_lat // GRID_W
    pairs = QK_ROPE_DIM // 4
    row = jnp.repeat(jnp.arange(rows, dtype=F32), GRID_W)
    col = jnp.tile(jnp.arange(GRID_W, dtype=F32), rows)
    inv = ROPE_THETA ** (-jnp.arange(pairs, dtype=F32) / pairs)
    ang = jnp.stack([row[:, None] * inv, col[:, None] * inv], axis=1)
    cos, sin = jnp.cos(ang), jnp.sin(ang)
    zero = jnp.zeros_like(sin)
    cos_t = jnp.stack([cos, cos], axis=2).reshape(n_lat, QK_ROPE_DIM)
    slo_t = jnp.stack([-sin, zero], axis=2).reshape(n_lat, QK_ROPE_DIM)
    shi_t = jnp.stack([zero, sin], axis=2).reshape(n_lat, QK_ROPE_DIM)

    def full(tab, ctx_val):
        tab = jnp.pad(tab, ((0, 0), (0, LANES - QK_ROPE_DIM)))
        return jnp.concatenate([jnp.full((n_ctx, LANES), ctx_val, F32), tab], axis=0)

    return full(cos_t, 1.0), full(slo_t, 0.0), full(shi_t, 0.0)


def _pad_cols(w, n):
    return jnp.pad(w, ((0, 0), (0, n - w.shape[1])))


def _mla_weights(w_in, g_q_lora, w_q_up, g_kv_lora, w_kv_up, g_q_head, g_k_head):
    qp = -(-Q_LORA_RANK // LANES) * LANES
    kvp = -(-KV_LORA_RANK // LANES) * LANES
    qr, kvr = Q_LORA_RANK, KV_LORA_RANK
    win_p = jnp.concatenate([_pad_cols(w_in[:, :qr], qp), _pad_cols(w_in[:, qr:qr + kvr], kvp),
                             _pad_cols(w_in[:, qr + kvr:], LANES)], axis=1).astype(BF16)
    gq_p = _pad_cols(g_q_lora[None, :], qp)
    gkv_p = _pad_cols(g_kv_lora[None, :], kvp)
    hd = QK_NOPE_DIM + QK_ROPE_DIM
    wq = w_q_up.reshape(qr, MLA_HEADS, hd)
    wq = jnp.pad(wq, ((0, qp - qr), (0, 0), (0, HEAD_SLOT - hd)))
    wq_p = wq.reshape(qp, MLA_HEADS * HEAD_SLOT).astype(BF16)
    wkv_p = jnp.pad(w_kv_up, ((0, kvp - kvr), (0, 0))).astype(BF16)
    gq_slot = _pad_cols(g_q_head[None, :], HEAD_SLOT)
    gk_slot = _pad_cols(g_k_head[None, :], HEAD_SLOT)
    return win_p, gq_p, gkv_p, wq_p, wkv_p, gq_slot, gk_slot


def kernel(x, c, ctx, c_ctx, w_mod, b_mod, g_mix, g_ffn, fourier_w_out, fourier_b_out, mla_w_in, mla_g_q_lora, mla_w_q_up, mla_g_kv_lora, mla_w_kv_up, mla_g_q_head, mla_g_k_head, mla_w_out, router_w, router_b, expert_w_up, expert_b_up, expert_w_down, expert_b_down):
    bsz, n, d = x.shape
    l = ctx.shape[1]
    assert bsz < 8 and n % GRID_W == 0
    tl = min(256, l)
    assert l % tl == 0 and n % tl == 0
    ctx_row = bsz

    c8 = jnp.zeros((8, d), F32).at[:bsz].set(c).at[ctx_row].set(c_ctx)
    modv = _mod_vectors(c8, w_mod, b_mod).reshape(w_mod.shape[0], 8, 1, 6 * d)

    def lat_row(tm):
        return lambda i: i // (n // tm)

    def all_row(tm):
        nct = bsz * l // tm
        return lambda i: jnp.where(i < nct, ctx_row, (i - nct) // (n // tm))

    dg = d // FOURIER_GROUPS
    wf = fourier_w_out[0].astype(BF16)
    bf = fourier_b_out[0][None, :]
    g0 = g_mix[0][None, :]

    def fourier(x3, row_fn):
        nn = x3.shape[1]
        n1, n2, m_tab, w2, cs = _fourier_tables(nn, dg)
        tr, ti = _fft_stage1(x3, g0, modv, 0, row_fn, m_tab, n1, n2)
        y = _fft_stage2(tr, ti, w2, cs, 1.0 / math.sqrt(nn * dg))
        return _fft_out(y, wf, bf, x3, modv, 0, row_fn, n1, n2)

    tm_lat = min(512, n)
    x_lat1 = fourier(x, lambda b, j: b)
    x_ctx1 = fourier(ctx, lambda b, j: ctx_row)
    x_all = jnp.concatenate([x_ctx1, x_lat1], axis=0)
    n_ctx_tok = bsz * l

    f0, route0, cnt0 = _ffn_route(x_all, g_ffn[0][None, :], modv, 0, all_row(tl), router_w[0],
                                  router_b[0][None, :], tl)
    ys0, dest0 = _moe_experts(f0, route0, cnt0, expert_w_up, expert_b_up, expert_w_down,
                              expert_b_down, 0)
    tc = min(128, tl)
    x_all2 = _moe_combine(ys0, dest0, route0, x_all, modv, 0, all_row(tc), tc)

    win_p, gq_p, gkv_p, wq_p, wkv_p, gq_slot, gk_slot = _mla_weights(
        mla_w_in[0], mla_g_q_lora[0], mla_w_q_up[0], mla_g_kv_lora[0], mla_w_kv_up[0],
        mla_g_q_head[0], mla_g_k_head[0])
    q_raw, kv_raw, kpe = _mla_proj(x_all2, g_mix[1][None, :], modv, 1, all_row(tl),
                                   win_p, gq_p, gkv_p, wq_p, wkv_p, tl)
    cos, slo, shi = _rope_tables(l, n)
    qh, kh, vh = _head_norm_rope(q_raw, kv_raw, kpe, gq_slot, gk_slot, cos, slo, shi,
                                 bsz, l, n, tl)
    attn = _attention(qh, kh, vh, l, min(1024, n))
    x_lat2 = x_all2[n_ctx_tok:]
    zero_bias = jnp.zeros((1, d), F32)
    x_lat3 = _mm_residual(attn.reshape(bsz * n, -1), mla_w_out[0].astype(BF16), zero_bias,
                          x_lat2, modv, 1, 2, lat_row(tm_lat), tm_lat)

    f1, route1, cnt1 = _ffn_route(x_lat3, g_ffn[1][None, :], modv, 1, lat_row(tl), router_w[1],
                                  router_b[1][None, :], tl)
    ys1, dest1 = _moe_experts(f1, route1, cnt1, expert_w_up, expert_b_up, expert_w_down,
                              expert_b_down, 1)
    out = _moe_combine(ys1, dest1, route1, x_lat3, modv, 1, lat_row(tc), tc)
    return out.reshape(bsz, n, d)
```

```python
import functools
import math

import jax
import jax.numpy as jnp
from jax import lax
from jax.experimental import pallas as pl
from jax.experimental.pallas import tpu as pltpu
from jax.experimental.pallas import tpu_sc as plsc

GRID_W = 64
FOURIER_GROUPS = 4
MLA_HEADS = 16
Q_LORA_RANK = 448
KV_LORA_RANK = 512
QK_NOPE_DIM = 128
QK_ROPE_DIM = 64
V_HEAD_DIM = 128
ROPE_THETA = 10000.0
TOP_K = 4
SWIGLU_ALPHA = 1.702
SWIGLU_LIMIT = 7.0
NORM_EPS = 1e-6

LANES = 128
MOE_ROWS = 256
HEAD_SLOT = 256
SC_PIECE_WORDS = 256
SC_WINDOW = 128
VMEM_LIMIT = 56 << 20

F32 = jnp.float32
BF16 = jnp.bfloat16
HIGHEST = lax.Precision.HIGHEST


def _cparams(*sem, vmem=VMEM_LIMIT):
    return pltpu.CompilerParams(dimension_semantics=sem, vmem_limit_bytes=vmem)


def _norm_mod(x, g, sh, sc):
    ms = jnp.mean(x * x, axis=-1, keepdims=True)
    return (x * lax.rsqrt(ms + NORM_EPS)) * g * (1.0 + sc) + sh


def _pack_bf16_pairs(x):
    h = x.shape[1] // 2
    lo = pltpu.bitcast(x[:, :h].astype(BF16).astype(F32), jnp.uint32)
    hi = pltpu.bitcast(x[:, h:].astype(BF16).astype(F32), jnp.uint32)
    return hi | (lo >> 16)


def _unpack_bf16_pairs(w):
    lo = pltpu.bitcast(w << 16, F32).astype(BF16)
    hi = pltpu.bitcast(w & jnp.uint32(0xFFFF0000), F32).astype(BF16)
    return jnp.concatenate([lo, hi], axis=1)


def _piece_width(words):
    return min(SC_PIECE_WORDS, words)


def _store_pieces(ref, words):
    for c in range(ref.shape[0]):
        ref[c] = words[:, c * ref.shape[2]:(c + 1) * ref.shape[2]]


def _load_pieces(ref):
    return jnp.concatenate([ref[c] for c in range(ref.shape[0])], axis=1)


def _sc_row_gather(table, idx):
    n = idx.shape[0]
    pw = table.shape[1]
    assert n % SC_WINDOW == 0
    mesh = plsc.VectorSubcoreMesh(core_axis_name="c", subcore_axis_name="s")

    @pl.kernel(out_type=jax.ShapeDtypeStruct((n, pw), table.dtype), mesh=mesh)
    def gather(t_hbm, i_hbm, o_hbm):
        def body(i_vmem, o_vmem):
            pltpu.sync_copy(t_hbm.at[i_vmem.at[0]], o_vmem)

        pltpu.emit_pipeline(
            body,
            grid=(n // SC_WINDOW,),
            in_specs=[pl.BlockSpec((1, SC_WINDOW), index_map=lambda i: (0, i))],
            out_specs=[pl.BlockSpec((SC_WINDOW, pw), index_map=lambda i: (i, 0))],
            core_axis_name=("c", "s"),
            dimension_semantics=(pltpu.PARALLEL,),
        )(i_hbm, o_hbm)

    return gather(table, idx.reshape(1, n))


def _mod_spec(d, piece, row_fn, layer):
    return pl.BlockSpec((None, None, 1, d), lambda *ids: (layer, row_fn(*ids), 0, piece))


def _mod_kernel(c_ref, w_ref, b_ref, o_ref):
    c = c_ref[...]
    a = c * jax.nn.sigmoid(c)
    o_ref[...] = jnp.dot(a, w_ref[...], preferred_element_type=F32, precision=HIGHEST) + b_ref[...]


def _mod_vectors(c8, w_mod, b_mod):
    depth, d, n6 = w_mod.shape
    tn = next(t for t in (1024, 512, 256, 128) if n6 % t == 0)
    return pl.pallas_call(
        _mod_kernel,
        grid=(depth, n6 // tn),
        in_specs=[
            pl.BlockSpec((8, d), lambda l, j: (0, 0)),
            pl.BlockSpec((None, d, tn), lambda l, j: (l, 0, j)),
            pl.BlockSpec((None, 1, tn), lambda l, j: (l, 0, j)),
        ],
        out_specs=pl.BlockSpec((None, 8, tn), lambda l, j: (l, 0, j)),
        out_shape=jax.ShapeDtypeStruct((depth, 8, n6), F32),
        compiler_params=_cparams("arbitrary", "arbitrary"),
        name="mod_vectors",
    )(c8, w_mod, b_mod.reshape(depth, 1, n6))


ROW_CHUNK = 8


def _fft1_kernel(x_ref, g_ref, sh_ref, sc_ref, m_ref, tr_ref, ti_ref):
    n1 = x_ref.shape[0]
    g, sh, sc = g_ref[...], sh_ref[...], sc_ref[...]
    for c in range(x_ref.shape[1]):
        h = _norm_mod(x_ref[:, c, :], g, sh, sc).astype(BF16)
        t = jnp.dot(m_ref[c], h, preferred_element_type=F32)
        tr_ref[:, c, :] = t[:n1]
        ti_ref[:, c, :] = t[n1:]


def _fft_stage1(x3, g, modv, layer, row_fn, m_tab, n1, n2):
    bsz, n, d = x3.shape
    x4 = x3.reshape(bsz, n1, n2, d)
    blk = pl.BlockSpec((None, n1, ROW_CHUNK, d), lambda b, j: (b, 0, j, 0))
    return pl.pallas_call(
        _fft1_kernel,
        grid=(bsz, n2 // ROW_CHUNK),
        in_specs=[
            blk,
            pl.BlockSpec((1, d), lambda b, j: (0, 0)),
            _mod_spec(d, 0, row_fn, layer),
            _mod_spec(d, 1, row_fn, layer),
            pl.BlockSpec((ROW_CHUNK, 2 * n1, n1), lambda b, j: (j, 0, 0)),
        ],
        out_specs=[blk, blk],
        out_shape=[jax.ShapeDtypeStruct((bsz, n1, n2, d), F32)] * 2,
        compiler_params=_cparams("arbitrary", "arbitrary"),
        name="fourier_stage1",
    )(x4, g, modv, modv, m_tab)


def _fft2_kernel(tr_ref, ti_ref, w2_ref, cs_ref, y_ref, *, groups, scale):
    ck, n2, d = tr_ref.shape
    dg = d // groups
    xr, xi = [], []
    for kk in range(ck):
        t = jnp.concatenate([tr_ref[kk], ti_ref[kk]], axis=0).astype(BF16)
        xx = jnp.dot(w2_ref[...], t, preferred_element_type=F32)
        xr.append(xx[:n2])
        xi.append(xx[n2:])
    xr = jnp.concatenate(xr, axis=0).astype(BF16)
    xi = jnp.concatenate(xi, axis=0).astype(BF16)
    for gi in range(groups):
        cols = slice(gi * dg, (gi + 1) * dg)
        y = (jnp.dot(xr[:, cols], cs_ref[0], preferred_element_type=F32)
             + jnp.dot(xi[:, cols], cs_ref[1], preferred_element_type=F32))
        y_ref[:, cols] = (y * scale).astype(y_ref.dtype)


def _fft_stage2(tr, ti, w2, cs, scale):
    bsz, n1, n2, d = tr.shape
    ck = ROW_CHUNK
    blk = pl.BlockSpec((None, ck, n2, d), lambda b, j: (b, j, 0, 0))
    return pl.pallas_call(
        functools.partial(_fft2_kernel, groups=FOURIER_GROUPS, scale=scale),
        grid=(bsz, n1 // ck),
        in_specs=[blk, blk,
                  pl.BlockSpec(w2.shape, lambda b, j: (0, 0)),
                  pl.BlockSpec(cs.shape, lambda b, j: (0, 0, 0))],
        out_specs=pl.BlockSpec((None, ck * n2, d), lambda b, j: (b, j, 0)),
        out_shape=jax.ShapeDtypeStruct((bsz, n1 * n2, d), BF16),
        compiler_params=_cparams("arbitrary", "arbitrary"),
        name="fourier_stage2",
    )(tr, ti, w2, cs)


def _fft_out_kernel(y_ref, w_ref, b_ref, x_ref, g_ref, o_ref):
    n2, ck, _ = x_ref.shape
    out = jnp.dot(y_ref[...], w_ref[...], preferred_element_type=F32) + b_ref[...]
    gate = g_ref[...]
    for kk in range(ck):
        o_ref[:, kk, :] = x_ref[:, kk, :] + gate * out[kk * n2:(kk + 1) * n2]


def _fft_out(y, w_bf, bias, x3, modv, layer, row_fn, n1, n2):
    bsz, n, d = x3.shape
    ck = ROW_CHUNK
    x4 = x3.reshape(bsz, n2, n1, d)
    blk = pl.BlockSpec((None, n2, ck, d), lambda b, j: (b, 0, j, 0))
    out = pl.pallas_call(
        _fft_out_kernel,
        grid=(bsz, n1 // ck),
        in_specs=[
            pl.BlockSpec((None, ck * n2, d), lambda b, j: (b, j, 0)),
            pl.BlockSpec(w_bf.shape, lambda b, j: (0, 0)),
            pl.BlockSpec((1, d), lambda b, j: (0, 0)),
            blk,
            _mod_spec(d, 2, row_fn, layer),
        ],
        out_specs=blk,
        out_shape=jax.ShapeDtypeStruct((bsz, n2, n1, d), F32),
        compiler_params=_cparams("arbitrary", "arbitrary"),
        name="fourier_out_residual",
    )(y, w_bf, bias, x4, modv)
    return out.reshape(bsz * n, d)


def _fourier_tables(n, dg):
    assert n & (n - 1) == 0
    n2 = 1 << ((n.bit_length() - 1) // 2)
    n1 = n // n2
    assert n1 % ROW_CHUNK == 0 and n2 % ROW_CHUNK == 0
    i32 = jnp.int32
    k1 = jnp.arange(n1, dtype=i32)[None, :, None]
    pos = n2 * jnp.arange(n1, dtype=i32)[None, None, :] + jnp.arange(n2, dtype=i32)[:, None, None]
    ang1 = ((k1 * pos) % n).astype(F32) * (2.0 * math.pi / n)
    m_tab = jnp.concatenate([jnp.cos(ang1), -jnp.sin(ang1)], axis=1).astype(BF16)
    c2, s2 = _dft_tables(n2)
    w2 = jnp.concatenate([jnp.concatenate([c2, s2], axis=1),
                          jnp.concatenate([-s2, c2], axis=1)], axis=0).astype(BF16)
    cc, sc = _dft_tables(dg)
    cs = jnp.stack([cc, sc]).astype(BF16)
    return n1, n2, m_tab, w2, cs


def _mm_res_kernel(y_ref, w_ref, b_ref, r_ref, g_ref, o_ref):
    acc = jnp.dot(y_ref[...], w_ref[...], preferred_element_type=F32)
    o_ref[...] = r_ref[...] + g_ref[...] * (acc + b_ref[...])


def _mm_residual(y2d, w_bf, bias, res2d, res_row0, modv, layer, piece, row_fn, tm):
    t, k = y2d.shape
    d = w_bf.shape[1]
    tn = min(1024, d)
    nj = d // tn
    off = res_row0 // tm
    assert res_row0 % tm == 0
    return pl.pallas_call(
        _mm_res_kernel,
        grid=(nj, t // tm),
        in_specs=[
            pl.BlockSpec((tm, k), lambda j, i: (i, 0)),
            pl.BlockSpec((k, tn), lambda j, i: (0, j)),
            pl.BlockSpec((1, tn), lambda j, i: (0, j)),
            pl.BlockSpec((tm, tn), lambda j, i: (i + off, j)),
            pl.BlockSpec((None, None, 1, tn), lambda j, i: (layer, row_fn(i), 0, piece * nj + j)),
        ],
        out_specs=pl.BlockSpec((tm, tn), lambda j, i: (i, j)),
        out_shape=jax.ShapeDtypeStruct((t, d), F32),
        compiler_params=_cparams("arbitrary", "arbitrary"),
        name="mixer_out_residual",
    )(y2d, w_bf, bias, res2d, modv)


def _route_kernel(x_ref, g_ref, sh_ref, sc_ref, wr_ref, br_ref, f_ref, r_ref, cnt_ref):
    @pl.when(pl.program_id(0) == 0)
    def _():
        cnt_ref[...] = jnp.zeros_like(cnt_ref)

    f = _norm_mod(x_ref[...], g_ref[...], sh_ref[...], sc_ref[...])
    _store_pieces(f_ref, _pack_bf16_pairs(f))
    logits = jnp.dot(f, wr_ref[...], preferred_element_type=F32, precision=HIGHEST) + br_ref[...]
    tm, ne = logits.shape
    col = lax.broadcasted_iota(jnp.int32, (tm, ne), 1).astype(F32)
    lane = lax.broadcasted_iota(jnp.int32, (tm, LANES), 1)
    out = jnp.zeros((tm, LANES), F32)
    vals, idxs = [], []
    hot = jnp.zeros((tm, ne), F32)
    for k in range(TOP_K):
        m = jnp.max(logits, axis=-1, keepdims=True)
        idx = jnp.min(jnp.where(logits == m, col, float(ne)), axis=-1, keepdims=True)
        logits = jnp.where(col == idx, -jnp.inf, logits)
        hot = jnp.where(col == idx, 1.0, hot)
        out = jnp.where(lane == k, idx, out)
        vals.append(m)
        idxs.append(idx)
    es = [jnp.exp(v - vals[0]) for v in vals]
    den = es[0]
    for e in es[1:]:
        den = den + e
    for k in range(TOP_K):
        out = jnp.where(lane == TOP_K + k, es[k] / den, out)
    earlier = (lax.broadcasted_iota(jnp.int32, (tm, tm), 1)
               < lax.broadcasted_iota(jnp.int32, (tm, tm), 0))
    before = jnp.dot(jnp.where(earlier, 1.0, 0.0).astype(BF16), hot.astype(BF16),
                     preferred_element_type=F32) + cnt_ref[:, :ne]
    for k in range(TOP_K):
        rank = jnp.sum(jnp.where(col == idxs[k], before, 0.0), axis=-1, keepdims=True)
        out = jnp.where(lane == 2 * TOP_K + k, rank, out)
    cnt_ref[:, :ne] = cnt_ref[:, :ne] + jnp.sum(hot, axis=0, keepdims=True)
    r_ref[...] = out


def _ffn_route(x2d, g, modv, layer, row_fn, w_router, b_router, tm):
    t, d = x2d.shape
    ne = w_router.shape[1]
    pw = _piece_width(d // 2)
    spec = pl.BlockSpec((tm, d), lambda i: (i, 0))
    return pl.pallas_call(
        _route_kernel,
        grid=(t // tm,),
        in_specs=[
            spec,
            pl.BlockSpec((1, d), lambda i: (0, 0)),
            _mod_spec(d, 3, row_fn, layer),
            _mod_spec(d, 4, row_fn, layer),
            pl.BlockSpec((d, ne), lambda i: (0, 0)),
            pl.BlockSpec((1, ne), lambda i: (0, 0)),
        ],
        out_specs=[pl.BlockSpec((d // 2 // pw, tm, pw), lambda i: (0, i, 0)),
                   pl.BlockSpec((tm, LANES), lambda i: (i, 0)),
                   pl.BlockSpec((1, LANES), lambda i: (0, 0))],
        out_shape=[jax.ShapeDtypeStruct((d // 2 // pw, t, pw), jnp.uint32),
                   jax.ShapeDtypeStruct((t, LANES), F32),
                   jax.ShapeDtypeStruct((1, LANES), F32)],
        compiler_params=_cparams("arbitrary"),
        name="ffn_norm_route",
    )(x2d, g, modv, modv, w_router, b_router)


def _moe_gather(f_pieces, slot_tok):
    npc, t, pw = f_pieces.shape
    s = slot_tok.shape[0]
    idx = (jnp.arange(npc, dtype=jnp.int32)[:, None] * t + slot_tok[None, :]).reshape(-1)
    return _sc_row_gather(f_pieces.reshape(npc * t, pw), idx).reshape(npc, s, pw)


def _new_expert(be_ref, b):
    return (b == 0) | (be_ref[b] != be_ref[jnp.maximum(b - 1, 0)])


def _moe_up_kernel(be_ref, nu_ref, x_ref, w_ref, bg_ref, bl_ref, perm_ref, o_ref, wg_ref, wl_ref):
    b = pl.program_id(1)

    @pl.when(_new_expert(be_ref, b))
    def _():
        for q in range(w_ref.shape[1] // (2 * LANES)):
            wq = w_ref[:, q * 2 * LANES:(q + 1) * 2 * LANES].astype(BF16)
            r = jnp.dot(wq, perm_ref[...], preferred_element_type=F32)
            wg_ref[:, q * LANES:(q + 1) * LANES] = r[:, :LANES].astype(BF16)
            wl_ref[:, q * LANES:(q + 1) * LANES] = r[:, LANES:].astype(BF16)

    @pl.when(b < nu_ref[0])
    def _():
        x = _unpack_bf16_pairs(_load_pieces(x_ref))
        glu = jnp.dot(x, wg_ref[...], preferred_element_type=F32) + bg_ref[...]
        lin = jnp.dot(x, wl_ref[...], preferred_element_type=F32) + bl_ref[...]
        glu = jnp.minimum(glu, SWIGLU_LIMIT)
        lin = jnp.clip(lin, -SWIGLU_LIMIT, SWIGLU_LIMIT)
        o_ref[...] = (glu * jax.nn.sigmoid(SWIGLU_ALPHA * glu) * (lin + 1.0)).astype(o_ref.dtype)

    @pl.when(b >= nu_ref[0])
    def _():
        o_ref[...] = jnp.zeros_like(o_ref)


def _moe_up(xs, block_expert, n_used, w_up, b_glu, b_lin, perm, layer):
    npc, s, pw = xs.shape
    d = 2 * npc * pw
    nb = s // MOE_ROWS
    de = w_up.shape[3] // 2
    tn = min(1024, de)
    bspec = pl.BlockSpec((None, None, 1, tn), lambda j, b, be, nu: (layer, be[b], 0, j))
    return pl.pallas_call(
        _moe_up_kernel,
        grid_spec=pltpu.PrefetchScalarGridSpec(
            num_scalar_prefetch=2,
            grid=(de // tn, nb),
            in_specs=[pl.BlockSpec((npc, MOE_ROWS, pw), lambda j, b, be, nu: (0, b, 0)),
                      pl.BlockSpec((None, None, d, 2 * tn),
                                   lambda j, b, be, nu: (layer, be[b], 0, j)),
                      bspec, bspec, pl.BlockSpec(perm.shape, lambda j, b, be, nu: (0, 0))],
            out_specs=pl.BlockSpec((MOE_ROWS, tn), lambda j, b, be, nu: (b, j)),
            scratch_shapes=[pltpu.VMEM((d, tn), BF16), pltpu.VMEM((d, tn), BF16)],
        ),
        out_shape=jax.ShapeDtypeStruct((s, de), BF16),
        compiler_params=_cparams("arbitrary", "arbitrary"),
        name="moe_up_swiglu",
    )(block_expert, n_used, xs, w_up, b_glu, b_lin, perm)


def _moe_down_kernel(be_ref, nu_ref, a_ref, w_ref, b_ref, o_ref, wbf_ref):
    b = pl.program_id(1)

    @pl.when(_new_expert(be_ref, b))
    def _():
        wbf_ref[...] = w_ref[...].astype(BF16)

    @pl.when(b < nu_ref[0])
    def _():
        _store_pieces(o_ref, _pack_bf16_pairs(
            jnp.dot(a_ref[...], wbf_ref[...], preferred_element_type=F32) + b_ref[...]))

    @pl.when(b >= nu_ref[0])
    def _():
        o_ref[...] = jnp.zeros_like(o_ref)


def _moe_down(act, block_expert, n_used, w_down, b_down, layer):
    s, de = act.shape
    d = w_down.shape[3]
    nb = s // MOE_ROWS
    tn = d
    pw = _piece_width(d // 2)
    npc = d // 2 // pw
    return pl.pallas_call(
        _moe_down_kernel,
        grid_spec=pltpu.PrefetchScalarGridSpec(
            num_scalar_prefetch=2,
            grid=(d // tn, nb),
            in_specs=[pl.BlockSpec((MOE_ROWS, de), lambda j, b, be, nu: (b, 0)),
                      pl.BlockSpec((None, None, de, tn), lambda j, b, be, nu: (layer, be[b], 0, j)),
                      pl.BlockSpec((None, None, 1, tn), lambda j, b, be, nu: (layer, be[b], 0, j))],
            out_specs=pl.BlockSpec((npc, MOE_ROWS, pw), lambda j, b, be, nu: (0, b, 0)),
            scratch_shapes=[pltpu.VMEM((de, tn), BF16)],
        ),
        out_shape=jax.ShapeDtypeStruct((npc, s, pw), jnp.uint32),
        compiler_params=_cparams("arbitrary", "arbitrary"),
        name="moe_down",
    )(block_expert, n_used, act, w_down, b_down)


def _combine_kernel(y_ref, r_ref, x_ref, g_ref, o_ref):
    route = r_ref[...]
    h = x_ref.shape[1] // 2
    acc_lo = acc_hi = 0.0
    for k in range(TOP_K):
        w = jnp.concatenate([y_ref[c, k] for c in range(y_ref.shape[0])], axis=1)
        p = route[:, TOP_K + k:TOP_K + k + 1]
        acc_lo = acc_lo + p * pltpu.bitcast(w << 16, F32)
        acc_hi = acc_hi + p * pltpu.bitcast(w & jnp.uint32(0xFFFF0000), F32)
    o_ref[:, :h] = x_ref[:, :h] + g_ref[:, :h] * acc_lo
    o_ref[:, h:] = x_ref[:, h:] + g_ref[:, h:] * acc_hi


def _moe_combine(ys, dest, route, x2d, modv, layer, row_fn, tm):
    t, d = x2d.shape
    npc, s, pw = ys.shape
    idx = (jnp.arange(npc, dtype=jnp.int32)[:, None, None] * s
           + dest.reshape(t, TOP_K).T[None, :, :]).reshape(-1)
    rows = _sc_row_gather(ys.reshape(npc * s, pw), idx).reshape(npc, TOP_K, t, pw)
    return pl.pallas_call(
        _combine_kernel,
        grid=(t // tm,),
        in_specs=[
            pl.BlockSpec((npc, TOP_K, tm, pw), lambda i: (0, 0, i, 0)),
            pl.BlockSpec((tm, LANES), lambda i: (i, 0)),
            pl.BlockSpec((tm, d), lambda i: (i, 0)),
            _mod_spec(d, 5, row_fn, layer),
        ],
        out_specs=pl.BlockSpec((tm, d), lambda i: (i, 0)),
        out_shape=jax.ShapeDtypeStruct((t, d), F32),
        compiler_params=_cparams("arbitrary"),
        name="moe_combine",
    )(rows, route, x2d, modv)


def _routing_tables(route, counts_f, ne):
    t = route.shape[0]
    i32 = jnp.int32
    flat_e = route[:, :TOP_K].astype(i32).reshape(-1)
    rank = route[:, 2 * TOP_K:3 * TOP_K].astype(i32).reshape(-1)
    counts = counts_f[0, :ne].astype(i32)
    n_rows = t * TOP_K
    padded = (counts + MOE_ROWS - 1) // MOE_ROWS * MOE_ROWS
    pad_end = jnp.cumsum(padded)
    pad_start = pad_end - padded
    dest = pad_start[flat_e] + rank
    n_blocks = -(-n_rows // MOE_ROWS) + ne
    blk_row0 = jnp.arange(n_blocks, dtype=i32) * MOE_ROWS
    block_expert = jnp.minimum(
        jnp.sum((pad_end[None, :] <= blk_row0[:, None]).astype(i32), axis=1), ne - 1)
    n_used = (pad_end[-1:] // MOE_ROWS).astype(i32)
    shift = max(n_rows - 1, 1).bit_length()
    assert ne << shift < 2 ** 31
    order = jnp.sort(flat_e * (1 << shift) + jnp.arange(n_rows, dtype=i32)) & ((1 << shift) - 1)
    first = jnp.cumsum(counts) - counts
    local = (blk_row0 - pad_start[block_expert])[:, None] + jnp.arange(MOE_ROWS, dtype=i32)[None, :]
    src = jnp.clip(first[block_expert][:, None] + local, 0, n_rows - 1)
    spare = (blk_row0[:, None] + jnp.arange(MOE_ROWS, dtype=i32)[None, :]) % t
    slot_tok = jnp.where(local < counts[block_expert][:, None], order[src] // TOP_K, spare)
    return dest.astype(i32), slot_tok.reshape(-1).astype(i32), block_expert.astype(i32), n_used


def _moe_experts(f_packed, route, counts_f, w_up, b_up, w_down, b_down, layer):
    depth, ne = w_up.shape[:2]
    dest, slot_tok, block_expert, n_used = _routing_tables(route, counts_f, ne)
    b_pairs = b_up.reshape(depth, ne, 1, -1, 2)
    col = jnp.arange(2 * LANES)
    src = jnp.where(col < LANES, 2 * col, 2 * (col - LANES) + 1)
    perm = (jnp.arange(2 * LANES)[:, None] == src[None, :]).astype(BF16)
    xs = _moe_gather(f_packed, slot_tok)
    act = _moe_up(xs, block_expert, n_used, w_up, b_pairs[..., 0], b_pairs[..., 1], perm, layer)
    ys = _moe_down(act, block_expert, n_used, w_down, b_down.reshape(depth, ne, 1, -1), layer)
    return ys, dest


def _mla_proj_kernel(x_ref, g_ref, sh_ref, sc_ref, win_ref, gq_ref, gkv_ref, wq_ref, wkv_ref,
                     q_ref, kv_ref, kpe_ref, *, qp, kvp):
    h = _norm_mod(x_ref[...], g_ref[...], sh_ref[...], sc_ref[...]).astype(BF16)
    a = jnp.dot(h, win_ref[...], preferred_element_type=F32)
    cq = a[:, :qp]
    ckv = a[:, qp:qp + kvp]
    kpe_ref[...] = a[:, qp + kvp:]
    cqn = cq * lax.rsqrt(jnp.sum(cq * cq, axis=-1, keepdims=True) * (1.0 / Q_LORA_RANK) + NORM_EPS)
    ckvn = ckv * lax.rsqrt(jnp.sum(ckv * ckv, axis=-1, keepdims=True) * (1.0 / KV_LORA_RANK)
                           + NORM_EPS)
    q_ref[...] = jnp.dot((cqn * gq_ref[...]).astype(BF16), wq_ref[...], preferred_element_type=F32)
    kv_ref[...] = jnp.dot((ckvn * gkv_ref[...]).astype(BF16), wkv_ref[...],
                          preferred_element_type=F32)


def _mla_proj(x2d, g, modv, layer, row_fn, win_p, gq_p, gkv_p, wq_p, wkv_p, tm):
    t, d = x2d.shape
    qp, kvp = gq_p.shape[1], gkv_p.shape[1]
    na = win_p.shape[1]
    nq, nkv = wq_p.shape[1], wkv_p.shape[1]
    full = lambda arr: pl.BlockSpec(arr.shape, lambda i: (0, 0))
    return pl.pallas_call(
        functools.partial(_mla_proj_kernel, qp=qp, kvp=kvp),
        grid=(t // tm,),
        in_specs=[
            pl.BlockSpec((tm, d), lambda i: (i, 0)),
            pl.BlockSpec((1, d), lambda i: (0, 0)),
            _mod_spec(d, 0, row_fn, layer),
            _mod_spec(d, 1, row_fn, layer),
            full(win_p), full(gq_p), full(gkv_p), full(wq_p), full(wkv_p),
        ],
        out_specs=[pl.BlockSpec((tm, nq), lambda i: (i, 0)),
                   pl.BlockSpec((tm, nkv), lambda i: (i, 0)),
                   pl.BlockSpec((tm, na - qp - kvp), lambda i: (i, 0))],
        out_shape=[jax.ShapeDtypeStruct((t, nq), F32), jax.ShapeDtypeStruct((t, nkv), F32),
                   jax.ShapeDtypeStruct((t, na - qp - kvp), F32)],
        compiler_params=_cparams("arbitrary"),
        name="mla_projections",
    )(x2d, g, modv, modv, win_p, gq_p, gkv_p, wq_p, wkv_p)


def _rope(x, cos, sin_lo, sin_hi):
    quarter = QK_ROPE_DIM // 4
    return (x * cos + pltpu.roll(x, LANES - quarter, 1) * sin_lo
            + pltpu.roll(x, quarter, 1) * sin_hi)


def _headnorm_kernel(q_ref, kv_ref, kpe_ref, gq_ref, gk_ref, cos_ref, slo_ref, shi_ref,
                     qo_ref, ko_ref, vo_ref, *, heads, q_scale):
    inv = 1.0 / (QK_NOPE_DIM + QK_ROPE_DIM)
    cos, slo, shi = cos_ref[...], slo_ref[...], shi_ref[...]
    kpe = kpe_ref[...]
    kpe_ss = jnp.sum(kpe * kpe, axis=-1, keepdims=True)
    gq = gq_ref[...]
    gk = gk_ref[...]
    for h in range(heads):
        qh = q_ref[:, h * HEAD_SLOT:(h + 1) * HEAD_SLOT]
        rs = lax.rsqrt(jnp.sum(qh * qh, axis=-1, keepdims=True) * inv + NORM_EPS) * q_scale
        qn = qh * rs * gq
        qo_ref[h, :, :LANES] = qn[:, :LANES].astype(BF16)
        qo_ref[h, :, LANES:] = _rope(qn[:, LANES:], cos, slo, shi).astype(BF16)
        kn = kv_ref[:, h * HEAD_SLOT:h * HEAD_SLOT + QK_NOPE_DIM]
        rk = lax.rsqrt((jnp.sum(kn * kn, axis=-1, keepdims=True) + kpe_ss) * inv + NORM_EPS)
        ko_ref[h, :, :LANES] = (kn * rk * gk[:, :LANES]).astype(BF16)
        ko_ref[h, :, LANES:] = _rope(kpe * rk * gk[:, LANES:], cos, slo, shi).astype(BF16)
        vo_ref[h] = kv_ref[:, h * HEAD_SLOT + QK_NOPE_DIM:(h + 1) * HEAD_SLOT].astype(BF16)


def _head_norm_rope(q_raw, kv_raw, kpe, gq_slot, gk_slot, cos, slo, shi, bsz, n_ctx, n_lat, tm):
    heads = MLA_HEADS
    n_keys = n_ctx + n_lat
    ctx_tiles = bsz * n_ctx // tm
    per_ctx = n_ctx // tm
    per_lat = n_lat // tm

    def batch_of(i):
        return jnp.where(i < ctx_tiles, i // per_ctx, (i - ctx_tiles) // per_lat)

    def key_blk(i):
        return jnp.where(i < ctx_tiles, i % per_ctx, per_ctx + (i - ctx_tiles) % per_lat)

    tab = pl.BlockSpec((tm, LANES), lambda i: (key_blk(i), 0))
    t = q_raw.shape[0]
    out_map = lambda i: (batch_of(i), 0, key_blk(i), 0)

    def q_map(i):
        blk = jnp.where(i < ctx_tiles, per_lat + i % per_ctx, (i - ctx_tiles) % per_lat)
        return (batch_of(i), 0, blk, 0)

    return pl.pallas_call(
        functools.partial(_headnorm_kernel, heads=heads,
                          q_scale=float(QK_NOPE_DIM + QK_ROPE_DIM) ** -0.5),
        grid=(t // tm,),
        in_specs=[
            pl.BlockSpec((tm, heads * HEAD_SLOT), lambda i: (i, 0)),
            pl.BlockSpec((tm, heads * HEAD_SLOT), lambda i: (i, 0)),
            pl.BlockSpec((tm, LANES), lambda i: (i, 0)),
            pl.BlockSpec((1, HEAD_SLOT), lambda i: (0, 0)),
            pl.BlockSpec((1, HEAD_SLOT), lambda i: (0, 0)),
            tab, tab, tab,
        ],
        out_specs=[pl.BlockSpec((None, heads, tm, HEAD_SLOT), q_map),
                   pl.BlockSpec((None, heads, tm, HEAD_SLOT), out_map),
                   pl.BlockSpec((None, heads, tm, V_HEAD_DIM), out_map)],
        out_shape=[jax.ShapeDtypeStruct((bsz, heads, n_keys, HEAD_SLOT), BF16),
                   jax.ShapeDtypeStruct((bsz, heads, n_keys, HEAD_SLOT), BF16),
                   jax.ShapeDtypeStruct((bsz, heads, n_keys, V_HEAD_DIM), BF16)],
        compiler_params=_cparams("arbitrary"),
        name="mla_headnorm_rope",
    )(q_raw, kv_raw, kpe, gq_slot, gk_slot, cos, slo, shi)


def _attn_kernel(q_ref, k_ref, v_ref, o_ref, *, sub):
    for r in range(q_ref.shape[0] // sub):
        rows = pl.ds(r * sub, sub)
        s = lax.dot_general(q_ref[rows, :], k_ref[...], (((1,), (1,)), ((), ())),
                            preferred_element_type=F32)
        m = jnp.max(s, axis=-1, keepdims=True)
        p = jnp.exp(s - m)
        l = jnp.sum(p, axis=-1, keepdims=True)
        o = jnp.dot(p.astype(BF16), v_ref[...], preferred_element_type=F32)
        o_ref[rows, :] = (o / l).astype(o_ref.dtype)


def _attention(q, k, v, n_ctx, tq):
    bsz, heads, n_keys, _ = k.shape
    n_lat = n_keys - n_ctx
    return pl.pallas_call(
        functools.partial(_attn_kernel, sub=min(256, tq)),
        grid=(bsz, heads, n_lat // tq),
        in_specs=[
            pl.BlockSpec((None, None, tq, HEAD_SLOT), lambda b, h, i: (b, h, i, 0)),
            pl.BlockSpec((None, None, n_keys, HEAD_SLOT), lambda b, h, i: (b, h, 0, 0)),
            pl.BlockSpec((None, None, n_keys, V_HEAD_DIM), lambda b, h, i: (b, h, 0, 0)),
        ],
        out_specs=pl.BlockSpec((None, tq, V_HEAD_DIM), lambda b, h, i: (b, i, h)),
        out_shape=jax.ShapeDtypeStruct((bsz, n_lat, heads * V_HEAD_DIM), BF16),
        compiler_params=_cparams("arbitrary", "arbitrary", "arbitrary"),
        name="mla_attention",
    )(q, k, v)


def _dft_tables(n):
    k = jnp.arange(n, dtype=jnp.int32)
    ang = ((k[:, None] * k[None, :]) % n).astype(F32) * (2.0 * math.pi / n)
    return jnp.cos(ang), jnp.sin(ang)


def _rope_tables(n_ctx, n_lat):
    rows = n_lat // GRID_W
    pairs = QK_ROPE_DIM // 4
    row = jnp.repeat(jnp.arange(rows, dtype=F32), GRID_W)
    col = jnp.tile(jnp.arange(GRID_W, dtype=F32), rows)
    inv = ROPE_THETA ** (-jnp.arange(pairs, dtype=F32) / pairs)
    ang = jnp.stack([row[:, None] * inv, col[:, None] * inv], axis=1)
    cos, sin = jnp.cos(ang), jnp.sin(ang)
    zero = jnp.zeros_like(sin)
    cos_t = jnp.stack([cos, cos], axis=2).reshape(n_lat, QK_ROPE_DIM)
    slo_t = jnp.stack([-sin, zero], axis=2).reshape(n_lat, QK_ROPE_DIM)
    shi_t = jnp.stack([zero, sin], axis=2).reshape(n_lat, QK_ROPE_DIM)

    def full(tab, ctx_val):
        tab = jnp.pad(tab, ((0, 0), (0, LANES - QK_ROPE_DIM)))
        return jnp.concatenate([jnp.full((n_ctx, LANES), ctx_val, F32), tab], axis=0)

    return full(cos_t, 1.0), full(slo_t, 0.0), full(shi_t, 0.0)


def _pad_cols(w, n):
    return jnp.pad(w, ((0, 0), (0, n - w.shape[1])))


def _mla_weights(w_in, g_q_lora, w_q_up, g_kv_lora, w_kv_up, g_q_head, g_k_head):
    qp = -(-Q_LORA_RANK // LANES) * LANES
    kvp = -(-KV_LORA_RANK // LANES) * LANES
    qr, kvr = Q_LORA_RANK, KV_LORA_RANK
    win_p = jnp.concatenate([_pad_cols(w_in[:, :qr], qp), _pad_cols(w_in[:, qr:qr + kvr], kvp),
                             _pad_cols(w_in[:, qr + kvr:], LANES)], axis=1).astype(BF16)
    gq_p = _pad_cols(g_q_lora[None, :], qp)
    gkv_p = _pad_cols(g_kv_lora[None, :], kvp)
    hd = QK_NOPE_DIM + QK_ROPE_DIM
    wq = w_q_up.reshape(qr, MLA_HEADS, hd)
    wq = jnp.pad(wq, ((0, qp - qr), (0, 0), (0, HEAD_SLOT - hd)))
    wq_p = wq.reshape(qp, MLA_HEADS * HEAD_SLOT).astype(BF16)
    wkv_p = jnp.pad(w_kv_up, ((0, kvp - kvr), (0, 0))).astype(BF16)
    gq_slot = _pad_cols(g_q_head[None, :], HEAD_SLOT)
    gk_slot = _pad_cols(g_k_head[None, :], HEAD_SLOT)
    return win_p, gq_p, gkv_p, wq_p, wkv_p, gq_slot, gk_slot


def kernel(x, c, ctx, c_ctx, w_mod, b_mod, g_mix, g_ffn, fourier_w_out, fourier_b_out, mla_w_in, mla_g_q_lora, mla_w_q_up, mla_g_kv_lora, mla_w_kv_up, mla_g_q_head, mla_g_k_head, mla_w_out, router_w, router_b, expert_w_up, expert_b_up, expert_w_down, expert_b_down):
    bsz, n, d = x.shape
    l = ctx.shape[1]
    assert bsz < 8 and n % GRID_W == 0
    tl = min(256, l)
    assert l % tl == 0 and n % tl == 0
    ctx_row = bsz

    c8 = jnp.zeros((8, d), F32).at[:bsz].set(c).at[ctx_row].set(c_ctx)
    modv = _mod_vectors(c8, w_mod, b_mod).reshape(w_mod.shape[0], 8, 1, 6 * d)

    def lat_row(tm):
        return lambda i: i // (n // tm)

    def all_row(tm):
        nct = bsz * l // tm
        return lambda i: jnp.where(i < nct, ctx_row, (i - nct) // (n // tm))

    dg = d // FOURIER_GROUPS
    wf = fourier_w_out[0].astype(BF16)
    bf = fourier_b_out[0][None, :]
    g0 = g_mix[0][None, :]

    def fourier(x3, row_fn):
        nn = x3.shape[1]
        n1, n2, m_tab, w2, cs = _fourier_tables(nn, dg)
        tr, ti = _fft_stage1(x3, g0, modv, 0, row_fn, m_tab, n1, n2)
        y = _fft_stage2(tr, ti, w2, cs, 1.0 / math.sqrt(nn * dg))
        return _fft_out(y, wf, bf, x3, modv, 0, row_fn, n1, n2)

    tm_lat = min(512, n)
    x_lat1 = fourier(x, lambda b, j: b)
    x_ctx1 = fourier(ctx, lambda b, j: ctx_row)
    x_all = jnp.concatenate([x_ctx1, x_lat1], axis=0)
    n_ctx_tok = bsz * l

    f0, route0, cnt0 = _ffn_route(x_all, g_ffn[0][None, :], modv, 0, all_row(tl), router_w[0],
                                  router_b[0][None, :], tl)
    ys0, dest0 = _moe_experts(f0, route0, cnt0, expert_w_up, expert_b_up, expert_w_down,
                              expert_b_down, 0)
    tc = tl
    x_all2 = _moe_combine(ys0, dest0, route0, x_all, modv, 0, all_row(tc), tc)

    win_p, gq_p, gkv_p, wq_p, wkv_p, gq_slot, gk_slot = _mla_weights(
        mla_w_in[0], mla_g_q_lora[0], mla_w_q_up[0], mla_g_kv_lora[0], mla_w_kv_up[0],
        mla_g_q_head[0], mla_g_k_head[0])
    q_raw, kv_raw, kpe = _mla_proj(x_all2, g_mix[1][None, :], modv, 1, all_row(tl),
                                   win_p, gq_p, gkv_p, wq_p, wkv_p, tl)
    cos, slo, shi = _rope_tables(l, n)
    qh, kh, vh = _head_norm_rope(q_raw, kv_raw, kpe, gq_slot, gk_slot, cos, slo, shi,
                                 bsz, l, n, tl)
    attn = _attention(qh, kh, vh, l, min(1024, n))
    zero_bias = jnp.zeros((1, d), F32)
    tm_out = math.gcd(tm_lat, n_ctx_tok)
    x_lat3 = _mm_residual(attn.reshape(bsz * n, -1), mla_w_out[0].astype(BF16), zero_bias,
                          x_all2, n_ctx_tok, modv, 1, 2, lat_row(tm_out), tm_out)

    f1, route1, cnt1 = _ffn_route(x_lat3, g_ffn[1][None, :], modv, 1, lat_row(tl), router_w[1],
                                  router_b[1][None, :], tl)
    ys1, dest1 = _moe_experts(f1, route1, cnt1, expert_w_up, expert_b_up, expert_w_down,
                              expert_b_down, 1)
    out = _moe_combine(ys1, dest1, route1, x_lat3, modv, 1, lat_row(tc), tc)
    return out.reshape(bsz, n, d)
```

```python
import functools
import math

import jax
import jax.numpy as jnp
from jax import lax
from jax.experimental import pallas as pl
from jax.experimental.pallas import tpu as pltpu
from jax.experimental.pallas import tpu_sc as plsc

GRID_W = 64
FOURIER_GROUPS = 4
MLA_HEADS = 16
Q_LORA_RANK = 448
KV_LORA_RANK = 512
QK_NOPE_DIM = 128
QK_ROPE_DIM = 64
V_HEAD_DIM = 128
ROPE_THETA = 10000.0
TOP_K = 4
SWIGLU_ALPHA = 1.702
SWIGLU_LIMIT = 7.0
NORM_EPS = 1e-6

LANES = 128
MOE_ROWS = 256
HEAD_SLOT = 256
SC_PIECE_WORDS = 256
SC_WINDOW = 128
VMEM_LIMIT = 56 << 20

F32 = jnp.float32
BF16 = jnp.bfloat16
HIGHEST = lax.Precision.HIGHEST


def _cparams(*sem, vmem=VMEM_LIMIT):
    return pltpu.CompilerParams(dimension_semantics=sem, vmem_limit_bytes=vmem)


def _norm_mod(x, g, sh, sc):
    ms = jnp.mean(x * x, axis=-1, keepdims=True)
    return (x * lax.rsqrt(ms + NORM_EPS)) * g * (1.0 + sc) + sh


def _pack_bf16_pairs(x):
    h = x.shape[1] // 2
    lo = pltpu.bitcast(x[:, :h].astype(BF16).astype(F32), jnp.uint32)
    hi = pltpu.bitcast(x[:, h:].astype(BF16).astype(F32), jnp.uint32)
    return hi | (lo >> 16)


def _unpack_bf16_pairs(w):
    lo = pltpu.bitcast(w << 16, F32).astype(BF16)
    hi = pltpu.bitcast(w & jnp.uint32(0xFFFF0000), F32).astype(BF16)
    return jnp.concatenate([lo, hi], axis=1)


def _piece_width(words):
    return min(SC_PIECE_WORDS, words)


def _store_pieces(ref, words):
    for c in range(ref.shape[0]):
        ref[c] = words[:, c * ref.shape[2]:(c + 1) * ref.shape[2]]


def _load_pieces(ref):
    return jnp.concatenate([ref[c] for c in range(ref.shape[0])], axis=1)


def _sc_row_gather(table, idx):
    n = idx.shape[0]
    pw = table.shape[1]
    assert n % SC_WINDOW == 0
    mesh = plsc.VectorSubcoreMesh(core_axis_name="c", subcore_axis_name="s")

    @pl.kernel(out_type=jax.ShapeDtypeStruct((n, pw), table.dtype), mesh=mesh)
    def gather(t_hbm, i_hbm, o_hbm):
        def body(i_vmem, o_vmem):
            pltpu.sync_copy(t_hbm.at[i_vmem.at[0]], o_vmem)

        pltpu.emit_pipeline(
            body,
            grid=(n // SC_WINDOW,),
            in_specs=[pl.BlockSpec((1, SC_WINDOW), index_map=lambda i: (0, i))],
            out_specs=[pl.BlockSpec((SC_WINDOW, pw), index_map=lambda i: (i, 0))],
            core_axis_name=("c", "s"),
            dimension_semantics=(pltpu.PARALLEL,),
        )(i_hbm, o_hbm)

    return gather(table, idx.reshape(1, n))


def _mod_spec(d, piece, row_fn, layer):
    return pl.BlockSpec((None, None, 1, d), lambda *ids: (layer, row_fn(*ids), 0, piece))


def _mod_kernel(c_ref, w_ref, b_ref, o_ref):
    c = c_ref[...]
    a = c * jax.nn.sigmoid(c)
    o_ref[...] = jnp.dot(a, w_ref[...], preferred_element_type=F32, precision=HIGHEST) + b_ref[...]


def _mod_vectors(c8, w_mod, b_mod):
    depth, d, n6 = w_mod.shape
    tn = next(t for t in (1024, 512, 256, 128) if n6 % t == 0)
    return pl.pallas_call(
        _mod_kernel,
        grid=(depth, n6 // tn),
        in_specs=[
            pl.BlockSpec((8, d), lambda l, j: (0, 0)),
            pl.BlockSpec((None, d, tn), lambda l, j: (l, 0, j)),
            pl.BlockSpec((None, 1, tn), lambda l, j: (l, 0, j)),
        ],
        out_specs=pl.BlockSpec((None, 8, tn), lambda l, j: (l, 0, j)),
        out_shape=jax.ShapeDtypeStruct((depth, 8, n6), F32),
        compiler_params=_cparams("arbitrary", "arbitrary"),
        name="mod_vectors",
    )(c8, w_mod, b_mod.reshape(depth, 1, n6))


ROW_CHUNK = 8


def _fft1_kernel(x_ref, g_ref, sh_ref, sc_ref, m_ref, tr_ref, ti_ref):
    n1 = x_ref.shape[0]
    g, sh, sc = g_ref[...], sh_ref[...], sc_ref[...]
    for c in range(x_ref.shape[1]):
        h = _norm_mod(x_ref[:, c, :], g, sh, sc).astype(BF16)
        t = jnp.dot(m_ref[c], h, preferred_element_type=F32)
        tr_ref[:, c, :] = t[:n1]
        ti_ref[:, c, :] = t[n1:]


def _fft_stage1(x3, g, modv, layer, row_fn, m_tab, n1, n2):
    bsz, n, d = x3.shape
    x4 = x3.reshape(bsz, n1, n2, d)
    blk = pl.BlockSpec((None, n1, ROW_CHUNK, d), lambda b, j: (b, 0, j, 0))
    return pl.pallas_call(
        _fft1_kernel,
        grid=(bsz, n2 // ROW_CHUNK),
        in_specs=[
            blk,
            pl.BlockSpec((1, d), lambda b, j: (0, 0)),
            _mod_spec(d, 0, row_fn, layer),
            _mod_spec(d, 1, row_fn, layer),
            pl.BlockSpec((ROW_CHUNK, 2 * n1, n1), lambda b, j: (j, 0, 0)),
        ],
        out_specs=[blk, blk],
        out_shape=[jax.ShapeDtypeStruct((bsz, n1, n2, d), F32)] * 2,
        compiler_params=_cparams("arbitrary", "arbitrary"),
        name="fourier_stage1",
    )(x4, g, modv, modv, m_tab)


def _fft2_kernel(tr_ref, ti_ref, w2_ref, cs_ref, y_ref, *, groups, scale):
    ck, n2, d = tr_ref.shape
    dg = d // groups
    xr, xi = [], []
    for kk in range(ck):
        t = jnp.concatenate([tr_ref[kk], ti_ref[kk]], axis=0).astype(BF16)
        xx = jnp.dot(w2_ref[...], t, preferred_element_type=F32)
        xr.append(xx[:n2])
        xi.append(xx[n2:])
    xr = jnp.concatenate(xr, axis=0).astype(BF16)
    xi = jnp.concatenate(xi, axis=0).astype(BF16)
    for gi in range(groups):
        cols = slice(gi * dg, (gi + 1) * dg)
        y = (jnp.dot(xr[:, cols], cs_ref[0], preferred_element_type=F32)
             + jnp.dot(xi[:, cols], cs_ref[1], preferred_element_type=F32))
        y_ref[:, cols] = (y * scale).astype(y_ref.dtype)


def _fft_stage2(tr, ti, w2, cs, scale):
    bsz, n1, n2, d = tr.shape
    ck = ROW_CHUNK
    blk = pl.BlockSpec((None, ck, n2, d), lambda b, j: (b, j, 0, 0))
    return pl.pallas_call(
        functools.partial(_fft2_kernel, groups=FOURIER_GROUPS, scale=scale),
        grid=(bsz, n1 // ck),
        in_specs=[blk, blk,
                  pl.BlockSpec(w2.shape, lambda b, j: (0, 0)),
                  pl.BlockSpec(cs.shape, lambda b, j: (0, 0, 0))],
        out_specs=pl.BlockSpec((None, ck * n2, d), lambda b, j: (b, j, 0)),
        out_shape=jax.ShapeDtypeStruct((bsz, n1 * n2, d), BF16),
        compiler_params=_cparams("arbitrary", "arbitrary"),
        name="fourier_stage2",
    )(tr, ti, w2, cs)


def _fft_out_kernel(y_ref, w_ref, b_ref, x_ref, g_ref, o_ref):
    n2, ck, _ = x_ref.shape
    out = jnp.dot(y_ref[...], w_ref[...], preferred_element_type=F32) + b_ref[...]
    gate = g_ref[...]
    for kk in range(ck):
        o_ref[:, kk, :] = x_ref[:, kk, :] + gate * out[kk * n2:(kk + 1) * n2]


def _fft_out(y, w_bf, bias, x3, modv, layer, row_fn, n1, n2):
    bsz, n, d = x3.shape
    ck = ROW_CHUNK
    x4 = x3.reshape(bsz, n2, n1, d)
    blk = pl.BlockSpec((None, n2, ck, d), lambda b, j: (b, 0, j, 0))
    out = pl.pallas_call(
        _fft_out_kernel,
        grid=(bsz, n1 // ck),
        in_specs=[
            pl.BlockSpec((None, ck * n2, d), lambda b, j: (b, j, 0)),
            pl.BlockSpec(w_bf.shape, lambda b, j: (0, 0)),
            pl.BlockSpec((1, d), lambda b, j: (0, 0)),
            blk,
            _mod_spec(d, 2, row_fn, layer),
        ],
        out_specs=blk,
        out_shape=jax.ShapeDtypeStruct((bsz, n2, n1, d), F32),
        compiler_params=_cparams("arbitrary", "arbitrary"),
        name="fourier_out_residual",
    )(y, w_bf, bias, x4, modv)
    return out.reshape(bsz * n, d)


def _fourier_tables(n, dg):
    assert n & (n - 1) == 0
    n2 = 1 << ((n.bit_length() - 1) // 2)
    n1 = n // n2
    assert n1 % ROW_CHUNK == 0 and n2 % ROW_CHUNK == 0
    i32 = jnp.int32
    k1 = jnp.arange(n1, dtype=i32)[None, :, None]
    pos = n2 * jnp.arange(n1, dtype=i32)[None, None, :] + jnp.arange(n2, dtype=i32)[:, None, None]
    ang1 = ((k1 * pos) % n).astype(F32) * (2.0 * math.pi / n)
    m_tab = jnp.concatenate([jnp.cos(ang1), -jnp.sin(ang1)], axis=1).astype(BF16)
    c2, s2 = _dft_tables(n2)
    w2 = jnp.concatenate([jnp.concatenate([c2, s2], axis=1),
                          jnp.concatenate([-s2, c2], axis=1)], axis=0).astype(BF16)
    cc, sc = _dft_tables(dg)
    cs = jnp.stack([cc, sc]).astype(BF16)
    return n1, n2, m_tab, w2, cs


def _mm_res_kernel(y_ref, w_ref, b_ref, r_ref, g_ref, o_ref):
    acc = jnp.dot(y_ref[...], w_ref[...], preferred_element_type=F32)
    o_ref[...] = r_ref[...] + g_ref[...] * (acc + b_ref[...])


def _mm_residual(y2d, w_bf, bias, res2d, res_row0, modv, layer, piece, row_fn, tm):
    t, k = y2d.shape
    d = w_bf.shape[1]
    tn = min(1024, d)
    nj = d // tn
    off = res_row0 // tm
    assert res_row0 % tm == 0
    return pl.pallas_call(
        _mm_res_kernel,
        grid=(nj, t // tm),
        in_specs=[
            pl.BlockSpec((tm, k), lambda j, i: (i, 0)),
            pl.BlockSpec((k, tn), lambda j, i: (0, j)),
            pl.BlockSpec((1, tn), lambda j, i: (0, j)),
            pl.BlockSpec((tm, tn), lambda j, i: (i + off, j)),
            pl.BlockSpec((None, None, 1, tn), lambda j, i: (layer, row_fn(i), 0, piece * nj + j)),
        ],
        out_specs=pl.BlockSpec((tm, tn), lambda j, i: (i, j)),
        out_shape=jax.ShapeDtypeStruct((t, d), F32),
        compiler_params=_cparams("arbitrary", "arbitrary"),
        name="mixer_out_residual",
    )(y2d, w_bf, bias, res2d, modv)


def _route_kernel(x_ref, g_ref, sh_ref, sc_ref, wr_ref, br_ref, f_ref, r_ref, cnt_ref):
    @pl.when(pl.program_id(0) == 0)
    def _():
        cnt_ref[...] = jnp.zeros_like(cnt_ref)

    f = _norm_mod(x_ref[...], g_ref[...], sh_ref[...], sc_ref[...])
    _store_pieces(f_ref, _pack_bf16_pairs(f))
    logits = jnp.dot(f, wr_ref[...], preferred_element_type=F32, precision=HIGHEST) + br_ref[...]
    tm, ne = logits.shape
    col = lax.broadcasted_iota(jnp.int32, (tm, ne), 1).astype(F32)
    lane = lax.broadcasted_iota(jnp.int32, (tm, LANES), 1)
    out = jnp.zeros((tm, LANES), F32)
    vals, idxs = [], []
    hot = jnp.zeros((tm, ne), F32)
    for k in range(TOP_K):
        m = jnp.max(logits, axis=-1, keepdims=True)
        idx = jnp.min(jnp.where(logits == m, col, float(ne)), axis=-1, keepdims=True)
        logits = jnp.where(col == idx, -jnp.inf, logits)
        hot = jnp.where(col == idx, 1.0, hot)
        out = jnp.where(lane == k, idx, out)
        vals.append(m)
        idxs.append(idx)
    es = [jnp.exp(v - vals[0]) for v in vals]
    den = es[0]
    for e in es[1:]:
        den = den + e
    for k in range(TOP_K):
        out = jnp.where(lane == TOP_K + k, es[k] / den, out)
    earlier = (lax.broadcasted_iota(jnp.int32, (tm, tm), 1)
               < lax.broadcasted_iota(jnp.int32, (tm, tm), 0))
    before = jnp.dot(jnp.where(earlier, 1.0, 0.0).astype(BF16), hot.astype(BF16),
                     preferred_element_type=F32) + cnt_ref[:, :ne]
    for k in range(TOP_K):
        rank = jnp.sum(jnp.where(col == idxs[k], before, 0.0), axis=-1, keepdims=True)
        out = jnp.where(lane == 2 * TOP_K + k, rank, out)
    cnt_ref[:, :ne] = cnt_ref[:, :ne] + jnp.sum(hot, axis=0, keepdims=True)
    r_ref[...] = out


def _ffn_route(x2d, g, modv, layer, row_fn, w_router, b_router, tm):
    t, d = x2d.shape
    ne = w_router.shape[1]
    pw = _piece_width(d // 2)
    spec = pl.BlockSpec((tm, d), lambda i: (i, 0))
    return pl.pallas_call(
        _route_kernel,
        grid=(t // tm,),
        in_specs=[
            spec,
            pl.BlockSpec((1, d), lambda i: (0, 0)),
            _mod_spec(d, 3, row_fn, layer),
            _mod_spec(d, 4, row_fn, layer),
            pl.BlockSpec((d, ne), lambda i: (0, 0)),
            pl.BlockSpec((1, ne), lambda i: (0, 0)),
        ],
        out_specs=[pl.BlockSpec((d // 2 // pw, tm, pw), lambda i: (0, i, 0)),
                   pl.BlockSpec((tm, LANES), lambda i: (i, 0)),
                   pl.BlockSpec((1, LANES), lambda i: (0, 0))],
        out_shape=[jax.ShapeDtypeStruct((d // 2 // pw, t, pw), jnp.uint32),
                   jax.ShapeDtypeStruct((t, LANES), F32),
                   jax.ShapeDtypeStruct((1, LANES), F32)],
        compiler_params=_cparams("arbitrary"),
        name="ffn_norm_route",
    )(x2d, g, modv, modv, w_router, b_router)


def _moe_gather(f_pieces, slot_tok):
    npc, t, pw = f_pieces.shape
    s = slot_tok.shape[0]
    idx = (jnp.arange(npc, dtype=jnp.int32)[:, None] * t + slot_tok[None, :]).reshape(-1)
    return _sc_row_gather(f_pieces.reshape(npc * t, pw), idx).reshape(npc, s, pw)


def _new_expert(be_ref, b):
    return (b == 0) | (be_ref[b] != be_ref[jnp.maximum(b - 1, 0)])


def _moe_up_kernel(be_ref, nu_ref, x_ref, w_ref, bg_ref, bl_ref, perm_ref, o_ref, wg_ref, wl_ref):
    b = pl.program_id(1)

    @pl.when(_new_expert(be_ref, b))
    def _():
        for q in range(w_ref.shape[1] // (2 * LANES)):
            wq = w_ref[:, q * 2 * LANES:(q + 1) * 2 * LANES].astype(BF16)
            r = jnp.dot(wq, perm_ref[...], preferred_element_type=F32)
            wg_ref[:, q * LANES:(q + 1) * LANES] = r[:, :LANES].astype(BF16)
            wl_ref[:, q * LANES:(q + 1) * LANES] = r[:, LANES:].astype(BF16)

    @pl.when(b < nu_ref[0])
    def _():
        x = _unpack_bf16_pairs(_load_pieces(x_ref))
        glu = jnp.dot(x, wg_ref[...], preferred_element_type=F32) + bg_ref[...]
        lin = jnp.dot(x, wl_ref[...], preferred_element_type=F32) + bl_ref[...]
        glu = jnp.minimum(glu, SWIGLU_LIMIT)
        lin = jnp.clip(lin, -SWIGLU_LIMIT, SWIGLU_LIMIT)
        o_ref[...] = (glu * jax.nn.sigmoid(SWIGLU_ALPHA * glu) * (lin + 1.0)).astype(o_ref.dtype)

    @pl.when(b >= nu_ref[0])
    def _():
        o_ref[...] = jnp.zeros_like(o_ref)


def _moe_up(xs, block_expert, n_used, w_up, b_glu, b_lin, perm, layer):
    npc, s, pw = xs.shape
    d = 2 * npc * pw
    nb = s // MOE_ROWS
    de = w_up.shape[3] // 2
    tn = min(1024, de)
    bspec = pl.BlockSpec((None, None, 1, tn), lambda j, b, be, nu: (layer, be[b], 0, j))
    return pl.pallas_call(
        _moe_up_kernel,
        grid_spec=pltpu.PrefetchScalarGridSpec(
            num_scalar_prefetch=2,
            grid=(de // tn, nb),
            in_specs=[pl.BlockSpec((npc, MOE_ROWS, pw), lambda j, b, be, nu: (0, b, 0)),
                      pl.BlockSpec((None, None, d, 2 * tn),
                                   lambda j, b, be, nu: (layer, be[b], 0, j)),
                      bspec, bspec, pl.BlockSpec(perm.shape, lambda j, b, be, nu: (0, 0))],
            out_specs=pl.BlockSpec((MOE_ROWS, tn), lambda j, b, be, nu: (b, j)),
            scratch_shapes=[pltpu.VMEM((d, tn), BF16), pltpu.VMEM((d, tn), BF16)],
        ),
        out_shape=jax.ShapeDtypeStruct((s, de), BF16),
        compiler_params=_cparams("arbitrary", "arbitrary"),
        name="moe_up_swiglu",
    )(block_expert, n_used, xs, w_up, b_glu, b_lin, perm)


def _moe_down_kernel(be_ref, nu_ref, a_ref, w_ref, b_ref, o_ref, wbf_ref):
    b = pl.program_id(1)

    @pl.when(_new_expert(be_ref, b))
    def _():
        wbf_ref[...] = w_ref[...].astype(BF16)

    @pl.when(b < nu_ref[0])
    def _():
        _store_pieces(o_ref, _pack_bf16_pairs(
            jnp.dot(a_ref[...], wbf_ref[...], preferred_element_type=F32) + b_ref[...]))

    @pl.when(b >= nu_ref[0])
    def _():
        o_ref[...] = jnp.zeros_like(o_ref)


def _moe_down(act, block_expert, n_used, w_down, b_down, layer):
    s, de = act.shape
    d = w_down.shape[3]
    nb = s // MOE_ROWS
    tn = d
    pw = _piece_width(d // 2)
    npc = d // 2 // pw
    return pl.pallas_call(
        _moe_down_kernel,
        grid_spec=pltpu.PrefetchScalarGridSpec(
            num_scalar_prefetch=2,
            grid=(d // tn, nb),
            in_specs=[pl.BlockSpec((MOE_ROWS, de), lambda j, b, be, nu: (b, 0)),
                      pl.BlockSpec((None, None, de, tn), lambda j, b, be, nu: (layer, be[b], 0, j)),
                      pl.BlockSpec((None, None, 1, tn), lambda j, b, be, nu: (layer, be[b], 0, j))],
            out_specs=pl.BlockSpec((npc, MOE_ROWS, pw), lambda j, b, be, nu: (0, b, 0)),
            scratch_shapes=[pltpu.VMEM((de, tn), BF16)],
        ),
        out_shape=jax.ShapeDtypeStruct((npc, s, pw), jnp.uint32),
        compiler_params=_cparams("arbitrary", "arbitrary"),
        name="moe_down",
    )(block_expert, n_used, act, w_down, b_down)


def _combine_kernel(y_ref, r_ref, x_ref, g_ref, o_ref):
    route = r_ref[...]
    h = x_ref.shape[1] // 2
    acc_lo = acc_hi = 0.0
    for k in range(TOP_K):
        w = jnp.concatenate([y_ref[c, k] for c in range(y_ref.shape[0])], axis=1)
        p = route[:, TOP_K + k:TOP_K + k + 1]
        acc_lo = acc_lo + p * pltpu.bitcast(w << 16, F32)
        acc_hi = acc_hi + p * pltpu.bitcast(w & jnp.uint32(0xFFFF0000), F32)
    o_ref[:, :h] = x_ref[:, :h] + g_ref[:, :h] * acc_lo
    o_ref[:, h:] = x_ref[:, h:] + g_ref[:, h:] * acc_hi


def _moe_combine(ys, dest, route, x2d, modv, layer, row_fn, tm):
    t, d = x2d.shape
    npc, s, pw = ys.shape
    idx = (jnp.arange(npc, dtype=jnp.int32)[:, None, None] * s
           + dest.reshape(t, TOP_K).T[None, :, :]).reshape(-1)
    rows = _sc_row_gather(ys.reshape(npc * s, pw), idx).reshape(npc, TOP_K, t, pw)
    return pl.pallas_call(
        _combine_kernel,
        grid=(t // tm,),
        in_specs=[
            pl.BlockSpec((npc, TOP_K, tm, pw), lambda i: (0, 0, i, 0)),
            pl.BlockSpec((tm, LANES), lambda i: (i, 0)),
            pl.BlockSpec((tm, d), lambda i: (i, 0)),
            _mod_spec(d, 5, row_fn, layer),
        ],
        out_specs=pl.BlockSpec((tm, d), lambda i: (i, 0)),
        out_shape=jax.ShapeDtypeStruct((t, d), F32),
        compiler_params=_cparams("arbitrary"),
        name="moe_combine",
    )(rows, route, x2d, modv)


def _routing_tables(route, counts_f, ne):
    t = route.shape[0]
    i32 = jnp.int32
    flat_e = route[:, :TOP_K].astype(i32).reshape(-1)
    rank = route[:, 2 * TOP_K:3 * TOP_K].astype(i32).reshape(-1)
    counts = counts_f[0, :ne].astype(i32)
    n_rows = t * TOP_K
    padded = (counts + MOE_ROWS - 1) // MOE_ROWS * MOE_ROWS
    pad_end = jnp.cumsum(padded)
    pad_start = pad_end - padded
    dest = pad_start[flat_e] + rank
    n_blocks = -(-n_rows // MOE_ROWS) + ne
    blk_row0 = jnp.arange(n_blocks, dtype=i32) * MOE_ROWS
    block_expert = jnp.minimum(
        jnp.sum((pad_end[None, :] <= blk_row0[:, None]).astype(i32), axis=1), ne - 1)
    n_used = (pad_end[-1:] // MOE_ROWS).astype(i32)
    shift = max(n_rows - 1, 1).bit_length()
    assert ne << shift < 2 ** 31
    order = jnp.sort(flat_e * (1 << shift) + jnp.arange(n_rows, dtype=i32)) & ((1 << shift) - 1)
    first = jnp.cumsum(counts) - counts
    local = (blk_row0 - pad_start[block_expert])[:, None] + jnp.arange(MOE_ROWS, dtype=i32)[None, :]
    src = jnp.clip(first[block_expert][:, None] + local, 0, n_rows - 1)
    spare = (blk_row0[:, None] + jnp.arange(MOE_ROWS, dtype=i32)[None, :]) % t
    slot_tok = jnp.where(local < counts[block_expert][:, None], order[src] // TOP_K, spare)
    return dest.astype(i32), slot_tok.reshape(-1).astype(i32), block_expert.astype(i32), n_used


def _moe_experts(f_packed, route, counts_f, w_up, b_up, w_down, b_down, layer):
    depth, ne = w_up.shape[:2]
    dest, slot_tok, block_expert, n_used = _routing_tables(route, counts_f, ne)
    b_pairs = b_up.reshape(depth, ne, 1, -1, 2)
    col = jnp.arange(2 * LANES)
    src = jnp.where(col < LANES, 2 * col, 2 * (col - LANES) + 1)
    perm = (jnp.arange(2 * LANES)[:, None] == src[None, :]).astype(BF16)
    xs = _moe_gather(f_packed, slot_tok)
    act = _moe_up(xs, block_expert, n_used, w_up, b_pairs[..., 0], b_pairs[..., 1], perm, layer)
    ys = _moe_down(act, block_expert, n_used, w_down, b_down.reshape(depth, ne, 1, -1), layer)
    return ys, dest


def _mla_proj_kernel(x_ref, g_ref, sh_ref, sc_ref, win_ref, gq_ref, gkv_ref, wq_ref, wkv_ref,
                     q_ref, kv_ref, kpe_ref, *, qp, kvp):
    h = _norm_mod(x_ref[...], g_ref[...], sh_ref[...], sc_ref[...]).astype(BF16)
    a = jnp.dot(h, win_ref[...], preferred_element_type=F32)
    cq = a[:, :qp]
    ckv = a[:, qp:qp + kvp]
    kpe_ref[...] = a[:, qp + kvp:]
    cqn = cq * lax.rsqrt(jnp.sum(cq * cq, axis=-1, keepdims=True) * (1.0 / Q_LORA_RANK) + NORM_EPS)
    ckvn = ckv * lax.rsqrt(jnp.sum(ckv * ckv, axis=-1, keepdims=True) * (1.0 / KV_LORA_RANK)
                           + NORM_EPS)
    q_ref[...] = jnp.dot((cqn * gq_ref[...]).astype(BF16), wq_ref[...], preferred_element_type=F32)
    kv_ref[...] = jnp.dot((ckvn * gkv_ref[...]).astype(BF16), wkv_ref[...],
                          preferred_element_type=F32)


def _mla_proj(x2d, g, modv, layer, row_fn, win_p, gq_p, gkv_p, wq_p, wkv_p, tm):
    t, d = x2d.shape
    qp, kvp = gq_p.shape[1], gkv_p.shape[1]
    na = win_p.shape[1]
    nq, nkv = wq_p.shape[1], wkv_p.shape[1]
    full = lambda arr: pl.BlockSpec(arr.shape, lambda i: (0, 0))
    return pl.pallas_call(
        functools.partial(_mla_proj_kernel, qp=qp, kvp=kvp),
        grid=(t // tm,),
        in_specs=[
            pl.BlockSpec((tm, d), lambda i: (i, 0)),
            pl.BlockSpec((1, d), lambda i: (0, 0)),
            _mod_spec(d, 0, row_fn, layer),
            _mod_spec(d, 1, row_fn, layer),
            full(win_p), full(gq_p), full(gkv_p), full(wq_p), full(wkv_p),
        ],
        out_specs=[pl.BlockSpec((tm, nq), lambda i: (i, 0)),
                   pl.BlockSpec((tm, nkv), lambda i: (i, 0)),
                   pl.BlockSpec((tm, na - qp - kvp), lambda i: (i, 0))],
        out_shape=[jax.ShapeDtypeStruct((t, nq), F32), jax.ShapeDtypeStruct((t, nkv), F32),
                   jax.ShapeDtypeStruct((t, na - qp - kvp), F32)],
        compiler_params=_cparams("arbitrary"),
        name="mla_projections",
    )(x2d, g, modv, modv, win_p, gq_p, gkv_p, wq_p, wkv_p)


def _rope(x, cos, sin_lo, sin_hi):
    quarter = QK_ROPE_DIM // 4
    return (x * cos + pltpu.roll(x, LANES - quarter, 1) * sin_lo
            + pltpu.roll(x, quarter, 1) * sin_hi)


def _headnorm_kernel(q_ref, kv_ref, kpe_ref, gq_ref, gk_ref, cos_ref, slo_ref, shi_ref,
                     qo_ref, ko_ref, vo_ref, *, heads, q_scale):
    inv = 1.0 / (QK_NOPE_DIM + QK_ROPE_DIM)
    cos, slo, shi = cos_ref[...], slo_ref[...], shi_ref[...]
    kpe = kpe_ref[...]
    kpe_ss = jnp.sum(kpe * kpe, axis=-1, keepdims=True)
    gq = gq_ref[...]
    gk = gk_ref[...]
    tm = q_ref.shape[0]
    ones_col = jnp.where(lax.broadcasted_iota(jnp.int32, (tm, V_HEAD_DIM), 1) == 0,
                         1.0, 0.0).astype(BF16)
    for h in range(heads):
        qh = q_ref[:, h * HEAD_SLOT:(h + 1) * HEAD_SLOT]
        rs = lax.rsqrt(jnp.sum(qh * qh, axis=-1, keepdims=True) * inv + NORM_EPS) * q_scale
        qn = qh * rs * gq
        qo_ref[h, :, :LANES] = qn[:, :LANES].astype(BF16)
        qo_ref[h, :, LANES:] = _rope(qn[:, LANES:], cos, slo, shi).astype(BF16)
        kn = kv_ref[:, h * HEAD_SLOT:h * HEAD_SLOT + QK_NOPE_DIM]
        rk = lax.rsqrt((jnp.sum(kn * kn, axis=-1, keepdims=True) + kpe_ss) * inv + NORM_EPS)
        ko_ref[h, :, :LANES] = (kn * rk * gk[:, :LANES]).astype(BF16)
        ko_ref[h, :, LANES:] = _rope(kpe * rk * gk[:, LANES:], cos, slo, shi).astype(BF16)
        vo_ref[h, :, :V_HEAD_DIM] = kv_ref[:, h * HEAD_SLOT + QK_NOPE_DIM:
                                           (h + 1) * HEAD_SLOT].astype(BF16)
        vo_ref[h, :, V_HEAD_DIM:] = ones_col


def _head_norm_rope(q_raw, kv_raw, kpe, gq_slot, gk_slot, cos, slo, shi, bsz, n_ctx, n_lat, tm):
    heads = MLA_HEADS
    n_keys = n_ctx + n_lat
    ctx_tiles = bsz * n_ctx // tm
    per_ctx = n_ctx // tm
    per_lat = n_lat // tm

    def batch_of(i):
        return jnp.where(i < ctx_tiles, i // per_ctx, (i - ctx_tiles) // per_lat)

    def key_blk(i):
        return jnp.where(i < ctx_tiles, i % per_ctx, per_ctx + (i - ctx_tiles) % per_lat)

    tab = pl.BlockSpec((tm, LANES), lambda i: (key_blk(i), 0))
    t = q_raw.shape[0]
    out_map = lambda i: (batch_of(i), 0, key_blk(i), 0)

    def q_map(i):
        blk = jnp.where(i < ctx_tiles, per_lat + i % per_ctx, (i - ctx_tiles) % per_lat)
        return (batch_of(i), 0, blk, 0)

    return pl.pallas_call(
        functools.partial(_headnorm_kernel, heads=heads,
                          q_scale=float(QK_NOPE_DIM + QK_ROPE_DIM) ** -0.5 * math.log2(math.e)),
        grid=(t // tm,),
        in_specs=[
            pl.BlockSpec((tm, heads * HEAD_SLOT), lambda i: (i, 0)),
            pl.BlockSpec((tm, heads * HEAD_SLOT), lambda i: (i, 0)),
            pl.BlockSpec((tm, LANES), lambda i: (i, 0)),
            pl.BlockSpec((1, HEAD_SLOT), lambda i: (0, 0)),
            pl.BlockSpec((1, HEAD_SLOT), lambda i: (0, 0)),
            tab, tab, tab,
        ],
        out_specs=[pl.BlockSpec((None, heads, tm, HEAD_SLOT), q_map),
                   pl.BlockSpec((None, heads, tm, HEAD_SLOT), out_map),
                   pl.BlockSpec((None, heads, tm, 2 * V_HEAD_DIM), out_map)],
        out_shape=[jax.ShapeDtypeStruct((bsz, heads, n_keys, HEAD_SLOT), BF16),
                   jax.ShapeDtypeStruct((bsz, heads, n_keys, HEAD_SLOT), BF16),
                   jax.ShapeDtypeStruct((bsz, heads, n_keys, 2 * V_HEAD_DIM), BF16)],
        compiler_params=_cparams("arbitrary"),
        name="mla_headnorm_rope",
    )(q_raw, kv_raw, kpe, gq_slot, gk_slot, cos, slo, shi)


def _attn_kernel(q_ref, k_ref, v_ref, o_ref, *, sub, kchunks):
    vd = o_ref.shape[1]
    n_sub = q_ref.shape[0] // sub
    (k0, k1), = kchunks

    def qk(r):
        return lax.dot_general(q_ref[pl.ds(r * sub, sub), :], k_ref[k0:k1, :],
                               (((1,), (1,)), ((), ())), preferred_element_type=F32)

    s_next = qk(0)
    for r in range(n_sub):
        s = s_next
        if r + 1 < n_sub:
            s_next = qk(r + 1)
        m = jnp.max(s, axis=-1, keepdims=True)
        acc = jnp.dot(jnp.exp2(s - m).astype(BF16), v_ref[k0:k1, :],
                      preferred_element_type=F32)
        o_ref[pl.ds(r * sub, sub), :] = (acc[:, :vd] / acc[:, vd:vd + 1]).astype(o_ref.dtype)


def _attention(q, k, v, n_ctx, tq):
    bsz, heads, n_keys, _ = k.shape
    n_lat = n_keys - n_ctx
    return pl.pallas_call(
        functools.partial(_attn_kernel, sub=min(256, tq), kchunks=[(0, n_keys)]),
        grid=(bsz, heads, n_lat // tq),
        in_specs=[
            pl.BlockSpec((None, None, tq, HEAD_SLOT), lambda b, h, i: (b, h, i, 0)),
            pl.BlockSpec((None, None, n_keys, HEAD_SLOT), lambda b, h, i: (b, h, 0, 0)),
            pl.BlockSpec((None, None, n_keys, 2 * V_HEAD_DIM), lambda b, h, i: (b, h, 0, 0)),
        ],
        out_specs=pl.BlockSpec((None, tq, V_HEAD_DIM), lambda b, h, i: (b, i, h)),
        out_shape=jax.ShapeDtypeStruct((bsz, n_lat, heads * V_HEAD_DIM), BF16),
        compiler_params=_cparams("arbitrary", "arbitrary", "arbitrary"),
        name="mla_attention",
    )(q, k, v)


def _dft_tables(n):
    k = jnp.arange(n, dtype=jnp.int32)
    ang = ((k[:, None] * k[None, :]) % n).astype(F32) * (2.0 * math.pi / n)
    return jnp.cos(ang), jnp.sin(ang)


def _rope_tables(n_ctx, n_lat):
    rows = n_lat // GRID_W
    pairs = QK_ROPE_DIM // 4
    row = jnp.repeat(jnp.arange(rows, dtype=F32), GRID_W)
    col = jnp.tile(jnp.arange(GRID_W, dtype=F32), rows)
    inv = ROPE_THETA ** (-jnp.arange(pairs, dtype=F32) / pairs)
    ang = jnp.stack([row[:, None] * inv, col[:, None] * inv], axis=1)
    cos, sin = jnp.cos(ang), jnp.sin(ang)
    zero = jnp.zeros_like(sin)
    cos_t = jnp.stack([cos, cos], axis=2).reshape(n_lat, QK_ROPE_DIM)
    slo_t = jnp.stack([-sin, zero], axis=2).reshape(n_lat, QK_ROPE_DIM)
    shi_t = jnp.stack([zero, sin], axis=2).reshape(n_lat, QK_ROPE_DIM)

    def full(tab, ctx_val):
        tab = jnp.pad(tab, ((0, 0), (0, LANES - QK_ROPE_DIM)))
        return jnp.concatenate([jnp.full((n_ctx, LANES), ctx_val, F32), tab], axis=0)

    return full(cos_t, 1.0), full(slo_t, 0.0), full(shi_t, 0.0)


def _pad_cols(w, n):
    return jnp.pad(w, ((0, 0), (0, n - w.shape[1])))


def _mla_weights(w_in, g_q_lora, w_q_up, g_kv_lora, w_kv_up, g_q_head, g_k_head):
    qp = -(-Q_LORA_RANK // LANES) * LANES
    kvp = -(-KV_LORA_RANK // LANES) * LANES
    qr, kvr = Q_LORA_RANK, KV_LORA_RANK
    win_p = jnp.concatenate([_pad_cols(w_in[:, :qr], qp), _pad_cols(w_in[:, qr:qr + kvr], kvp),
                             _pad_cols(w_in[:, qr + kvr:], LANES)], axis=1).astype(BF16)
    gq_p = _pad_cols(g_q_lora[None, :], qp)
    gkv_p = _pad_cols(g_kv_lora[None, :], kvp)
    hd = QK_NOPE_DIM + QK_ROPE_DIM
    wq = w_q_up.reshape(qr, MLA_HEADS, hd)
    wq = jnp.pad(wq, ((0, qp - qr), (0, 0), (0, HEAD_SLOT - hd)))
    wq_p = wq.reshape(qp, MLA_HEADS * HEAD_SLOT).astype(BF16)
    wkv_p = jnp.pad(w_kv_up, ((0, kvp - kvr), (0, 0))).astype(BF16)
    gq_slot = _pad_cols(g_q_head[None, :], HEAD_SLOT)
    gk_slot = _pad_cols(g_k_head[None, :], HEAD_SLOT)
    return win_p, gq_p, gkv_p, wq_p, wkv_p, gq_slot, gk_slot


def kernel(x, c, ctx, c_ctx, w_mod, b_mod, g_mix, g_ffn, fourier_w_out, fourier_b_out, mla_w_in, mla_g_q_lora, mla_w_q_up, mla_g_kv_lora, mla_w_kv_up, mla_g_q_head, mla_g_k_head, mla_w_out, router_w, router_b, expert_w_up, expert_b_up, expert_w_down, expert_b_down):
    bsz, n, d = x.shape
    l = ctx.shape[1]
    assert bsz < 8 and n % GRID_W == 0
    tl = min(256, l)
    assert l % tl == 0 and n % tl == 0
    ctx_row = bsz

    c8 = jnp.zeros((8, d), F32).at[:bsz].set(c).at[ctx_row].set(c_ctx)
    modv = _mod_vectors(c8, w_mod, b_mod).reshape(w_mod.shape[0], 8, 1, 6 * d)

    def lat_row(tm):
        return lambda i: i // (n // tm)

    def all_row(tm):
        nct = bsz * l // tm
        return lambda i: jnp.where(i < nct, ctx_row, (i - nct) // (n // tm))

    dg = d // FOURIER_GROUPS
    wf = fourier_w_out[0].astype(BF16)
    bf = fourier_b_out[0][None, :]
    g0 = g_mix[0][None, :]

    def fourier(x3, row_fn):
        nn = x3.shape[1]
        n1, n2, m_tab, w2, cs = _fourier_tables(nn, dg)
        tr, ti = _fft_stage1(x3, g0, modv, 0, row_fn, m_tab, n1, n2)
        y = _fft_stage2(tr, ti, w2, cs, 1.0 / math.sqrt(nn * dg))
        return _fft_out(y, wf, bf, x3, modv, 0, row_fn, n1, n2)

    tm_lat = min(512, n)
    x_lat1 = fourier(x, lambda b, j: b)
    x_ctx1 = fourier(ctx, lambda b, j: ctx_row)
    x_all = jnp.concatenate([x_ctx1, x_lat1], axis=0)
    n_ctx_tok = bsz * l

    f0, route0, cnt0 = _ffn_route(x_all, g_ffn[0][None, :], modv, 0, all_row(tl), router_w[0],
                                  router_b[0][None, :], tl)
    ys0, dest0 = _moe_experts(f0, route0, cnt0, expert_w_up, expert_b_up, expert_w_down,
                              expert_b_down, 0)
    tc = tl
    x_all2 = _moe_combine(ys0, dest0, route0, x_all, modv, 0, all_row(tc), tc)

    win_p, gq_p, gkv_p, wq_p, wkv_p, gq_slot, gk_slot = _mla_weights(
        mla_w_in[0], mla_g_q_lora[0], mla_w_q_up[0], mla_g_kv_lora[0], mla_w_kv_up[0],
        mla_g_q_head[0], mla_g_k_head[0])
    q_raw, kv_raw, kpe = _mla_proj(x_all2, g_mix[1][None, :], modv, 1, all_row(tl),
                                   win_p, gq_p, gkv_p, wq_p, wkv_p, tl)
    cos, slo, shi = _rope_tables(l, n)
    qh, kh, vh = _head_norm_rope(q_raw, kv_raw, kpe, gq_slot, gk_slot, cos, slo, shi,
                                 bsz, l, n, tl)
    attn = _attention(qh, kh, vh, l, min(1024, n))
    zero_bias = jnp.zeros((1, d), F32)
    tm_out = math.gcd(tm_lat, n_ctx_tok)
    x_lat3 = _mm_residual(attn.reshape(bsz * n, -1), mla_w_out[0].astype(BF16), zero_bias,
                          x_all2, n_ctx_tok, modv, 1, 2, lat_row(tm_out), tm_out)

    f1, route1, cnt1 = _ffn_route(x_lat3, g_ffn[1][None, :], modv, 1, lat_row(tl), router_w[1],
                                  router_b[1][None, :], tl)
    ys1, dest1 = _moe_experts(f1, route1, cnt1, expert_w_up, expert_b_up, expert_w_down,
                              expert_b_down, 1)
    out = _moe_combine(ys1, dest1, route1, x_lat3, modv, 1, lat_row(tc), tc)
    return out.reshape(bsz, n, d)
```

```python
import functools
import math

import jax
import jax.numpy as jnp
from jax import lax
from jax.experimental import pallas as pl
from jax.experimental.pallas import tpu as pltpu
from jax.experimental.pallas import tpu_sc as plsc

GRID_W = 64
FOURIER_GROUPS = 4
MLA_HEADS = 16
Q_LORA_RANK = 448
KV_LORA_RANK = 512
QK_NOPE_DIM = 128
QK_ROPE_DIM = 64
V_HEAD_DIM = 128
ROPE_THETA = 10000.0
TOP_K = 4
SWIGLU_ALPHA = 1.702
SWIGLU_LIMIT = 7.0
NORM_EPS = 1e-6

LANES = 128
MOE_ROWS = 256
HEAD_SLOT = 256
SC_PIECE_WORDS = 256
SC_WINDOW = 128
VMEM_LIMIT = 56 << 20
TOKEN_TILE = 256
DENSE_ROWS = 512
STREAM_COLS = 1024
ATTN_Q_ROWS = 1024
ATTN_SUB_ROWS = 256

F32 = jnp.float32
BF16 = jnp.bfloat16
HIGHEST = lax.Precision.HIGHEST


def _cparams(*sem, vmem=VMEM_LIMIT):
    return pltpu.CompilerParams(dimension_semantics=sem, vmem_limit_bytes=vmem)


def _norm_mod(x, g, sh, sc):
    ms = jnp.mean(x * x, axis=-1, keepdims=True)
    return (x * lax.rsqrt(ms + NORM_EPS)) * g * (1.0 + sc) + sh


def _pack_bf16_pairs(x):
    h = x.shape[1] // 2
    lo = pltpu.bitcast(x[:, :h].astype(BF16).astype(F32), jnp.uint32)
    hi = pltpu.bitcast(x[:, h:].astype(BF16).astype(F32), jnp.uint32)
    return hi | (lo >> 16)


def _unpack_bf16_pairs(w):
    lo = pltpu.bitcast(w << 16, F32).astype(BF16)
    hi = pltpu.bitcast(w & jnp.uint32(0xFFFF0000), F32).astype(BF16)
    return jnp.concatenate([lo, hi], axis=1)


def _piece_width(words):
    return min(SC_PIECE_WORDS, words)


def _store_pieces(ref, words):
    for c in range(ref.shape[0]):
        ref[c] = words[:, c * ref.shape[2]:(c + 1) * ref.shape[2]]


def _load_pieces(ref):
    return jnp.concatenate([ref[c] for c in range(ref.shape[0])], axis=1)


def _sc_row_gather(table, idx):
    n = idx.shape[0]
    pw = table.shape[1]
    assert n % SC_WINDOW == 0
    mesh = plsc.VectorSubcoreMesh(core_axis_name="c", subcore_axis_name="s")

    @pl.kernel(out_type=jax.ShapeDtypeStruct((n, pw), table.dtype), mesh=mesh)
    def gather(t_hbm, i_hbm, o_hbm):
        def body(i_vmem, o_vmem):
            pltpu.sync_copy(t_hbm.at[i_vmem.at[0]], o_vmem)

        pltpu.emit_pipeline(
            body,
            grid=(n // SC_WINDOW,),
            in_specs=[pl.BlockSpec((1, SC_WINDOW), index_map=lambda i: (0, i))],
            out_specs=[pl.BlockSpec((SC_WINDOW, pw), index_map=lambda i: (i, 0))],
            core_axis_name=("c", "s"),
            dimension_semantics=(pltpu.PARALLEL,),
        )(i_hbm, o_hbm)

    return gather(table, idx.reshape(1, n))


def _mod_spec(d, piece, row_fn, layer):
    return pl.BlockSpec((None, None, 1, d), lambda *ids: (layer, row_fn(*ids), 0, piece))


def _mod_kernel(c_ref, w_ref, b_ref, o_ref):
    c = c_ref[...]
    a = c * jax.nn.sigmoid(c)
    o_ref[...] = jnp.dot(a, w_ref[...], preferred_element_type=F32, precision=HIGHEST) + b_ref[...]


def _mod_vectors(c8, w_mod, b_mod):
    depth, d, n6 = w_mod.shape
    tn = next(t for t in (1024, 512, 256, 128) if n6 % t == 0)
    return pl.pallas_call(
        _mod_kernel,
        grid=(depth, n6 // tn),
        in_specs=[
            pl.BlockSpec((8, d), lambda l, j: (0, 0)),
            pl.BlockSpec((None, d, tn), lambda l, j: (l, 0, j)),
            pl.BlockSpec((None, 1, tn), lambda l, j: (l, 0, j)),
        ],
        out_specs=pl.BlockSpec((None, 8, tn), lambda l, j: (l, 0, j)),
        out_shape=jax.ShapeDtypeStruct((depth, 8, n6), F32),
        compiler_params=_cparams("arbitrary", "arbitrary"),
        name="mod_vectors",
    )(c8, w_mod, b_mod.reshape(depth, 1, n6))


ROW_CHUNK = 8


def _fft1_kernel(x_ref, g_ref, sh_ref, sc_ref, m_ref, tr_ref, ti_ref):
    n1 = x_ref.shape[0]
    g, sh, sc = g_ref[...], sh_ref[...], sc_ref[...]
    for c in range(x_ref.shape[1]):
        h = _norm_mod(x_ref[:, c, :], g, sh, sc).astype(BF16)
        t = jnp.dot(m_ref[c], h, preferred_element_type=F32)
        tr_ref[:, c, :] = t[:n1]
        ti_ref[:, c, :] = t[n1:]


def _fft_stage1(x3, g, modv, layer, row_fn, m_tab, n1, n2):
    bsz, n, d = x3.shape
    x4 = x3.reshape(bsz, n1, n2, d)
    blk = pl.BlockSpec((None, n1, ROW_CHUNK, d), lambda b, j: (b, 0, j, 0))
    return pl.pallas_call(
        _fft1_kernel,
        grid=(bsz, n2 // ROW_CHUNK),
        in_specs=[
            blk,
            pl.BlockSpec((1, d), lambda b, j: (0, 0)),
            _mod_spec(d, 0, row_fn, layer),
            _mod_spec(d, 1, row_fn, layer),
            pl.BlockSpec((ROW_CHUNK, 2 * n1, n1), lambda b, j: (j, 0, 0)),
        ],
        out_specs=[blk, blk],
        out_shape=[jax.ShapeDtypeStruct((bsz, n1, n2, d), F32)] * 2,
        compiler_params=_cparams("arbitrary", "arbitrary"),
        name="fourier_stage1",
    )(x4, g, modv, modv, m_tab)


def _fft2_kernel(tr_ref, ti_ref, w2_ref, cs_ref, y_ref, *, groups, scale):
    ck, n2, d = tr_ref.shape
    dg = d // groups
    xr, xi = [], []
    for kk in range(ck):
        t = jnp.concatenate([tr_ref[kk], ti_ref[kk]], axis=0).astype(BF16)
        xx = jnp.dot(w2_ref[...], t, preferred_element_type=F32)
        xr.append(xx[:n2])
        xi.append(xx[n2:])
    xr = jnp.concatenate(xr, axis=0).astype(BF16)
    xi = jnp.concatenate(xi, axis=0).astype(BF16)
    for gi in range(groups):
        cols = slice(gi * dg, (gi + 1) * dg)
        y = (jnp.dot(xr[:, cols], cs_ref[0], preferred_element_type=F32)
             + jnp.dot(xi[:, cols], cs_ref[1], preferred_element_type=F32))
        y_ref[:, cols] = (y * scale).astype(y_ref.dtype)


def _fft_stage2(tr, ti, w2, cs, scale):
    bsz, n1, n2, d = tr.shape
    ck = ROW_CHUNK
    blk = pl.BlockSpec((None, ck, n2, d), lambda b, j: (b, j, 0, 0))
    return pl.pallas_call(
        functools.partial(_fft2_kernel, groups=FOURIER_GROUPS, scale=scale),
        grid=(bsz, n1 // ck),
        in_specs=[blk, blk,
                  pl.BlockSpec(w2.shape, lambda b, j: (0, 0)),
                  pl.BlockSpec(cs.shape, lambda b, j: (0, 0, 0))],
        out_specs=pl.BlockSpec((None, ck * n2, d), lambda b, j: (b, j, 0)),
        out_shape=jax.ShapeDtypeStruct((bsz, n1 * n2, d), BF16),
        compiler_params=_cparams("arbitrary", "arbitrary"),
        name="fourier_stage2",
    )(tr, ti, w2, cs)


def _fft_out_kernel(y_ref, w_ref, b_ref, x_ref, g_ref, o_ref):
    n2, ck, _ = x_ref.shape
    out = jnp.dot(y_ref[...], w_ref[...], preferred_element_type=F32) + b_ref[...]
    gate = g_ref[...]
    for kk in range(ck):
        o_ref[:, kk, :] = x_ref[:, kk, :] + gate * out[kk * n2:(kk + 1) * n2]


def _fft_out(y, w_bf, bias, x3, modv, layer, row_fn, n1, n2):
    bsz, n, d = x3.shape
    ck = ROW_CHUNK
    x4 = x3.reshape(bsz, n2, n1, d)
    blk = pl.BlockSpec((None, n2, ck, d), lambda b, j: (b, 0, j, 0))
    out = pl.pallas_call(
        _fft_out_kernel,
        grid=(bsz, n1 // ck),
        in_specs=[
            pl.BlockSpec((None, ck * n2, d), lambda b, j: (b, j, 0)),
            pl.BlockSpec(w_bf.shape, lambda b, j: (0, 0)),
            pl.BlockSpec((1, d), lambda b, j: (0, 0)),
            blk,
            _mod_spec(d, 2, row_fn, layer),
        ],
        out_specs=blk,
        out_shape=jax.ShapeDtypeStruct((bsz, n2, n1, d), F32),
        compiler_params=_cparams("arbitrary", "arbitrary"),
        name="fourier_out_residual",
    )(y, w_bf, bias, x4, modv)
    return out.reshape(bsz * n, d)


def _fourier_tables(n, dg):
    assert n & (n - 1) == 0
    n2 = 1 << ((n.bit_length() - 1) // 2)
    n1 = n // n2
    assert n1 % ROW_CHUNK == 0 and n2 % ROW_CHUNK == 0
    i32 = jnp.int32
    k1 = jnp.arange(n1, dtype=i32)[None, :, None]
    pos = n2 * jnp.arange(n1, dtype=i32)[None, None, :] + jnp.arange(n2, dtype=i32)[:, None, None]
    ang1 = ((k1 * pos) % n).astype(F32) * (2.0 * math.pi / n)
    m_tab = jnp.concatenate([jnp.cos(ang1), -jnp.sin(ang1)], axis=1).astype(BF16)
    c2, s2 = _dft_tables(n2)
    w2 = jnp.concatenate([jnp.concatenate([c2, s2], axis=1),
                          jnp.concatenate([-s2, c2], axis=1)], axis=0).astype(BF16)
    cc, sc = _dft_tables(dg)
    cs = jnp.stack([cc, sc]).astype(BF16)
    return n1, n2, m_tab, w2, cs


def _mm_res_kernel(y_ref, w_ref, b_ref, r_ref, g_ref, o_ref):
    acc = jnp.dot(y_ref[...], w_ref[...], preferred_element_type=F32)
    o_ref[...] = r_ref[...] + g_ref[...] * (acc + b_ref[...])


def _mm_residual(y2d, w_bf, bias, res2d, res_row0, modv, layer, piece, row_fn, tm):
    t, k = y2d.shape
    d = w_bf.shape[1]
    tn = min(STREAM_COLS, d)
    nj = d // tn
    off = res_row0 // tm
    assert res_row0 % tm == 0
    return pl.pallas_call(
        _mm_res_kernel,
        grid=(nj, t // tm),
        in_specs=[
            pl.BlockSpec((tm, k), lambda j, i: (i, 0)),
            pl.BlockSpec((k, tn), lambda j, i: (0, j)),
            pl.BlockSpec((1, tn), lambda j, i: (0, j)),
            pl.BlockSpec((tm, tn), lambda j, i: (i + off, j)),
            pl.BlockSpec((None, None, 1, tn), lambda j, i: (layer, row_fn(i), 0, piece * nj + j)),
        ],
        out_specs=pl.BlockSpec((tm, tn), lambda j, i: (i, j)),
        out_shape=jax.ShapeDtypeStruct((t, d), F32),
        compiler_params=_cparams("arbitrary", "arbitrary"),
        name="mixer_out_residual",
    )(y2d, w_bf, bias, res2d, modv)


def _route_kernel(x_ref, g_ref, sh_ref, sc_ref, wr_ref, br_ref, f_ref, r_ref, cnt_ref):
    @pl.when(pl.program_id(0) == 0)
    def _():
        cnt_ref[...] = jnp.zeros_like(cnt_ref)

    f = _norm_mod(x_ref[...], g_ref[...], sh_ref[...], sc_ref[...])
    _store_pieces(f_ref, _pack_bf16_pairs(f))
    logits = jnp.dot(f, wr_ref[...], preferred_element_type=F32, precision=HIGHEST) + br_ref[...]
    tm, ne = logits.shape
    col = lax.broadcasted_iota(jnp.int32, (tm, ne), 1).astype(F32)
    lane = lax.broadcasted_iota(jnp.int32, (tm, LANES), 1)
    out = jnp.zeros((tm, LANES), F32)
    vals, idxs = [], []
    hot = jnp.zeros((tm, ne), F32)
    for k in range(TOP_K):
        m = jnp.max(logits, axis=-1, keepdims=True)
        idx = jnp.min(jnp.where(logits == m, col, float(ne)), axis=-1, keepdims=True)
        logits = jnp.where(col == idx, -jnp.inf, logits)
        hot = jnp.where(col == idx, 1.0, hot)
        out = jnp.where(lane == k, idx, out)
        vals.append(m)
        idxs.append(idx)
    es = [jnp.exp(v - vals[0]) for v in vals]
    den = es[0]
    for e in es[1:]:
        den = den + e
    for k in range(TOP_K):
        out = jnp.where(lane == TOP_K + k, es[k] / den, out)
    earlier = (lax.broadcasted_iota(jnp.int32, (tm, tm), 1)
               < lax.broadcasted_iota(jnp.int32, (tm, tm), 0))
    before = jnp.dot(jnp.where(earlier, 1.0, 0.0).astype(BF16), hot.astype(BF16),
                     preferred_element_type=F32) + cnt_ref[:, :ne]
    for k in range(TOP_K):
        rank = jnp.sum(jnp.where(col == idxs[k], before, 0.0), axis=-1, keepdims=True)
        out = jnp.where(lane == 2 * TOP_K + k, rank, out)
    cnt_ref[:, :ne] = cnt_ref[:, :ne] + jnp.sum(hot, axis=0, keepdims=True)
    r_ref[...] = out


def _ffn_route(x2d, g, modv, layer, row_fn, w_router, b_router, tm):
    t, d = x2d.shape
    ne = w_router.shape[1]
    pw = _piece_width(d // 2)
    spec = pl.BlockSpec((tm, d), lambda i: (i, 0))
    return pl.pallas_call(
        _route_kernel,
        grid=(t // tm,),
        in_specs=[
            spec,
            pl.BlockSpec((1, d), lambda i: (0, 0)),
            _mod_spec(d, 3, row_fn, layer),
            _mod_spec(d, 4, row_fn, layer),
            pl.BlockSpec((d, ne), lambda i: (0, 0)),
            pl.BlockSpec((1, ne), lambda i: (0, 0)),
        ],
        out_specs=[pl.BlockSpec((d // 2 // pw, tm, pw), lambda i: (0, i, 0)),
                   pl.BlockSpec((tm, LANES), lambda i: (i, 0)),
                   pl.BlockSpec((1, LANES), lambda i: (0, 0))],
        out_shape=[jax.ShapeDtypeStruct((d // 2 // pw, t, pw), jnp.uint32),
                   jax.ShapeDtypeStruct((t, LANES), F32),
                   jax.ShapeDtypeStruct((1, LANES), F32)],
        compiler_params=_cparams("arbitrary"),
        name="ffn_norm_route",
    )(x2d, g, modv, modv, w_router, b_router)


def _moe_gather(f_pieces, slot_tok):
    npc, t, pw = f_pieces.shape
    s = slot_tok.shape[0]
    idx = (jnp.arange(npc, dtype=jnp.int32)[:, None] * t + slot_tok[None, :]).reshape(-1)
    return _sc_row_gather(f_pieces.reshape(npc * t, pw), idx).reshape(npc, s, pw)


def _new_expert(be_ref, b):
    return (b == 0) | (be_ref[b] != be_ref[jnp.maximum(b - 1, 0)])


def _moe_up_kernel(be_ref, nu_ref, x_ref, w_ref, bg_ref, bl_ref, perm_ref, o_ref, wg_ref, wl_ref):
    b = pl.program_id(1)

    @pl.when(_new_expert(be_ref, b))
    def _():
        for q in range(w_ref.shape[1] // (2 * LANES)):
            wq = w_ref[:, q * 2 * LANES:(q + 1) * 2 * LANES].astype(BF16)
            r = jnp.dot(wq, perm_ref[...], preferred_element_type=F32)
            wg_ref[:, q * LANES:(q + 1) * LANES] = r[:, :LANES].astype(BF16)
            wl_ref[:, q * LANES:(q + 1) * LANES] = r[:, LANES:].astype(BF16)

    @pl.when(b < nu_ref[0])
    def _():
        x = _unpack_bf16_pairs(_load_pieces(x_ref))
        glu = jnp.dot(x, wg_ref[...], preferred_element_type=F32) + bg_ref[...]
        lin = jnp.dot(x, wl_ref[...], preferred_element_type=F32) + bl_ref[...]
        glu = jnp.minimum(glu, SWIGLU_LIMIT)
        lin = jnp.clip(lin, -SWIGLU_LIMIT, SWIGLU_LIMIT)
        o_ref[...] = (glu * jax.nn.sigmoid(SWIGLU_ALPHA * glu) * (lin + 1.0)).astype(o_ref.dtype)

    @pl.when(b >= nu_ref[0])
    def _():
        o_ref[...] = jnp.zeros_like(o_ref)


def _moe_up(xs, block_expert, n_used, w_up, b_glu, b_lin, perm, layer):
    npc, s, pw = xs.shape
    d = 2 * npc * pw
    nb = s // MOE_ROWS
    de = w_up.shape[3] // 2
    tn = min(STREAM_COLS, de)
    bspec = pl.BlockSpec((None, None, 1, tn), lambda j, b, be, nu: (layer, be[b], 0, j))
    return pl.pallas_call(
        _moe_up_kernel,
        grid_spec=pltpu.PrefetchScalarGridSpec(
            num_scalar_prefetch=2,
            grid=(de // tn, nb),
            in_specs=[pl.BlockSpec((npc, MOE_ROWS, pw), lambda j, b, be, nu: (0, b, 0)),
                      pl.BlockSpec((None, None, d, 2 * tn),
                                   lambda j, b, be, nu: (layer, be[b], 0, j)),
                      bspec, bspec, pl.BlockSpec(perm.shape, lambda j, b, be, nu: (0, 0))],
            out_specs=pl.BlockSpec((MOE_ROWS, tn), lambda j, b, be, nu: (b, j)),
            scratch_shapes=[pltpu.VMEM((d, tn), BF16), pltpu.VMEM((d, tn), BF16)],
        ),
        out_shape=jax.ShapeDtypeStruct((s, de), BF16),
        compiler_params=_cparams("arbitrary", "arbitrary"),
        name="moe_up_swiglu",
    )(block_expert, n_used, xs, w_up, b_glu, b_lin, perm)


def _moe_down_kernel(be_ref, nu_ref, a_ref, w_ref, b_ref, o_ref, wbf_ref):
    b = pl.program_id(1)

    @pl.when(_new_expert(be_ref, b))
    def _():
        wbf_ref[...] = w_ref[...].astype(BF16)

    @pl.when(b < nu_ref[0])
    def _():
        _store_pieces(o_ref, _pack_bf16_pairs(
            jnp.dot(a_ref[...], wbf_ref[...], preferred_element_type=F32) + b_ref[...]))

    @pl.when(b >= nu_ref[0])
    def _():
        o_ref[...] = jnp.zeros_like(o_ref)


def _moe_down(act, block_expert, n_used, w_down, b_down, layer):
    s, de = act.shape
    d = w_down.shape[3]
    nb = s // MOE_ROWS
    tn = d
    pw = _piece_width(d // 2)
    npc = d // 2 // pw
    return pl.pallas_call(
        _moe_down_kernel,
        grid_spec=pltpu.PrefetchScalarGridSpec(
            num_scalar_prefetch=2,
            grid=(d // tn, nb),
            in_specs=[pl.BlockSpec((MOE_ROWS, de), lambda j, b, be, nu: (b, 0)),
                      pl.BlockSpec((None, None, de, tn), lambda j, b, be, nu: (layer, be[b], 0, j)),
                      pl.BlockSpec((None, None, 1, tn), lambda j, b, be, nu: (layer, be[b], 0, j))],
            out_specs=pl.BlockSpec((npc, MOE_ROWS, pw), lambda j, b, be, nu: (0, b, 0)),
            scratch_shapes=[pltpu.VMEM((de, tn), BF16)],
        ),
        out_shape=jax.ShapeDtypeStruct((npc, s, pw), jnp.uint32),
        compiler_params=_cparams("arbitrary", "arbitrary"),
        name="moe_down",
    )(block_expert, n_used, act, w_down, b_down)


def _combine_kernel(y_ref, r_ref, x_ref, g_ref, o_ref):
    route = r_ref[...]
    h = x_ref.shape[1] // 2
    acc_lo = acc_hi = 0.0
    for k in range(TOP_K):
        w = jnp.concatenate([y_ref[c, k] for c in range(y_ref.shape[0])], axis=1)
        p = route[:, TOP_K + k:TOP_K + k + 1]
        acc_lo = acc_lo + p * pltpu.bitcast(w << 16, F32)
        acc_hi = acc_hi + p * pltpu.bitcast(w & jnp.uint32(0xFFFF0000), F32)
    o_ref[:, :h] = x_ref[:, :h] + g_ref[:, :h] * acc_lo
    o_ref[:, h:] = x_ref[:, h:] + g_ref[:, h:] * acc_hi


def _moe_combine(ys, dest, route, x2d, modv, layer, row_fn, tm):
    t, d = x2d.shape
    npc, s, pw = ys.shape
    idx = (jnp.arange(npc, dtype=jnp.int32)[:, None, None] * s
           + dest.reshape(t, TOP_K).T[None, :, :]).reshape(-1)
    rows = _sc_row_gather(ys.reshape(npc * s, pw), idx).reshape(npc, TOP_K, t, pw)
    return pl.pallas_call(
        _combine_kernel,
        grid=(t // tm,),
        in_specs=[
            pl.BlockSpec((npc, TOP_K, tm, pw), lambda i: (0, 0, i, 0)),
            pl.BlockSpec((tm, LANES), lambda i: (i, 0)),
            pl.BlockSpec((tm, d), lambda i: (i, 0)),
            _mod_spec(d, 5, row_fn, layer),
        ],
        out_specs=pl.BlockSpec((tm, d), lambda i: (i, 0)),
        out_shape=jax.ShapeDtypeStruct((t, d), F32),
        compiler_params=_cparams("arbitrary"),
        name="moe_combine",
    )(rows, route, x2d, modv)


def _routing_tables(route, counts_f, ne):
    t = route.shape[0]
    i32 = jnp.int32
    flat_e = route[:, :TOP_K].astype(i32).reshape(-1)
    rank = route[:, 2 * TOP_K:3 * TOP_K].astype(i32).reshape(-1)
    counts = counts_f[0, :ne].astype(i32)
    n_rows = t * TOP_K
    padded = (counts + MOE_ROWS - 1) // MOE_ROWS * MOE_ROWS
    pad_end = jnp.cumsum(padded)
    pad_start = pad_end - padded
    dest = pad_start[flat_e] + rank
    n_blocks = -(-n_rows // MOE_ROWS) + ne
    blk_row0 = jnp.arange(n_blocks, dtype=i32) * MOE_ROWS
    block_expert = jnp.minimum(
        jnp.sum((pad_end[None, :] <= blk_row0[:, None]).astype(i32), axis=1), ne - 1)
    n_used = (pad_end[-1:] // MOE_ROWS).astype(i32)
    shift = max(n_rows - 1, 1).bit_length()
    assert ne << shift < 2 ** 31
    order = jnp.sort(flat_e * (1 << shift) + jnp.arange(n_rows, dtype=i32)) & ((1 << shift) - 1)
    first = jnp.cumsum(counts) - counts
    local = (blk_row0 - pad_start[block_expert])[:, None] + jnp.arange(MOE_ROWS, dtype=i32)[None, :]
    src = jnp.clip(first[block_expert][:, None] + local, 0, n_rows - 1)
    spare = (blk_row0[:, None] + jnp.arange(MOE_ROWS, dtype=i32)[None, :]) % t
    slot_tok = jnp.where(local < counts[block_expert][:, None], order[src] // TOP_K, spare)
    return dest.astype(i32), slot_tok.reshape(-1).astype(i32), block_expert.astype(i32), n_used


def _moe_experts(f_packed, route, counts_f, w_up, b_up, w_down, b_down, layer):
    depth, ne = w_up.shape[:2]
    dest, slot_tok, block_expert, n_used = _routing_tables(route, counts_f, ne)
    b_pairs = b_up.reshape(depth, ne, 1, -1, 2)
    col = jnp.arange(2 * LANES)
    src = jnp.where(col < LANES, 2 * col, 2 * (col - LANES) + 1)
    perm = (jnp.arange(2 * LANES)[:, None] == src[None, :]).astype(BF16)
    xs = _moe_gather(f_packed, slot_tok)
    act = _moe_up(xs, block_expert, n_used, w_up, b_pairs[..., 0], b_pairs[..., 1], perm, layer)
    ys = _moe_down(act, block_expert, n_used, w_down, b_down.reshape(depth, ne, 1, -1), layer)
    return ys, dest


def _mla_proj_kernel(x_ref, g_ref, sh_ref, sc_ref, win_ref, gq_ref, gkv_ref, wq_ref, wkv_ref,
                     q_ref, kv_ref, kpe_ref, *, qp, kvp):
    h = _norm_mod(x_ref[...], g_ref[...], sh_ref[...], sc_ref[...]).astype(BF16)
    a = jnp.dot(h, win_ref[...], preferred_element_type=F32)
    cq = a[:, :qp]
    ckv = a[:, qp:qp + kvp]
    kpe_ref[...] = a[:, qp + kvp:]
    cqn = cq * lax.rsqrt(jnp.sum(cq * cq, axis=-1, keepdims=True) * (1.0 / Q_LORA_RANK) + NORM_EPS)
    ckvn = ckv * lax.rsqrt(jnp.sum(ckv * ckv, axis=-1, keepdims=True) * (1.0 / KV_LORA_RANK)
                           + NORM_EPS)
    q_ref[...] = jnp.dot((cqn * gq_ref[...]).astype(BF16), wq_ref[...], preferred_element_type=F32)
    kv_ref[...] = jnp.dot((ckvn * gkv_ref[...]).astype(BF16), wkv_ref[...],
                          preferred_element_type=F32)


def _mla_proj(x2d, g, modv, layer, row_fn, win_p, gq_p, gkv_p, wq_p, wkv_p, tm):
    t, d = x2d.shape
    qp, kvp = gq_p.shape[1], gkv_p.shape[1]
    na = win_p.shape[1]
    nq, nkv = wq_p.shape[1], wkv_p.shape[1]
    full = lambda arr: pl.BlockSpec(arr.shape, lambda i: (0, 0))
    return pl.pallas_call(
        functools.partial(_mla_proj_kernel, qp=qp, kvp=kvp),
        grid=(t // tm,),
        in_specs=[
            pl.BlockSpec((tm, d), lambda i: (i, 0)),
            pl.BlockSpec((1, d), lambda i: (0, 0)),
            _mod_spec(d, 0, row_fn, layer),
            _mod_spec(d, 1, row_fn, layer),
            full(win_p), full(gq_p), full(gkv_p), full(wq_p), full(wkv_p),
        ],
        out_specs=[pl.BlockSpec((tm, nq), lambda i: (i, 0)),
                   pl.BlockSpec((tm, nkv), lambda i: (i, 0)),
                   pl.BlockSpec((tm, na - qp - kvp), lambda i: (i, 0))],
        out_shape=[jax.ShapeDtypeStruct((t, nq), F32), jax.ShapeDtypeStruct((t, nkv), F32),
                   jax.ShapeDtypeStruct((t, na - qp - kvp), F32)],
        compiler_params=_cparams("arbitrary"),
        name="mla_projections",
    )(x2d, g, modv, modv, win_p, gq_p, gkv_p, wq_p, wkv_p)


def _rope(x, cos, sin_lo, sin_hi):
    quarter = QK_ROPE_DIM // 4
    return (x * cos + pltpu.roll(x, LANES - quarter, 1) * sin_lo
            + pltpu.roll(x, quarter, 1) * sin_hi)


def _headnorm_kernel(q_ref, kv_ref, kpe_ref, gq_ref, gk_ref, cos_ref, slo_ref, shi_ref,
                     qo_ref, ko_ref, vo_ref, *, heads, q_scale):
    inv = 1.0 / (QK_NOPE_DIM + QK_ROPE_DIM)
    cos, slo, shi = cos_ref[...], slo_ref[...], shi_ref[...]
    kpe = kpe_ref[...]
    kpe_ss = jnp.sum(kpe * kpe, axis=-1, keepdims=True)
    gq = gq_ref[...]
    gk = gk_ref[...]
    tm = q_ref.shape[0]
    ones_col = jnp.where(lax.broadcasted_iota(jnp.int32, (tm, V_HEAD_DIM), 1) == 0,
                         1.0, 0.0).astype(BF16)
    for h in range(heads):
        qh = q_ref[:, h * HEAD_SLOT:(h + 1) * HEAD_SLOT]
        rs = lax.rsqrt(jnp.sum(qh * qh, axis=-1, keepdims=True) * inv + NORM_EPS) * q_scale
        qn = qh * rs * gq
        qo_ref[h, :, :LANES] = qn[:, :LANES].astype(BF16)
        qo_ref[h, :, LANES:] = _rope(qn[:, LANES:], cos, slo, shi).astype(BF16)
        kn = kv_ref[:, h * HEAD_SLOT:h * HEAD_SLOT + QK_NOPE_DIM]
        rk = lax.rsqrt((jnp.sum(kn * kn, axis=-1, keepdims=True) + kpe_ss) * inv + NORM_EPS)
        ko_ref[h, :, :LANES] = (kn * rk * gk[:, :LANES]).astype(BF16)
        ko_ref[h, :, LANES:] = _rope(kpe * rk * gk[:, LANES:], cos, slo, shi).astype(BF16)
        vo_ref[h, :, :V_HEAD_DIM] = kv_ref[:, h * HEAD_SLOT + QK_NOPE_DIM:
                                           (h + 1) * HEAD_SLOT].astype(BF16)
        vo_ref[h, :, V_HEAD_DIM:] = ones_col


def _head_norm_rope(q_raw, kv_raw, kpe, gq_slot, gk_slot, cos, slo, shi, bsz, n_ctx, n_lat, tm):
    heads = MLA_HEADS
    n_keys = n_ctx + n_lat
    ctx_tiles = bsz * n_ctx // tm
    per_ctx = n_ctx // tm
    per_lat = n_lat // tm

    def batch_of(i):
        return jnp.where(i < ctx_tiles, i // per_ctx, (i - ctx_tiles) // per_lat)

    def key_blk(i):
        return jnp.where(i < ctx_tiles, i % per_ctx, per_ctx + (i - ctx_tiles) % per_lat)

    tab = pl.BlockSpec((tm, LANES), lambda i: (key_blk(i), 0))
    t = q_raw.shape[0]
    out_map = lambda i: (batch_of(i), 0, key_blk(i), 0)

    def q_map(i):
        blk = jnp.where(i < ctx_tiles, per_lat + i % per_ctx, (i - ctx_tiles) % per_lat)
        return (batch_of(i), 0, blk, 0)

    return pl.pallas_call(
        functools.partial(_headnorm_kernel, heads=heads,
                          q_scale=float(QK_NOPE_DIM + QK_ROPE_DIM) ** -0.5 * math.log2(math.e)),
        grid=(t // tm,),
        in_specs=[
            pl.BlockSpec((tm, heads * HEAD_SLOT), lambda i: (i, 0)),
            pl.BlockSpec((tm, heads * HEAD_SLOT), lambda i: (i, 0)),
            pl.BlockSpec((tm, LANES), lambda i: (i, 0)),
            pl.BlockSpec((1, HEAD_SLOT), lambda i: (0, 0)),
            pl.BlockSpec((1, HEAD_SLOT), lambda i: (0, 0)),
            tab, tab, tab,
        ],
        out_specs=[pl.BlockSpec((None, heads, tm, HEAD_SLOT), q_map),
                   pl.BlockSpec((None, heads, tm, HEAD_SLOT), out_map),
                   pl.BlockSpec((None, heads, tm, 2 * V_HEAD_DIM), out_map)],
        out_shape=[jax.ShapeDtypeStruct((bsz, heads, n_keys, HEAD_SLOT), BF16),
                   jax.ShapeDtypeStruct((bsz, heads, n_keys, HEAD_SLOT), BF16),
                   jax.ShapeDtypeStruct((bsz, heads, n_keys, 2 * V_HEAD_DIM), BF16)],
        compiler_params=_cparams("arbitrary"),
        name="mla_headnorm_rope",
    )(q_raw, kv_raw, kpe, gq_slot, gk_slot, cos, slo, shi)


def _attn_kernel(q_ref, k_ref, v_ref, o_ref, *, sub):
    vd = o_ref.shape[1]
    n_sub = q_ref.shape[0] // sub

    def qk(r):
        return lax.dot_general(q_ref[pl.ds(r * sub, sub), :], k_ref[...],
                               (((1,), (1,)), ((), ())), preferred_element_type=F32)

    s_next = qk(0)
    for r in range(n_sub):
        s = s_next
        if r + 1 < n_sub:
            s_next = qk(r + 1)
        m = jnp.max(s, axis=-1, keepdims=True)
        acc = jnp.dot(jnp.exp2(s - m).astype(BF16), v_ref[...], preferred_element_type=F32)
        o_ref[pl.ds(r * sub, sub), :] = (acc[:, :vd] / acc[:, vd:vd + 1]).astype(o_ref.dtype)


def _attention(q, k, v, n_ctx, tq):
    bsz, heads, n_keys, _ = k.shape
    n_lat = n_keys - n_ctx
    return pl.pallas_call(
        functools.partial(_attn_kernel, sub=min(ATTN_SUB_ROWS, tq)),
        grid=(bsz, heads, n_lat // tq),
        in_specs=[
            pl.BlockSpec((None, None, tq, HEAD_SLOT), lambda b, h, i: (b, h, i, 0)),
            pl.BlockSpec((None, None, n_keys, HEAD_SLOT), lambda b, h, i: (b, h, 0, 0)),
            pl.BlockSpec((None, None, n_keys, 2 * V_HEAD_DIM), lambda b, h, i: (b, h, 0, 0)),
        ],
        out_specs=pl.BlockSpec((None, tq, V_HEAD_DIM), lambda b, h, i: (b, i, h)),
        out_shape=jax.ShapeDtypeStruct((bsz, n_lat, heads * V_HEAD_DIM), BF16),
        compiler_params=_cparams("arbitrary", "arbitrary", "arbitrary"),
        name="mla_attention",
    )(q, k, v)


def _dft_tables(n):
    k = jnp.arange(n, dtype=jnp.int32)
    ang = ((k[:, None] * k[None, :]) % n).astype(F32) * (2.0 * math.pi / n)
    return jnp.cos(ang), jnp.sin(ang)


def _rope_tables(n_ctx, n_lat):
    rows = n_lat // GRID_W
    pairs = QK_ROPE_DIM // 4
    row = jnp.repeat(jnp.arange(rows, dtype=F32), GRID_W)
    col = jnp.tile(jnp.arange(GRID_W, dtype=F32), rows)
    inv = ROPE_THETA ** (-jnp.arange(pairs, dtype=F32) / pairs)
    ang = jnp.stack([row[:, None] * inv, col[:, None] * inv], axis=1)
    cos, sin = jnp.cos(ang), jnp.sin(ang)
    zero = jnp.zeros_like(sin)
    cos_t = jnp.stack([cos, cos], axis=2).reshape(n_lat, QK_ROPE_DIM)
    slo_t = jnp.stack([-sin, zero], axis=2).reshape(n_lat, QK_ROPE_DIM)
    shi_t = jnp.stack([zero, sin], axis=2).reshape(n_lat, QK_ROPE_DIM)

    def full(tab, ctx_val):
        tab = jnp.pad(tab, ((0, 0), (0, LANES - QK_ROPE_DIM)))
        return jnp.concatenate([jnp.full((n_ctx, LANES), ctx_val, F32), tab], axis=0)

    return full(cos_t, 1.0), full(slo_t, 0.0), full(shi_t, 0.0)


def _pad_cols(w, n):
    return jnp.pad(w, ((0, 0), (0, n - w.shape[1])))


def _mla_weights(w_in, g_q_lora, w_q_up, g_kv_lora, w_kv_up, g_q_head, g_k_head):
    qp = -(-Q_LORA_RANK // LANES) * LANES
    kvp = -(-KV_LORA_RANK // LANES) * LANES
    qr, kvr = Q_LORA_RANK, KV_LORA_RANK
    win_p = jnp.concatenate([_pad_cols(w_in[:, :qr], qp), _pad_cols(w_in[:, qr:qr + kvr], kvp),
                             _pad_cols(w_in[:, qr + kvr:], LANES)], axis=1).astype(BF16)
    gq_p = _pad_cols(g_q_lora[None, :], qp)
    gkv_p = _pad_cols(g_kv_lora[None, :], kvp)
    hd = QK_NOPE_DIM + QK_ROPE_DIM
    wq = w_q_up.reshape(qr, MLA_HEADS, hd)
    wq = jnp.pad(wq, ((0, qp - qr), (0, 0), (0, HEAD_SLOT - hd)))
    wq_p = wq.reshape(qp, MLA_HEADS * HEAD_SLOT).astype(BF16)
    wkv_p = jnp.pad(w_kv_up, ((0, kvp - kvr), (0, 0))).astype(BF16)
    gq_slot = _pad_cols(g_q_head[None, :], HEAD_SLOT)
    gk_slot = _pad_cols(g_k_head[None, :], HEAD_SLOT)
    return win_p, gq_p, gkv_p, wq_p, wkv_p, gq_slot, gk_slot


def kernel(x, c, ctx, c_ctx, w_mod, b_mod, g_mix, g_ffn, fourier_w_out, fourier_b_out, mla_w_in, mla_g_q_lora, mla_w_q_up, mla_g_kv_lora, mla_w_kv_up, mla_g_q_head, mla_g_k_head, mla_w_out, router_w, router_b, expert_w_up, expert_b_up, expert_w_down, expert_b_down):
    bsz, n, d = x.shape
    l = ctx.shape[1]
    assert bsz < 8 and n % GRID_W == 0
    tl = min(TOKEN_TILE, l)
    assert l % tl == 0 and n % tl == 0
    ctx_row = bsz

    c8 = jnp.zeros((8, d), F32).at[:bsz].set(c).at[ctx_row].set(c_ctx)
    modv = _mod_vectors(c8, w_mod, b_mod).reshape(w_mod.shape[0], 8, 1, 6 * d)

    def lat_row(tm):
        return lambda i: i // (n // tm)

    def all_row(tm):
        nct = bsz * l // tm
        return lambda i: jnp.where(i < nct, ctx_row, (i - nct) // (n // tm))

    dg = d // FOURIER_GROUPS
    wf = fourier_w_out[0].astype(BF16)
    bf = fourier_b_out[0][None, :]
    g0 = g_mix[0][None, :]

    def fourier(x3, row_fn):
        nn = x3.shape[1]
        n1, n2, m_tab, w2, cs = _fourier_tables(nn, dg)
        tr, ti = _fft_stage1(x3, g0, modv, 0, row_fn, m_tab, n1, n2)
        y = _fft_stage2(tr, ti, w2, cs, 1.0 / math.sqrt(nn * dg))
        return _fft_out(y, wf, bf, x3, modv, 0, row_fn, n1, n2)

    tm_lat = min(DENSE_ROWS, n)
    x_lat1 = fourier(x, lambda b, j: b)
    x_ctx1 = fourier(ctx, lambda b, j: ctx_row)
    x_all = jnp.concatenate([x_ctx1, x_lat1], axis=0)
    n_ctx_tok = bsz * l

    f0, route0, cnt0 = _ffn_route(x_all, g_ffn[0][None, :], modv, 0, all_row(tl), router_w[0],
                                  router_b[0][None, :], tl)
    ys0, dest0 = _moe_experts(f0, route0, cnt0, expert_w_up, expert_b_up, expert_w_down,
                              expert_b_down, 0)
    tc = math.gcd(DENSE_ROWS, n_ctx_tok, n)
    x_all2 = _moe_combine(ys0, dest0, route0, x_all, modv, 0, all_row(tc), tc)

    win_p, gq_p, gkv_p, wq_p, wkv_p, gq_slot, gk_slot = _mla_weights(
        mla_w_in[0], mla_g_q_lora[0], mla_w_q_up[0], mla_g_kv_lora[0], mla_w_kv_up[0],
        mla_g_q_head[0], mla_g_k_head[0])
    q_raw, kv_raw, kpe = _mla_proj(x_all2, g_mix[1][None, :], modv, 1, all_row(tl),
                                   win_p, gq_p, gkv_p, wq_p, wkv_p, tl)
    cos, slo, shi = _rope_tables(l, n)
    qh, kh, vh = _head_norm_rope(q_raw, kv_raw, kpe, gq_slot, gk_slot, cos, slo, shi,
                                 bsz, l, n, tl)
    attn = _attention(qh, kh, vh, l, min(ATTN_Q_ROWS, n))
    zero_bias = jnp.zeros((1, d), F32)
    tm_out = math.gcd(tm_lat, n_ctx_tok)
    x_lat3 = _mm_residual(attn.reshape(bsz * n, -1), mla_w_out[0].astype(BF16), zero_bias,
                          x_all2, n_ctx_tok, modv, 1, 2, lat_row(tm_out), tm_out)

    f1, route1, cnt1 = _ffn_route(x_lat3, g_ffn[1][None, :], modv, 1, lat_row(tl), router_w[1],
                                  router_b[1][None, :], tl)
    ys1, dest1 = _moe_experts(f1, route1, cnt1, expert_w_up, expert_b_up, expert_w_down,
                              expert_b_down, 1)
    out = _moe_combine(ys1, dest1, route1, x_lat3, modv, 1, lat_row(tc), tc)
    return out.reshape(bsz, n, d)
```

```python
import functools
import math

import jax
import jax.numpy as jnp
from jax import lax
from jax.experimental import pallas as pl
from jax.experimental.pallas import tpu as pltpu
from jax.experimental.pallas import tpu_sc as plsc

GRID_W = 64
FOURIER_GROUPS = 4
MLA_HEADS = 16
Q_LORA_RANK = 448
KV_LORA_RANK = 512
QK_NOPE_DIM = 128
QK_ROPE_DIM = 64
V_HEAD_DIM = 128
ROPE_THETA = 10000.0
TOP_K = 4
SWIGLU_ALPHA = 1.702
SWIGLU_LIMIT = 7.0
NORM_EPS = 1e-6

LANES = 128
MOE_ROWS = 256
HEAD_SLOT = 256
SC_PIECE_WORDS = 256
SC_WINDOW = 128
VMEM_LIMIT = 56 << 20
TOKEN_TILE = 256
DENSE_ROWS = 512
STREAM_COLS = 1024
ATTN_Q_ROWS = 2048
ATTN_SUB_ROWS = 256

F32 = jnp.float32
BF16 = jnp.bfloat16
HIGHEST = lax.Precision.HIGHEST


def _cparams(*sem, vmem=VMEM_LIMIT):
    return pltpu.CompilerParams(dimension_semantics=sem, vmem_limit_bytes=vmem)


def _norm_mod(x, g, sh, sc):
    ms = jnp.mean(x * x, axis=-1, keepdims=True)
    return (x * lax.rsqrt(ms + NORM_EPS)) * g * (1.0 + sc) + sh


def _pack_bf16_pairs(x):
    h = x.shape[1] // 2
    lo = pltpu.bitcast(x[:, :h].astype(BF16).astype(F32), jnp.uint32)
    hi = pltpu.bitcast(x[:, h:].astype(BF16).astype(F32), jnp.uint32)
    return hi | (lo >> 16)


def _unpack_bf16_pairs(w):
    lo = pltpu.bitcast(w << 16, F32).astype(BF16)
    hi = pltpu.bitcast(w & jnp.uint32(0xFFFF0000), F32).astype(BF16)
    return jnp.concatenate([lo, hi], axis=1)


def _piece_width(words):
    return min(SC_PIECE_WORDS, words)


def _store_pieces(ref, words):
    for c in range(ref.shape[0]):
        ref[c] = words[:, c * ref.shape[2]:(c + 1) * ref.shape[2]]


def _load_pieces(ref):
    return jnp.concatenate([ref[c] for c in range(ref.shape[0])], axis=1)


def _sc_row_gather(table, idx):
    n = idx.shape[0]
    pw = table.shape[1]
    assert n % SC_WINDOW == 0
    mesh = plsc.VectorSubcoreMesh(core_axis_name="c", subcore_axis_name="s")

    @pl.kernel(out_type=jax.ShapeDtypeStruct((n, pw), table.dtype), mesh=mesh)
    def gather(t_hbm, i_hbm, o_hbm):
        def body(i_vmem, o_vmem):
            pltpu.sync_copy(t_hbm.at[i_vmem.at[0]], o_vmem)

        pltpu.emit_pipeline(
            body,
            grid=(n // SC_WINDOW,),
            in_specs=[pl.BlockSpec((1, SC_WINDOW), index_map=lambda i: (0, i))],
            out_specs=[pl.BlockSpec((SC_WINDOW, pw), index_map=lambda i: (i, 0))],
            core_axis_name=("c", "s"),
            dimension_semantics=(pltpu.PARALLEL,),
        )(i_hbm, o_hbm)

    return gather(table, idx.reshape(1, n))


def _mod_spec(d, piece, row_fn, layer):
    return pl.BlockSpec((None, None, 1, d), lambda *ids: (layer, row_fn(*ids), 0, piece))


def _mod_kernel(c_ref, w_ref, b_ref, o_ref):
    c = c_ref[...]
    a = c * jax.nn.sigmoid(c)
    o_ref[...] = jnp.dot(a, w_ref[...], preferred_element_type=F32, precision=HIGHEST) + b_ref[...]


def _mod_vectors(c8, w_mod, b_mod):
    depth, d, n6 = w_mod.shape
    tn = next(t for t in (1024, 512, 256, 128) if n6 % t == 0)
    return pl.pallas_call(
        _mod_kernel,
        grid=(depth, n6 // tn),
        in_specs=[
            pl.BlockSpec((8, d), lambda l, j: (0, 0)),
            pl.BlockSpec((None, d, tn), lambda l, j: (l, 0, j)),
            pl.BlockSpec((None, 1, tn), lambda l, j: (l, 0, j)),
        ],
        out_specs=pl.BlockSpec((None, 8, tn), lambda l, j: (l, 0, j)),
        out_shape=jax.ShapeDtypeStruct((depth, 8, n6), F32),
        compiler_params=_cparams("arbitrary", "arbitrary"),
        name="mod_vectors",
    )(c8, w_mod, b_mod.reshape(depth, 1, n6))


ROW_CHUNK = 8


def _fft1_kernel(x_ref, g_ref, sh_ref, sc_ref, m_ref, tr_ref, ti_ref):
    n1 = x_ref.shape[0]
    g, sh, sc = g_ref[...], sh_ref[...], sc_ref[...]
    for c in range(x_ref.shape[1]):
        h = _norm_mod(x_ref[:, c, :], g, sh, sc).astype(BF16)
        t = jnp.dot(m_ref[c], h, preferred_element_type=F32)
        tr_ref[:, c, :] = t[:n1]
        ti_ref[:, c, :] = t[n1:]


def _fft_stage1(x3, g, modv, layer, row_fn, m_tab, n1, n2):
    bsz, n, d = x3.shape
    x4 = x3.reshape(bsz, n1, n2, d)
    blk = pl.BlockSpec((None, n1, ROW_CHUNK, d), lambda b, j: (b, 0, j, 0))
    return pl.pallas_call(
        _fft1_kernel,
        grid=(bsz, n2 // ROW_CHUNK),
        in_specs=[
            blk,
            pl.BlockSpec((1, d), lambda b, j: (0, 0)),
            _mod_spec(d, 0, row_fn, layer),
            _mod_spec(d, 1, row_fn, layer),
            pl.BlockSpec((ROW_CHUNK, 2 * n1, n1), lambda b, j: (j, 0, 0)),
        ],
        out_specs=[blk, blk],
        out_shape=[jax.ShapeDtypeStruct((bsz, n1, n2, d), F32)] * 2,
        compiler_params=_cparams("arbitrary", "arbitrary"),
        name="fourier_stage1",
    )(x4, g, modv, modv, m_tab)


def _fft2_kernel(tr_ref, ti_ref, w2_ref, cs_ref, y_ref, *, groups, scale):
    ck, n2, d = tr_ref.shape
    dg = d // groups
    xr, xi = [], []
    for kk in range(ck):
        t = jnp.concatenate([tr_ref[kk], ti_ref[kk]], axis=0).astype(BF16)
        xx = jnp.dot(w2_ref[...], t, preferred_element_type=F32)
        xr.append(xx[:n2])
        xi.append(xx[n2:])
    xr = jnp.concatenate(xr, axis=0).astype(BF16)
    xi = jnp.concatenate(xi, axis=0).astype(BF16)
    for gi in range(groups):
        cols = slice(gi * dg, (gi + 1) * dg)
        y = (jnp.dot(xr[:, cols], cs_ref[0], preferred_element_type=F32)
             + jnp.dot(xi[:, cols], cs_ref[1], preferred_element_type=F32))
        y_ref[:, cols] = (y * scale).astype(y_ref.dtype)


def _fft_stage2(tr, ti, w2, cs, scale):
    bsz, n1, n2, d = tr.shape
    ck = ROW_CHUNK
    blk = pl.BlockSpec((None, ck, n2, d), lambda b, j: (b, j, 0, 0))
    return pl.pallas_call(
        functools.partial(_fft2_kernel, groups=FOURIER_GROUPS, scale=scale),
        grid=(bsz, n1 // ck),
        in_specs=[blk, blk,
                  pl.BlockSpec(w2.shape, lambda b, j: (0, 0)),
                  pl.BlockSpec(cs.shape, lambda b, j: (0, 0, 0))],
        out_specs=pl.BlockSpec((None, ck * n2, d), lambda b, j: (b, j, 0)),
        out_shape=jax.ShapeDtypeStruct((bsz, n1 * n2, d), BF16),
        compiler_params=_cparams("arbitrary", "arbitrary"),
        name="fourier_stage2",
    )(tr, ti, w2, cs)


def _fft_out_kernel(y_ref, w_ref, b_ref, x_ref, g_ref, o_ref):
    n2, ck, _ = x_ref.shape
    out = jnp.dot(y_ref[...], w_ref[...], preferred_element_type=F32) + b_ref[...]
    gate = g_ref[...]
    for kk in range(ck):
        o_ref[:, kk, :] = x_ref[:, kk, :] + gate * out[kk * n2:(kk + 1) * n2]


def _fft_out(y, w_bf, bias, x3, modv, layer, row_fn, n1, n2):
    bsz, n, d = x3.shape
    ck = ROW_CHUNK
    x4 = x3.reshape(bsz, n2, n1, d)
    blk = pl.BlockSpec((None, n2, ck, d), lambda b, j: (b, 0, j, 0))
    out = pl.pallas_call(
        _fft_out_kernel,
        grid=(bsz, n1 // ck),
        in_specs=[
            pl.BlockSpec((None, ck * n2, d), lambda b, j: (b, j, 0)),
            pl.BlockSpec(w_bf.shape, lambda b, j: (0, 0)),
            pl.BlockSpec((1, d), lambda b, j: (0, 0)),
            blk,
            _mod_spec(d, 2, row_fn, layer),
        ],
        out_specs=blk,
        out_shape=jax.ShapeDtypeStruct((bsz, n2, n1, d), F32),
        compiler_params=_cparams("arbitrary", "arbitrary"),
        name="fourier_out_residual",
    )(y, w_bf, bias, x4, modv)
    return out.reshape(bsz * n, d)


def _fourier_tables(n, dg):
    assert n & (n - 1) == 0
    n2 = 1 << ((n.bit_length() - 1) // 2)
    n1 = n // n2
    assert n1 % ROW_CHUNK == 0 and n2 % ROW_CHUNK == 0
    i32 = jnp.int32
    k1 = jnp.arange(n1, dtype=i32)[None, :, None]
    pos = n2 * jnp.arange(n1, dtype=i32)[None, None, :] + jnp.arange(n2, dtype=i32)[:, None, None]
    ang1 = ((k1 * pos) % n).astype(F32) * (2.0 * math.pi / n)
    m_tab = jnp.concatenate([jnp.cos(ang1), -jnp.sin(ang1)], axis=1).astype(BF16)
    c2, s2 = _dft_tables(n2)
    w2 = jnp.concatenate([jnp.concatenate([c2, s2], axis=1),
                          jnp.concatenate([-s2, c2], axis=1)], axis=0).astype(BF16)
    cc, sc = _dft_tables(dg)
    cs = jnp.stack([cc, sc]).astype(BF16)
    return n1, n2, m_tab, w2, cs


def _mm_res_kernel(y_ref, w_ref, b_ref, r_ref, g_ref, o_ref):
    acc = jnp.dot(y_ref[...], w_ref[...], preferred_element_type=F32)
    o_ref[...] = r_ref[...] + g_ref[...] * (acc + b_ref[...])


def _mm_residual(y2d, w_bf, bias, res2d, res_row0, modv, layer, piece, row_fn, tm):
    t, k = y2d.shape
    d = w_bf.shape[1]
    tn = min(STREAM_COLS, d)
    nj = d // tn
    off = res_row0 // tm
    assert res_row0 % tm == 0
    return pl.pallas_call(
        _mm_res_kernel,
        grid=(nj, t // tm),
        in_specs=[
            pl.BlockSpec((tm, k), lambda j, i: (i, 0)),
            pl.BlockSpec((k, tn), lambda j, i: (0, j)),
            pl.BlockSpec((1, tn), lambda j, i: (0, j)),
            pl.BlockSpec((tm, tn), lambda j, i: (i + off, j)),
            pl.BlockSpec((None, None, 1, tn), lambda j, i: (layer, row_fn(i), 0, piece * nj + j)),
        ],
        out_specs=pl.BlockSpec((tm, tn), lambda j, i: (i, j)),
        out_shape=jax.ShapeDtypeStruct((t, d), F32),
        compiler_params=_cparams("arbitrary", "arbitrary"),
        name="mixer_out_residual",
    )(y2d, w_bf, bias, res2d, modv)


def _route_kernel(x_ref, g_ref, sh_ref, sc_ref, wr_ref, br_ref, f_ref, r_ref, cnt_ref):
    @pl.when(pl.program_id(0) == 0)
    def _():
        cnt_ref[...] = jnp.zeros_like(cnt_ref)

    f = _norm_mod(x_ref[...], g_ref[...], sh_ref[...], sc_ref[...])
    _store_pieces(f_ref, _pack_bf16_pairs(f))
    logits = jnp.dot(f, wr_ref[...], preferred_element_type=F32, precision=HIGHEST) + br_ref[...]
    tm, ne = logits.shape
    col = lax.broadcasted_iota(jnp.int32, (tm, ne), 1).astype(F32)
    lane = lax.broadcasted_iota(jnp.int32, (tm, LANES), 1)
    out = jnp.zeros((tm, LANES), F32)
    vals, idxs = [], []
    hot = jnp.zeros((tm, ne), F32)
    for k in range(TOP_K):
        m = jnp.max(logits, axis=-1, keepdims=True)
        idx = jnp.min(jnp.where(logits == m, col, float(ne)), axis=-1, keepdims=True)
        logits = jnp.where(col == idx, -jnp.inf, logits)
        hot = jnp.where(col == idx, 1.0, hot)
        out = jnp.where(lane == k, idx, out)
        vals.append(m)
        idxs.append(idx)
    es = [jnp.exp(v - vals[0]) for v in vals]
    den = es[0]
    for e in es[1:]:
        den = den + e
    for k in range(TOP_K):
        out = jnp.where(lane == TOP_K + k, es[k] / den, out)
    earlier = (lax.broadcasted_iota(jnp.int32, (tm, tm), 1)
               < lax.broadcasted_iota(jnp.int32, (tm, tm), 0))
    before = jnp.dot(jnp.where(earlier, 1.0, 0.0).astype(BF16), hot.astype(BF16),
                     preferred_element_type=F32) + cnt_ref[:, :ne]
    for k in range(TOP_K):
        rank = jnp.sum(jnp.where(col == idxs[k], before, 0.0), axis=-1, keepdims=True)
        out = jnp.where(lane == 2 * TOP_K + k, rank, out)
    cnt_ref[:, :ne] = cnt_ref[:, :ne] + jnp.sum(hot, axis=0, keepdims=True)
    r_ref[...] = out


def _ffn_route(x2d, g, modv, layer, row_fn, w_router, b_router, tm):
    t, d = x2d.shape
    ne = w_router.shape[1]
    pw = _piece_width(d // 2)
    spec = pl.BlockSpec((tm, d), lambda i: (i, 0))
    return pl.pallas_call(
        _route_kernel,
        grid=(t // tm,),
        in_specs=[
            spec,
            pl.BlockSpec((1, d), lambda i: (0, 0)),
            _mod_spec(d, 3, row_fn, layer),
            _mod_spec(d, 4, row_fn, layer),
            pl.BlockSpec((d, ne), lambda i: (0, 0)),
            pl.BlockSpec((1, ne), lambda i: (0, 0)),
        ],
        out_specs=[pl.BlockSpec((d // 2 // pw, tm, pw), lambda i: (0, i, 0)),
                   pl.BlockSpec((tm, LANES), lambda i: (i, 0)),
                   pl.BlockSpec((1, LANES), lambda i: (0, 0))],
        out_shape=[jax.ShapeDtypeStruct((d // 2 // pw, t, pw), jnp.uint32),
                   jax.ShapeDtypeStruct((t, LANES), F32),
                   jax.ShapeDtypeStruct((1, LANES), F32)],
        compiler_params=_cparams("arbitrary"),
        name="ffn_norm_route",
    )(x2d, g, modv, modv, w_router, b_router)


def _moe_gather(f_pieces, slot_tok):
    npc, t, pw = f_pieces.shape
    s = slot_tok.shape[0]
    idx = (jnp.arange(npc, dtype=jnp.int32)[:, None] * t + slot_tok[None, :]).reshape(-1)
    return _sc_row_gather(f_pieces.reshape(npc * t, pw), idx).reshape(npc, s, pw)


def _new_expert(be_ref, b):
    return (b == 0) | (be_ref[b] != be_ref[jnp.maximum(b - 1, 0)])


def _moe_up_kernel(be_ref, nu_ref, x_ref, w_ref, bg_ref, bl_ref, perm_ref, o_ref, wg_ref, wl_ref):
    b = pl.program_id(1)

    @pl.when(_new_expert(be_ref, b))
    def _():
        for q in range(w_ref.shape[1] // (2 * LANES)):
            wq = w_ref[:, q * 2 * LANES:(q + 1) * 2 * LANES].astype(BF16)
            r = jnp.dot(wq, perm_ref[...], preferred_element_type=F32)
            wg_ref[:, q * LANES:(q + 1) * LANES] = r[:, :LANES].astype(BF16)
            wl_ref[:, q * LANES:(q + 1) * LANES] = r[:, LANES:].astype(BF16)

    @pl.when(b < nu_ref[0])
    def _():
        x = _unpack_bf16_pairs(_load_pieces(x_ref))
        glu = jnp.dot(x, wg_ref[...], preferred_element_type=F32) + bg_ref[...]
        lin = jnp.dot(x, wl_ref[...], preferred_element_type=F32) + bl_ref[...]
        glu = jnp.minimum(glu, SWIGLU_LIMIT)
        lin = jnp.clip(lin, -SWIGLU_LIMIT, SWIGLU_LIMIT)
        o_ref[...] = (glu * jax.nn.sigmoid(SWIGLU_ALPHA * glu) * (lin + 1.0)).astype(o_ref.dtype)

    @pl.when(b >= nu_ref[0])
    def _():
        o_ref[...] = jnp.zeros_like(o_ref)


def _moe_up(xs, block_expert, n_used, w_up, b_glu, b_lin, perm, layer):
    npc, s, pw = xs.shape
    d = 2 * npc * pw
    nb = s // MOE_ROWS
    de = w_up.shape[3] // 2
    tn = min(STREAM_COLS, de)
    bspec = pl.BlockSpec((None, None, 1, tn), lambda j, b, be, nu: (layer, be[b], 0, j))
    return pl.pallas_call(
        _moe_up_kernel,
        grid_spec=pltpu.PrefetchScalarGridSpec(
            num_scalar_prefetch=2,
            grid=(de // tn, nb),
            in_specs=[pl.BlockSpec((npc, MOE_ROWS, pw), lambda j, b, be, nu: (0, b, 0)),
                      pl.BlockSpec((None, None, d, 2 * tn),
                                   lambda j, b, be, nu: (layer, be[b], 0, j)),
                      bspec, bspec, pl.BlockSpec(perm.shape, lambda j, b, be, nu: (0, 0))],
            out_specs=pl.BlockSpec((MOE_ROWS, tn), lambda j, b, be, nu: (b, j)),
            scratch_shapes=[pltpu.VMEM((d, tn), BF16), pltpu.VMEM((d, tn), BF16)],
        ),
        out_shape=jax.ShapeDtypeStruct((s, de), BF16),
        compiler_params=_cparams("arbitrary", "arbitrary"),
        name="moe_up_swiglu",
    )(block_expert, n_used, xs, w_up, b_glu, b_lin, perm)


def _moe_down_kernel(be_ref, nu_ref, a_ref, w_ref, b_ref, o_ref, wbf_ref):
    b = pl.program_id(1)

    @pl.when(_new_expert(be_ref, b))
    def _():
        wbf_ref[...] = w_ref[...].astype(BF16)

    @pl.when(b < nu_ref[0])
    def _():
        _store_pieces(o_ref, _pack_bf16_pairs(
            jnp.dot(a_ref[...], wbf_ref[...], preferred_element_type=F32) + b_ref[...]))

    @pl.when(b >= nu_ref[0])
    def _():
        o_ref[...] = jnp.zeros_like(o_ref)


def _moe_down(act, block_expert, n_used, w_down, b_down, layer):
    s, de = act.shape
    d = w_down.shape[3]
    nb = s // MOE_ROWS
    tn = d
    pw = _piece_width(d // 2)
    npc = d // 2 // pw
    return pl.pallas_call(
        _moe_down_kernel,
        grid_spec=pltpu.PrefetchScalarGridSpec(
            num_scalar_prefetch=2,
            grid=(d // tn, nb),
            in_specs=[pl.BlockSpec((MOE_ROWS, de), lambda j, b, be, nu: (b, 0)),
                      pl.BlockSpec((None, None, de, tn), lambda j, b, be, nu: (layer, be[b], 0, j)),
                      pl.BlockSpec((None, None, 1, tn), lambda j, b, be, nu: (layer, be[b], 0, j))],
            out_specs=pl.BlockSpec((npc, MOE_ROWS, pw), lambda j, b, be, nu: (0, b, 0)),
            scratch_shapes=[pltpu.VMEM((de, tn), BF16)],
        ),
        out_shape=jax.ShapeDtypeStruct((npc, s, pw), jnp.uint32),
        compiler_params=_cparams("arbitrary", "arbitrary"),
        name="moe_down",
    )(block_expert, n_used, act, w_down, b_down)


def _combine_kernel(y_ref, r_ref, x_ref, g_ref, o_ref):
    route = r_ref[...]
    h = x_ref.shape[1] // 2
    acc_lo = acc_hi = 0.0
    for k in range(TOP_K):
        w = jnp.concatenate([y_ref[c, k] for c in range(y_ref.shape[0])], axis=1)
        p = route[:, TOP_K + k:TOP_K + k + 1]
        acc_lo = acc_lo + p * pltpu.bitcast(w << 16, F32)
        acc_hi = acc_hi + p * pltpu.bitcast(w & jnp.uint32(0xFFFF0000), F32)
    o_ref[:, :h] = x_ref[:, :h] + g_ref[:, :h] * acc_lo
    o_ref[:, h:] = x_ref[:, h:] + g_ref[:, h:] * acc_hi


def _moe_combine(ys, dest, route, x2d, modv, layer, row_fn, tm):
    t, d = x2d.shape
    npc, s, pw = ys.shape
    idx = (jnp.arange(npc, dtype=jnp.int32)[:, None, None] * s
           + dest.reshape(t, TOP_K).T[None, :, :]).reshape(-1)
    rows = _sc_row_gather(ys.reshape(npc * s, pw), idx).reshape(npc, TOP_K, t, pw)
    return pl.pallas_call(
        _combine_kernel,
        grid=(t // tm,),
        in_specs=[
            pl.BlockSpec((npc, TOP_K, tm, pw), lambda i: (0, 0, i, 0)),
            pl.BlockSpec((tm, LANES), lambda i: (i, 0)),
            pl.BlockSpec((tm, d), lambda i: (i, 0)),
            _mod_spec(d, 5, row_fn, layer),
        ],
        out_specs=pl.BlockSpec((tm, d), lambda i: (i, 0)),
        out_shape=jax.ShapeDtypeStruct((t, d), F32),
        compiler_params=_cparams("arbitrary"),
        name="moe_combine",
    )(rows, route, x2d, modv)


def _routing_tables(route, counts_f, ne):
    t = route.shape[0]
    i32 = jnp.int32
    flat_e = route[:, :TOP_K].astype(i32).reshape(-1)
    rank = route[:, 2 * TOP_K:3 * TOP_K].astype(i32).reshape(-1)
    counts = counts_f[0, :ne].astype(i32)
    n_rows = t * TOP_K
    padded = (counts + MOE_ROWS - 1) // MOE_ROWS * MOE_ROWS
    pad_end = jnp.cumsum(padded)
    pad_start = pad_end - padded
    dest = pad_start[flat_e] + rank
    n_blocks = -(-n_rows // MOE_ROWS) + ne
    blk_row0 = jnp.arange(n_blocks, dtype=i32) * MOE_ROWS
    block_expert = jnp.minimum(
        jnp.sum((pad_end[None, :] <= blk_row0[:, None]).astype(i32), axis=1), ne - 1)
    n_used = (pad_end[-1:] // MOE_ROWS).astype(i32)
    shift = max(n_rows - 1, 1).bit_length()
    assert ne << shift < 2 ** 31
    order = jnp.sort(flat_e * (1 << shift) + jnp.arange(n_rows, dtype=i32)) & ((1 << shift) - 1)
    first = jnp.cumsum(counts) - counts
    local = (blk_row0 - pad_start[block_expert])[:, None] + jnp.arange(MOE_ROWS, dtype=i32)[None, :]
    src = jnp.clip(first[block_expert][:, None] + local, 0, n_rows - 1)
    spare = (blk_row0[:, None] + jnp.arange(MOE_ROWS, dtype=i32)[None, :]) % t
    slot_tok = jnp.where(local < counts[block_expert][:, None], order[src] // TOP_K, spare)
    return dest.astype(i32), slot_tok.reshape(-1).astype(i32), block_expert.astype(i32), n_used


def _moe_experts(f_packed, route, counts_f, w_up, b_up, w_down, b_down, layer):
    depth, ne = w_up.shape[:2]
    dest, slot_tok, block_expert, n_used = _routing_tables(route, counts_f, ne)
    b_pairs = b_up.reshape(depth, ne, 1, -1, 2)
    col = jnp.arange(2 * LANES)
    src = jnp.where(col < LANES, 2 * col, 2 * (col - LANES) + 1)
    perm = (jnp.arange(2 * LANES)[:, None] == src[None, :]).astype(BF16)
    xs = _moe_gather(f_packed, slot_tok)
    act = _moe_up(xs, block_expert, n_used, w_up, b_pairs[..., 0], b_pairs[..., 1], perm, layer)
    ys = _moe_down(act, block_expert, n_used, w_down, b_down.reshape(depth, ne, 1, -1), layer)
    return ys, dest


def _mla_proj_kernel(x_ref, g_ref, sh_ref, sc_ref, win_ref, gq_ref, gkv_ref, wq_ref, wkv_ref,
                     q_ref, kv_ref, kpe_ref, *, qp, kvp):
    h = _norm_mod(x_ref[...], g_ref[...], sh_ref[...], sc_ref[...]).astype(BF16)
    a = jnp.dot(h, win_ref[...], preferred_element_type=F32)
    cq = a[:, :qp]
    ckv = a[:, qp:qp + kvp]
    kpe_ref[...] = a[:, qp + kvp:]
    cqn = cq * lax.rsqrt(jnp.sum(cq * cq, axis=-1, keepdims=True) * (1.0 / Q_LORA_RANK) + NORM_EPS)
    ckvn = ckv * lax.rsqrt(jnp.sum(ckv * ckv, axis=-1, keepdims=True) * (1.0 / KV_LORA_RANK)
                           + NORM_EPS)
    q_ref[...] = jnp.dot((cqn * gq_ref[...]).astype(BF16), wq_ref[...], preferred_element_type=F32)
    kv_ref[...] = jnp.dot((ckvn * gkv_ref[...]).astype(BF16), wkv_ref[...],
                          preferred_element_type=F32)


def _mla_proj(x2d, g, modv, layer, row_fn, win_p, gq_p, gkv_p, wq_p, wkv_p, tm):
    t, d = x2d.shape
    qp, kvp = gq_p.shape[1], gkv_p.shape[1]
    na = win_p.shape[1]
    nq, nkv = wq_p.shape[1], wkv_p.shape[1]
    full = lambda arr: pl.BlockSpec(arr.shape, lambda i: (0, 0))
    return pl.pallas_call(
        functools.partial(_mla_proj_kernel, qp=qp, kvp=kvp),
        grid=(t // tm,),
        in_specs=[
            pl.BlockSpec((tm, d), lambda i: (i, 0)),
            pl.BlockSpec((1, d), lambda i: (0, 0)),
            _mod_spec(d, 0, row_fn, layer),
            _mod_spec(d, 1, row_fn, layer),
            full(win_p), full(gq_p), full(gkv_p), full(wq_p), full(wkv_p),
        ],
        out_specs=[pl.BlockSpec((tm, nq), lambda i: (i, 0)),
                   pl.BlockSpec((tm, nkv), lambda i: (i, 0)),
                   pl.BlockSpec((tm, na - qp - kvp), lambda i: (i, 0))],
        out_shape=[jax.ShapeDtypeStruct((t, nq), F32), jax.ShapeDtypeStruct((t, nkv), F32),
                   jax.ShapeDtypeStruct((t, na - qp - kvp), F32)],
        compiler_params=_cparams("arbitrary"),
        name="mla_projections",
    )(x2d, g, modv, modv, win_p, gq_p, gkv_p, wq_p, wkv_p)


def _rope(x, cos, sin_lo, sin_hi):
    quarter = QK_ROPE_DIM // 4
    return (x * cos + pltpu.roll(x, LANES - quarter, 1) * sin_lo
            + pltpu.roll(x, quarter, 1) * sin_hi)


def _headnorm_kernel(q_ref, kv_ref, kpe_ref, gq_ref, gk_ref, cos_ref, slo_ref, shi_ref,
                     qo_ref, ko_ref, vo_ref, *, heads, q_scale):
    inv = 1.0 / (QK_NOPE_DIM + QK_ROPE_DIM)
    cos, slo, shi = cos_ref[...], slo_ref[...], shi_ref[...]
    kpe = kpe_ref[...]
    kpe_ss = jnp.sum(kpe * kpe, axis=-1, keepdims=True)
    gq = gq_ref[...]
    gk = gk_ref[...]
    tm = q_ref.shape[0]
    ones_col = jnp.where(lax.broadcasted_iota(jnp.int32, (tm, V_HEAD_DIM), 1) == 0,
                         1.0, 0.0).astype(BF16)
    for h in range(heads):
        qh = q_ref[:, h * HEAD_SLOT:(h + 1) * HEAD_SLOT]
        rs = lax.rsqrt(jnp.sum(qh * qh, axis=-1, keepdims=True) * inv + NORM_EPS) * q_scale
        qn = qh * rs * gq
        qo_ref[h, :, :LANES] = qn[:, :LANES].astype(BF16)
        qo_ref[h, :, LANES:] = _rope(qn[:, LANES:], cos, slo, shi).astype(BF16)
        kn = kv_ref[:, h * HEAD_SLOT:h * HEAD_SLOT + QK_NOPE_DIM]
        rk = lax.rsqrt((jnp.sum(kn * kn, axis=-1, keepdims=True) + kpe_ss) * inv + NORM_EPS)
        ko_ref[h, :, :LANES] = (kn * rk * gk[:, :LANES]).astype(BF16)
        ko_ref[h, :, LANES:] = _rope(kpe * rk * gk[:, LANES:], cos, slo, shi).astype(BF16)
        vo_ref[h, :, :V_HEAD_DIM] = kv_ref[:, h * HEAD_SLOT + QK_NOPE_DIM:
                                           (h + 1) * HEAD_SLOT].astype(BF16)
        vo_ref[h, :, V_HEAD_DIM:] = ones_col


def _head_norm_rope(q_raw, kv_raw, kpe, gq_slot, gk_slot, cos, slo, shi, bsz, n_ctx, n_lat, tm):
    heads = MLA_HEADS
    n_keys = n_ctx + n_lat
    ctx_tiles = bsz * n_ctx // tm
    per_ctx = n_ctx // tm
    per_lat = n_lat // tm

    def batch_of(i):
        return jnp.where(i < ctx_tiles, i // per_ctx, (i - ctx_tiles) // per_lat)

    def key_blk(i):
        return jnp.where(i < ctx_tiles, i % per_ctx, per_ctx + (i - ctx_tiles) % per_lat)

    tab = pl.BlockSpec((tm, LANES), lambda i: (key_blk(i), 0))
    t = q_raw.shape[0]
    out_map = lambda i: (batch_of(i), 0, key_blk(i), 0)

    def q_map(i):
        blk = jnp.where(i < ctx_tiles, per_lat + i % per_ctx, (i - ctx_tiles) % per_lat)
        return (batch_of(i), 0, blk, 0)

    return pl.pallas_call(
        functools.partial(_headnorm_kernel, heads=heads,
                          q_scale=float(QK_NOPE_DIM + QK_ROPE_DIM) ** -0.5 * math.log2(math.e)),
        grid=(t // tm,),
        in_specs=[
            pl.BlockSpec((tm, heads * HEAD_SLOT), lambda i: (i, 0)),
            pl.BlockSpec((tm, heads * HEAD_SLOT), lambda i: (i, 0)),
            pl.BlockSpec((tm, LANES), lambda i: (i, 0)),
            pl.BlockSpec((1, HEAD_SLOT), lambda i: (0, 0)),
            pl.BlockSpec((1, HEAD_SLOT), lambda i: (0, 0)),
            tab, tab, tab,
        ],
        out_specs=[pl.BlockSpec((None, heads, tm, HEAD_SLOT), q_map),
                   pl.BlockSpec((None, heads, tm, HEAD_SLOT), out_map),
                   pl.BlockSpec((None, heads, tm, 2 * V_HEAD_DIM), out_map)],
        out_shape=[jax.ShapeDtypeStruct((bsz, heads, n_keys, HEAD_SLOT), BF16),
                   jax.ShapeDtypeStruct((bsz, heads, n_keys, HEAD_SLOT), BF16),
                   jax.ShapeDtypeStruct((bsz, heads, n_keys, 2 * V_HEAD_DIM), BF16)],
        compiler_params=_cparams("arbitrary"),
        name="mla_headnorm_rope",
    )(q_raw, kv_raw, kpe, gq_slot, gk_slot, cos, slo, shi)


def _attn_kernel(q_ref, k_ref, v_ref, o_ref, *, sub):
    vd = o_ref.shape[1]
    n_sub = q_ref.shape[0] // sub

    def qk(r):
        return lax.dot_general(q_ref[pl.ds(r * sub, sub), :], k_ref[...],
                               (((1,), (1,)), ((), ())), preferred_element_type=F32)

    s_next = qk(0)
    for r in range(n_sub):
        s = s_next
        if r + 1 < n_sub:
            s_next = qk(r + 1)
        m = jnp.max(s, axis=-1, keepdims=True)
        acc = jnp.dot(jnp.exp2(s - m).astype(BF16), v_ref[...], preferred_element_type=F32)
        o_ref[pl.ds(r * sub, sub), :] = (acc[:, :vd] / acc[:, vd:vd + 1]).astype(o_ref.dtype)


def _attention(q, k, v, n_ctx, tq):
    bsz, heads, n_keys, _ = k.shape
    n_lat = n_keys - n_ctx
    return pl.pallas_call(
        functools.partial(_attn_kernel, sub=min(ATTN_SUB_ROWS, tq)),
        grid=(bsz, heads, n_lat // tq),
        in_specs=[
            pl.BlockSpec((None, None, tq, HEAD_SLOT), lambda b, h, i: (b, h, i, 0)),
            pl.BlockSpec((None, None, n_keys, HEAD_SLOT), lambda b, h, i: (b, h, 0, 0)),
            pl.BlockSpec((None, None, n_keys, 2 * V_HEAD_DIM), lambda b, h, i: (b, h, 0, 0)),
        ],
        out_specs=pl.BlockSpec((None, tq, V_HEAD_DIM), lambda b, h, i: (b, i, h)),
        out_shape=jax.ShapeDtypeStruct((bsz, n_lat, heads * V_HEAD_DIM), BF16),
        compiler_params=_cparams("arbitrary", "arbitrary", "arbitrary"),
        name="mla_attention",
    )(q, k, v)


def _dft_tables(n):
    k = jnp.arange(n, dtype=jnp.int32)
    ang = ((k[:, None] * k[None, :]) % n).astype(F32) * (2.0 * math.pi / n)
    return jnp.cos(ang), jnp.sin(ang)


def _rope_tables(n_ctx, n_lat):
    rows = n_lat // GRID_W
    pairs = QK_ROPE_DIM // 4
    row = jnp.repeat(jnp.arange(rows, dtype=F32), GRID_W)
    col = jnp.tile(jnp.arange(GRID_W, dtype=F32), rows)
    inv = ROPE_THETA ** (-jnp.arange(pairs, dtype=F32) / pairs)
    ang = jnp.stack([row[:, None] * inv, col[:, None] * inv], axis=1)
    cos, sin = jnp.cos(ang), jnp.sin(ang)
    zero = jnp.zeros_like(sin)
    cos_t = jnp.stack([cos, cos], axis=2).reshape(n_lat, QK_ROPE_DIM)
    slo_t = jnp.stack([-sin, zero], axis=2).reshape(n_lat, QK_ROPE_DIM)
    shi_t = jnp.stack([zero, sin], axis=2).reshape(n_lat, QK_ROPE_DIM)

    def full(tab, ctx_val):
        tab = jnp.pad(tab, ((0, 0), (0, LANES - QK_ROPE_DIM)))
        return jnp.concatenate([jnp.full((n_ctx, LANES), ctx_val, F32), tab], axis=0)

    return full(cos_t, 1.0), full(slo_t, 0.0), full(shi_t, 0.0)


def _pad_cols(w, n):
    return jnp.pad(w, ((0, 0), (0, n - w.shape[1])))


def _mla_weights(w_in, g_q_lora, w_q_up, g_kv_lora, w_kv_up, g_q_head, g_k_head):
    qp = -(-Q_LORA_RANK // LANES) * LANES
    kvp = -(-KV_LORA_RANK // LANES) * LANES
    qr, kvr = Q_LORA_RANK, KV_LORA_RANK
    win_p = jnp.concatenate([_pad_cols(w_in[:, :qr], qp), _pad_cols(w_in[:, qr:qr + kvr], kvp),
                             _pad_cols(w_in[:, qr + kvr:], LANES)], axis=1).astype(BF16)
    gq_p = _pad_cols(g_q_lora[None, :], qp)
    gkv_p = _pad_cols(g_kv_lora[None, :], kvp)
    hd = QK_NOPE_DIM + QK_ROPE_DIM
    wq = w_q_up.reshape(qr, MLA_HEADS, hd)
    wq = jnp.pad(wq, ((0, qp - qr), (0, 0), (0, HEAD_SLOT - hd)))
    wq_p = wq.reshape(qp, MLA_HEADS * HEAD_SLOT).astype(BF16)
    wkv_p = jnp.pad(w_kv_up, ((0, kvp - kvr), (0, 0))).astype(BF16)
    gq_slot = _pad_cols(g_q_head[None, :], HEAD_SLOT)
    gk_slot = _pad_cols(g_k_head[None, :], HEAD_SLOT)
    return win_p, gq_p, gkv_p, wq_p, wkv_p, gq_slot, gk_slot


def kernel(x, c, ctx, c_ctx, w_mod, b_mod, g_mix, g_ffn, fourier_w_out, fourier_b_out, mla_w_in, mla_g_q_lora, mla_w_q_up, mla_g_kv_lora, mla_w_kv_up, mla_g_q_head, mla_g_k_head, mla_w_out, router_w, router_b, expert_w_up, expert_b_up, expert_w_down, expert_b_down):
    bsz, n, d = x.shape
    l = ctx.shape[1]
    assert bsz < 8 and n % GRID_W == 0
    tl = min(TOKEN_TILE, l)
    assert l % tl == 0 and n % tl == 0
    ctx_row = bsz

    c8 = jnp.zeros((8, d), F32).at[:bsz].set(c).at[ctx_row].set(c_ctx)
    modv = _mod_vectors(c8, w_mod, b_mod).reshape(w_mod.shape[0], 8, 1, 6 * d)

    def lat_row(tm):
        return lambda i: i // (n // tm)

    def all_row(tm):
        nct = bsz * l // tm
        return lambda i: jnp.where(i < nct, ctx_row, (i - nct) // (n // tm))

    dg = d // FOURIER_GROUPS
    wf = fourier_w_out[0].astype(BF16)
    bf = fourier_b_out[0][None, :]
    g0 = g_mix[0][None, :]

    def fourier(x3, row_fn):
        nn = x3.shape[1]
        n1, n2, m_tab, w2, cs = _fourier_tables(nn, dg)
        tr, ti = _fft_stage1(x3, g0, modv, 0, row_fn, m_tab, n1, n2)
        y = _fft_stage2(tr, ti, w2, cs, 1.0 / math.sqrt(nn * dg))
        return _fft_out(y, wf, bf, x3, modv, 0, row_fn, n1, n2)

    tm_lat = min(DENSE_ROWS, n)
    x_lat1 = fourier(x, lambda b, j: b)
    x_ctx1 = fourier(ctx, lambda b, j: ctx_row)
    x_all = jnp.concatenate([x_ctx1, x_lat1], axis=0)
    n_ctx_tok = bsz * l

    tc = math.gcd(DENSE_ROWS, n_ctx_tok, n)
    f0, route0, cnt0 = _ffn_route(x_all, g_ffn[0][None, :], modv, 0, all_row(tc), router_w[0],
                                  router_b[0][None, :], tc)
    ys0, dest0 = _moe_experts(f0, route0, cnt0, expert_w_up, expert_b_up, expert_w_down,
                              expert_b_down, 0)
    x_all2 = _moe_combine(ys0, dest0, route0, x_all, modv, 0, all_row(tc), tc)

    win_p, gq_p, gkv_p, wq_p, wkv_p, gq_slot, gk_slot = _mla_weights(
        mla_w_in[0], mla_g_q_lora[0], mla_w_q_up[0], mla_g_kv_lora[0], mla_w_kv_up[0],
        mla_g_q_head[0], mla_g_k_head[0])
    q_raw, kv_raw, kpe = _mla_proj(x_all2, g_mix[1][None, :], modv, 1, all_row(tl),
                                   win_p, gq_p, gkv_p, wq_p, wkv_p, tl)
    cos, slo, shi = _rope_tables(l, n)
    qh, kh, vh = _head_norm_rope(q_raw, kv_raw, kpe, gq_slot, gk_slot, cos, slo, shi,
                                 bsz, l, n, tl)
    attn = _attention(qh, kh, vh, l, min(ATTN_Q_ROWS, n))
    zero_bias = jnp.zeros((1, d), F32)
    tm_out = math.gcd(tm_lat, n_ctx_tok)
    x_lat3 = _mm_residual(attn.reshape(bsz * n, -1), mla_w_out[0].astype(BF16), zero_bias,
                          x_all2, n_ctx_tok, modv, 1, 2, lat_row(tm_out), tm_out)

    f1, route1, cnt1 = _ffn_route(x_lat3, g_ffn[1][None, :], modv, 1, lat_row(tc), router_w[1],
                                  router_b[1][None, :], tc)
    ys1, dest1 = _moe_experts(f1, route1, cnt1, expert_w_up, expert_b_up, expert_w_down,
                              expert_b_down, 1)
    out = _moe_combine(ys1, dest1, route1, x_lat3, modv, 1, lat_row(tc), tc)
    return out.reshape(bsz, n, d)
```
